```python
import math
import jax, jax.numpy as jnp
from jax import lax
import numpy as np

D_MODEL = 2048
BATCH = 2
SEQ = 8192
DEPTH = 1

CTX_LEN = 256
GRID_W = 64
S5_WIDTH = 1024
S5_GROUP = 16
S5_GROUPS = S5_WIDTH // S5_GROUP
S5_STATE = 64
HY_WIDTH = 1024
HY_BANDS = 16
HY_POS_DIM = 1 + 2 * HY_BANDS
HY_HIDDEN = 64
HY_DECAY_TARGET = 1e-2
HY_FAST_PCT = 0.3
HY_SLOW_PCT = 1.5
SHORT_CONV = 3
D_FF = -(-8 * D_MODEL // (3 * 256)) * 256
IN_COLS = S5_WIDTH + 3 * HY_WIDTH + 2 * D_MODEL
N_ADA = 6
EPS = 1e-6
DT_MIN = 1e-3
DT_MAX = 1e-1

kernel_name = "hybrid_s5_hyena_prefix_dit_block"


def rms_norm(x, g):
    xf = x.astype(jnp.float32)
    y = xf * lax.rsqrt(jnp.mean(xf * xf, axis=-1, keepdims=True) + EPS)
    return (y * g.astype(jnp.float32)).astype(x.dtype)


def modulate(x, g, shift, scale):
    return rms_norm(x, g) * (1.0 + scale[:, None, :]) + shift[:, None, :]


def sincos_2d(n_rows, dim):
    rows = jnp.repeat(jnp.arange(n_rows, dtype=jnp.float32), GRID_W)
    cols = jnp.tile(jnp.arange(GRID_W, dtype=jnp.float32), n_rows)
    quarter = dim // 4
    omega = 10000.0 ** (-jnp.arange(quarter, dtype=jnp.float32) / quarter)
    ar = rows[:, None] * omega
    ac = cols[:, None] * omega
    return jnp.concatenate([jnp.sin(ar), jnp.cos(ar), jnp.sin(ac), jnp.cos(ac)], axis=-1)


def linear_recurrence(e_i, e_j):
    a_i, b_i = e_i
    a_j, b_j = e_j
    return a_j * a_i, a_j * b_i + b_j


def s5_scan(u, a_re, a_im, log_dt, b_re, b_im, init, reverse):
    f32 = jnp.float32
    lam = lax.complex(a_re.astype(f32), a_im.astype(f32))
    lam_bar = jnp.exp(lam * jnp.exp(log_dt.astype(f32))[:, None])
    b_bar = ((lam_bar - 1.0) / lam)[:, :, None] * lax.complex(b_re.astype(f32), b_im.astype(f32))
    uf = u.astype(f32)
    bu = lax.complex(jnp.einsum("gph,blgh->blgp", b_bar.real, uf),
                     jnp.einsum("gph,blgh->blgp", b_bar.imag, uf))
    n = bu.shape[1]
    if init is not None:
        edge = n - 1 if reverse else 0
        bu = bu.at[:, edge].add(lam_bar * init)
    a = jnp.broadcast_to(lam_bar, (1, n) + lam_bar.shape)
    _, states = lax.associative_scan(linear_recurrence, (a, bu), reverse=reverse, axis=1)
    return states


def s5_states(u, lp, d, init, reverse):
    bsz, n, _ = u.shape
    ug = u.reshape(bsz, n, S5_GROUPS, S5_GROUP)
    return s5_scan(ug, lp["s5_a_re"][d], lp["s5_a_im"][d], lp["s5_log_dt"][d],
                   lp["s5_b_re"][d], lp["s5_b_im"][d], init, reverse)


def s5_readout(states, lp, d):
    f32 = jnp.float32
    return (jnp.einsum("ghp,blgp->blgh", lp["s5_c_re"][d].astype(f32), states.real)
            - jnp.einsum("ghp,blgp->blgh", lp["s5_c_im"][d].astype(f32), states.imag))


def centred_short_conv(u, w, b):
    n = u.shape[1]
    up = jnp.pad(u, ((0, 0), (1, 1), (0, 0)))
    return up[:, :n] * w[0] + up[:, 1:n + 1] * w[1] + up[:, 2:] * w[2] + b


def hyena_filters(n, lp):
    f32 = jnp.float32
    pos = jnp.arange(n, dtype=f32)
    t = pos / float(max(n - 1, 1))
    w = 2.0 * math.pi * pos / n
    bands = jnp.linspace(1e-4, HY_BANDS - 1, HY_BANDS, dtype=f32)
    feats = jnp.concatenate([t[:, None], jnp.cos(w[:, None] * bands), -jnp.sin(w[:, None] * bands)], axis=-1)
    h = jnp.sin(lp["hy_f1_freq"].astype(f32) * (feats @ lp["hy_f1_w"].astype(f32) + lp["hy_f1_b"].astype(f32)))
    h = jnp.sin(lp["hy_f2_freq"].astype(f32) * (h @ lp["hy_f2_w"].astype(f32) + lp["hy_f2_b"].astype(f32)))
    h = (h @ lp["hy_f3_w"].astype(f32)).reshape(n, 2, HY_WIDTH)
    window = jnp.exp(-t[:, None, None] * jnp.abs(lp["hy_decay"].astype(f32))[None])
    return h * window


def bidirectional_long_conv(u, h, bias):
    f32 = jnp.float32
    n = u.shape[1]
    h_fwd, h_bwd = h[:, 0], h[:, 1]
    taps = jnp.concatenate([h_fwd, jnp.zeros((1, HY_WIDTH), f32), h_bwd[:0:-1]], axis=0)
    kf = jnp.fft.rfft(taps, n=2 * n, axis=0)
    uf = jnp.fft.rfft(u.astype(f32), n=2 * n, axis=1)
    y = jnp.fft.irfft(uf * kf[None], n=2 * n, axis=1)[:, :n]
    return (y + u.astype(f32) * bias.astype(f32)).astype(u.dtype)


def hyena_branch(hy, lp):
    z = centred_short_conv(hy, lp["hy_conv_w"], lp["hy_conv_b"])
    x0, x1, v = jnp.split(z, 3, axis=-1)
    h = hyena_filters(hy.shape[1], lp)
    return x0 * bidirectional_long_conv(x1 * v, h, lp["hy_bias"])


def mixer_output(u, y_ssm, hy, gates, lp):
    y = (y_ssm.reshape(u.shape) + u.astype(jnp.float32) * lp["s5_d"].astype(jnp.float32)).astype(u.dtype)
    yg = jax.nn.gelu(y)
    glu_v, glu_g = jnp.split(yg @ lp["w_glu"], 2, axis=-1)
    branch_s5 = glu_v * jax.nn.sigmoid(glu_g)
    branch_hy = hyena_branch(hy, lp) @ lp["w_hy_out"]
    gate_s5, gate_hy = jnp.split(gates, 2, axis=-1)
    merged = jax.nn.sigmoid(gate_s5) * branch_s5 + jax.nn.sigmoid(gate_hy) * branch_hy
    return merged @ lp["w_out"]


def swiglu(h, w_in, w_out):
    a, b = jnp.split(h @ w_in, 2, axis=-1)
    return (jax.nn.silu(a) * b) @ w_out


def setup_inputs(seed: int = 0) -> dict:
    key = jax.random.key(seed)
    ks = iter(jax.random.split(key, 48))
    f32 = jnp.float32

    def nrm(shape, scale):
        return jax.random.normal(next(ks), shape, f32) * scale

    L, G, P, H = DEPTH, S5_GROUPS, S5_STATE, S5_GROUP
    n_idx = jnp.arange(P, dtype=f32)
    decay_base = jnp.linspace(math.log(HY_DECAY_TARGET) / HY_SLOW_PCT,
                              math.log(HY_DECAY_TARGET) / HY_FAST_PCT, HY_WIDTH, dtype=f32)
    return {
        "x": nrm((BATCH, SEQ, D_MODEL), 1.0),
        "c": nrm((BATCH, D_MODEL), 1.0),
        "ctx": nrm((BATCH, CTX_LEN, D_MODEL), 1.0),
        "c_ctx": nrm((D_MODEL,), 0.5),
        "w_ada": nrm((L, D_MODEL, N_ADA * D_MODEL), 0.2 * D_MODEL ** -0.5),
        "b_ada": nrm((L, N_ADA * D_MODEL), 0.01),
        "norm_mix": 1.0 + nrm((L, D_MODEL), 0.02),
        "w_in": nrm((L, D_MODEL, IN_COLS), D_MODEL ** -0.5),
        "s5_a_re": -0.5 + nrm((L, 2, G, P), 0.01),
        "s5_a_im": math.pi * n_idx + nrm((L, 2, G, P), 0.01),
        "s5_log_dt": jax.random.uniform(next(ks), (L, 2, G), f32, math.log(DT_MIN), math.log(DT_MAX)),
        "s5_b_re": nrm((L, 2, G, P, H), (2 * H) ** -0.5),
        "s5_b_im": nrm((L, 2, G, P, H), (2 * H) ** -0.5),
        "s5_c_re": nrm((L, 2, G, H, P), 0.5),
        "s5_c_im": nrm((L, 2, G, H, P), 0.5),
        "s5_d": nrm((L, S5_WIDTH), 1.0),
        "w_glu": nrm((L, S5_WIDTH, 2 * D_MODEL), S5_WIDTH ** -0.5),
        "hy_conv_w": nrm((L, SHORT_CONV, 3 * HY_WIDTH), 0.5),
        "hy_conv_b": nrm((L, 3 * HY_WIDTH), 0.01),
        "hy_f1_w": nrm((L, HY_POS_DIM, HY_HIDDEN), HY_POS_DIM ** -0.5),
        "hy_f1_b": nrm((L, HY_HIDDEN), 0.1),
        "hy_f1_freq": 1.0 + nrm((L, HY_HIDDEN), 0.02),
        "hy_f2_w": nrm((L, HY_HIDDEN, HY_HIDDEN), HY_HIDDEN ** -0.5),
        "hy_f2_b": nrm((L, HY_HIDDEN), 0.1),
        "hy_f2_freq": 1.0 + nrm((L, HY_HIDDEN), 0.02),
        "hy_f3_w": nrm((L, HY_HIDDEN, 2 * HY_WIDTH), 0.05 * HY_HIDDEN ** -0.5),
        "hy_decay": decay_base + nrm((L, 2, HY_WIDTH), 0.1),
        "hy_bias": nrm((L, HY_WIDTH), 1.0),
        "w_hy_out": nrm((L, HY_WIDTH, D_MODEL), HY_WIDTH ** -0.5),
        "w_out": nrm((L, D_MODEL, D_MODEL), D_MODEL ** -0.5),
        "norm_ffn": 1.0 + nrm((L, D_MODEL), 0.02),
        "w_ffn_in": nrm((L, D_MODEL, 2 * D_FF), D_MODEL ** -0.5),
        "w_ffn_out": nrm((L, D_FF, D_MODEL), D_FF ** -0.5),
        "norm_f": 1.0 + nrm((D_MODEL,), 0.02),
    }


def reference(x, c, ctx, c_ctx, w_ada, b_ada, norm_mix, w_in, s5_a_re, s5_a_im, s5_log_dt,
              s5_b_re, s5_b_im, s5_c_re, s5_c_im, s5_d, w_glu, hy_conv_w, hy_conv_b,
              hy_f1_w, hy_f1_b, hy_f1_freq, hy_f2_w, hy_f2_b, hy_f2_freq, hy_f3_w, hy_decay,
              hy_bias, w_hy_out, w_out, norm_ffn, w_ffn_in, w_ffn_out, norm_f):
    n_lat = x.shape[1]
    n_rows = n_lat // GRID_W
    x = x + sincos_2d(n_rows, D_MODEL).astype(x.dtype)[None]
    silu_c = jax.nn.silu(c)
    silu_cc = jax.nn.silu(c_ctx)[None]

    for l in range(DEPTH):
        lp = {
            "s5_a_re": s5_a_re[l], "s5_a_im": s5_a_im[l], "s5_log_dt": s5_log_dt[l],
            "s5_b_re": s5_b_re[l], "s5_b_im": s5_b_im[l],
            "s5_c_re": s5_c_re[l], "s5_c_im": s5_c_im[l],
            "s5_d": s5_d[l], "w_glu": w_glu[l],
            "hy_conv_w": hy_conv_w[l], "hy_conv_b": hy_conv_b[l],
            "hy_f1_w": hy_f1_w[l], "hy_f1_b": hy_f1_b[l], "hy_f1_freq": hy_f1_freq[l],
            "hy_f2_w": hy_f2_w[l], "hy_f2_b": hy_f2_b[l], "hy_f2_freq": hy_f2_freq[l],
            "hy_f3_w": hy_f3_w[l], "hy_decay": hy_decay[l], "hy_bias": hy_bias[l],
            "w_hy_out": w_hy_out[l], "w_out": w_out[l],
        }
        shift_mix, scale_mix, gate_mix, shift_ffn, scale_ffn, gate_ffn = jnp.split(
            silu_c @ w_ada[l] + b_ada[l], N_ADA, axis=-1)
        cshift_mix, cscale_mix, cgate_mix, cshift_ffn, cscale_ffn, cgate_ffn = jnp.split(
            silu_cc @ w_ada[l] + b_ada[l], N_ADA, axis=-1)

        h_c = modulate(ctx, norm_mix[l], cshift_mix, cscale_mix)
        u_c = h_c @ w_in[l][:, :S5_WIDTH]
        st_ctx_f = s5_states(u_c, lp, 0, None, False)
        st_ctx_b = s5_states(u_c, lp, 1, None, True)

        h_x = modulate(x, norm_mix[l], shift_mix, scale_mix)
        p_x = h_x @ w_in[l]
        u_x, hy_x, gates_x = jnp.split(p_x, [S5_WIDTH, S5_WIDTH + 3 * HY_WIDTH], axis=-1)
        y_x = (s5_readout(s5_states(u_x, lp, 0, st_ctx_f[:, -1], False), lp, 0)
               + s5_readout(s5_states(u_x, lp, 1, st_ctx_b[:, 0], True), lp, 1))
        x = x + gate_mix[:, None] * mixer_output(u_x, y_x, hy_x, gates_x, lp)
        x = x + gate_ffn[:, None] * swiglu(modulate(x, norm_ffn[l], shift_ffn, scale_ffn),
                                           w_ffn_in[l], w_ffn_out[l])

        if l < DEPTH - 1:
            hy_c, gates_c = jnp.split(h_c @ w_in[l][:, S5_WIDTH:], [3 * HY_WIDTH], axis=-1)
            y_c = s5_readout(st_ctx_f, lp, 0) + s5_readout(st_ctx_b, lp, 1)
            ctx = ctx + cgate_mix[:, None] * mixer_output(u_c, y_c, hy_c, gates_c, lp)
            ctx = ctx + cgate_ffn[:, None] * swiglu(modulate(ctx, norm_ffn[l], cshift_ffn, cscale_ffn),
                                                    w_ffn_in[l], w_ffn_out[l])

    return rms_norm(x, norm_f)
```

```python
import functools
import math

import jax
import jax.numpy as jnp
import numpy as np
from jax import lax
from jax.experimental import pallas as pl
from jax.experimental.pallas import tpu as pltpu

F32 = jnp.float32
BF16 = jnp.bfloat16
HIGHEST = lax.Precision.HIGHEST

GRID_W = 64
N_ADA = 6
EPS = 1e-6
HY_BANDS = 16
LANES = 128
SUBLANES = 8
DFT_N2 = LANES
VMEM_LIMIT = 56 * 1024 * 1024


def _cparams(sem):
    return pltpu.CompilerParams(dimension_semantics=sem, vmem_limit_bytes=VMEM_LIMIT)


def _dot(a, b):
    return jnp.dot(a, b, preferred_element_type=F32)


def _dot_hi(a, b):
    return jnp.dot(a, b, preferred_element_type=F32, precision=HIGHEST)


def _silu(x):
    return x * jax.nn.sigmoid(x)


def _rms_mod(x, g, shift, scale):
    y = x * lax.rsqrt(jnp.mean(x * x, axis=-1, keepdims=True) + EPS)
    return (y * g) * (1.0 + scale) + shift


def _pos_block(rows_ref, cols_ref, tm):
    nr = tm // GRID_W
    dh = rows_ref.shape[-1]
    rp = jnp.broadcast_to(rows_ref[...][:, None, :], (nr, GRID_W, dh)).reshape(tm, dh)
    cp = jnp.broadcast_to(cols_ref[...][None], (nr, GRID_W, dh)).reshape(tm, dh)
    return jnp.concatenate([rp, cp], axis=-1)


def _ada_kernel(ct_ref, w_ref, b_ref, o_ref, *, n_vec):
    sc = _silu(ct_ref[...])
    w = w_ref[...]
    rows = [jnp.sum(w * sc[:, r:r + 1], axis=0, keepdims=True) for r in range(n_vec)]
    rows += [jnp.zeros_like(rows[0])] * (SUBLANES - n_vec)
    o_ref[...] = jnp.concatenate(rows, axis=0) + b_ref[...]


def _ada(cond_t, w, b, n_vec, tn=1024):
    d, n = w.shape
    return pl.pallas_call(
        functools.partial(_ada_kernel, n_vec=n_vec),
        grid=(n // tn,),
        in_specs=[pl.BlockSpec((d, SUBLANES), lambda j: (0, 0)),
                  pl.BlockSpec((d, tn), lambda j: (0, j)),
                  pl.BlockSpec((1, tn), lambda j: (0, j))],
        out_specs=pl.BlockSpec((SUBLANES, tn), lambda j: (0, j)),
        out_shape=jax.ShapeDtypeStruct((SUBLANES, n), F32),
        compiler_params=_cparams(("arbitrary",)),
        name="ada",
    )(cond_t, w, b)


def _in_kernel(*refs, tm, use_pos):
    if use_pos:
        x_ref, rows_ref, cols_ref, g_ref, sh_ref, sc_ref, w_ref, o_ref, h_ref = refs
    else:
        x_ref, g_ref, sh_ref, sc_ref, w_ref, o_ref, h_ref = refs

    @pl.when(pl.program_id(2) == 0)
    def _():
        x = x_ref[0]
        if use_pos:
            x = x + _pos_block(rows_ref, cols_ref, tm)
        h_ref[...] = _rms_mod(x, g_ref[...], sh_ref[0], sc_ref[0]).astype(BF16)

    o_ref[0] = _dot(h_ref[...], w_ref[...]).astype(o_ref.dtype)


def _in_proj(x, pos_tabs, g, shift, scale, w, n_cols, tm, tn):
    bsz, n, d = x.shape
    use_pos = pos_tabs is not None
    in_specs = [pl.BlockSpec((1, tm, d), lambda b, i, j: (b, i, 0))]
    args = [x]
    if use_pos:
        rows_tab, cols_tab = pos_tabs
        dh = d // 2
        in_specs += [pl.BlockSpec((tm // GRID_W, dh), lambda b, i, j: (i, 0)),
                     pl.BlockSpec((GRID_W, dh), lambda b, i, j: (0, 0))]
        args += [rows_tab, cols_tab]
    in_specs += [pl.BlockSpec((1, d), lambda b, i, j: (0, 0)),
                 pl.BlockSpec((1, 1, d), lambda b, i, j: (b, 0, 0)),
                 pl.BlockSpec((1, 1, d), lambda b, i, j: (b, 0, 0)),
                 pl.BlockSpec((d, tn), lambda b, i, j: (0, j))]
    args += [g, shift, scale, w]
    return pl.pallas_call(
        functools.partial(_in_kernel, tm=tm, use_pos=use_pos),
        grid=(bsz, n // tm, n_cols // tn),
        in_specs=in_specs,
        out_specs=pl.BlockSpec((1, tm, tn), lambda b, i, j: (b, i, j)),
        out_shape=jax.ShapeDtypeStruct((bsz, n, n_cols), BF16),
        scratch_shapes=[pltpu.VMEM((tm, d), BF16)],
        compiler_params=_cparams(("parallel", "parallel", "arbitrary")),
        name="in_proj" if use_pos else "ctx_proj",
    )(*args)


S5_CHUNKS = 8


def _s5_kernel(uc_ref, ux_ref, bd_ref, cd_ref, lr_ref, li_ref, y_ref, s_ref, st_ref,
               *, tb, pitch, nc):
    d = pl.program_id(0)
    i = pl.program_id(2)
    nslab = 2 * (s_ref.shape[0] // 2)
    half = nslab // 2

    @pl.when(i == 0)
    def _():
        st_ref[...] = jnp.zeros_like(st_ref)

    def fill(u_ref):
        for k in range(S5_CHUNKS):
            bu = _dot(u_ref[0, :, k * LANES:(k + 1) * LANES], bd_ref[0, k])
            for m in range(nslab):
                s_ref[m, k * pitch:k * pitch + tb, :] = bu[:, m * LANES:(m + 1) * LANES]

    @pl.when(i < nc)
    def _():
        fill(uc_ref)

    @pl.when(i >= nc)
    def _():
        fill(ux_ref)

    lam_r = [lr_ref[0, m] for m in range(half)]
    lam_i = [li_ref[0, m] for m in range(half)]

    def step(q, carry):
        t = jnp.where(d == 0, q, tb - 1 - q)
        sr, si = carry
        new_r, new_i = [], []
        for m in range(half):
            idx = pl.ds(t, SUBLANES, stride=pitch)
            nr = lam_r[m] * sr[m] - lam_i[m] * si[m] + s_ref[m, idx, :]
            ni = lam_r[m] * si[m] + lam_i[m] * sr[m] + s_ref[half + m, idx, :]
            s_ref[m, idx, :] = nr
            s_ref[half + m, idx, :] = ni
            new_r.append(nr)
            new_i.append(ni)
        return tuple(new_r), tuple(new_i)

    init = (tuple(st_ref[m] for m in range(half)),
            tuple(st_ref[half + m] for m in range(half)))
    fin_r, fin_i = lax.fori_loop(0, tb, step, init)
    for m in range(half):
        st_ref[m] = fin_r[m]
        st_ref[half + m] = fin_i[m]

    @pl.when(i >= nc)
    def _():
        for k in range(S5_CHUNKS):
            sk = jnp.concatenate(
                [s_ref[m, k * pitch:k * pitch + tb, :] for m in range(nslab)], axis=-1)
            y_ref[0, 0, :, k * LANES:(k + 1) * LANES] = _dot(sk.astype(BF16), cd_ref[0, k])


def _s5(u_c, p_x, bd, cd, lam_r, lam_i, s5w, tb):
    bsz, lc, _ = u_c.shape
    n = p_x.shape[1]
    nc, nb = lc // tb, n // tb
    nslab = bd.shape[-1] // LANES
    pitch = tb + SUBLANES

    def blk(d, i, count):
        j = jnp.clip(i, 0, count - 1)
        return jnp.where(d == 0, j, count - 1 - j)

    return pl.pallas_call(
        functools.partial(_s5_kernel, tb=tb, pitch=pitch, nc=nc),
        grid=(2, bsz, nc + nb),
        in_specs=[
            pl.BlockSpec((1, tb, s5w), lambda d, b, i: (b, blk(d, i, nc), 0)),
            pl.BlockSpec((1, tb, s5w), lambda d, b, i: (b, blk(d, i - nc, nb), 0)),
            pl.BlockSpec((1, S5_CHUNKS, LANES, nslab * LANES), lambda d, b, i: (d, 0, 0, 0)),
            pl.BlockSpec((1, S5_CHUNKS, nslab * LANES, LANES), lambda d, b, i: (d, 0, 0, 0)),
            pl.BlockSpec((1, nslab // 2, SUBLANES, LANES), lambda d, b, i: (d, 0, 0, 0)),
            pl.BlockSpec((1, nslab // 2, SUBLANES, LANES), lambda d, b, i: (d, 0, 0, 0)),
        ],
        out_specs=pl.BlockSpec((1, 1, tb, s5w), lambda d, b, i: (d, b, blk(d, i - nc, nb), 0)),
        out_shape=jax.ShapeDtypeStruct((2, bsz, n, s5w), F32),
        scratch_shapes=[pltpu.VMEM((nslab, S5_CHUNKS * pitch, LANES), F32),
                        pltpu.VMEM((nslab, SUBLANES, LANES), F32)],
        compiler_params=_cparams(("arbitrary", "arbitrary", "arbitrary")),
        name="s5",
    )(u_c, p_x, bd, cd, lam_r, lam_i)


def _s5_params(a_re, a_im, log_dt, b_re, b_im, c_re, c_im):
    g, p, h = b_re.shape
    gl = g // S5_CHUNKS
    dt = jnp.exp(log_dt.astype(F32))[:, None]
    mag = jnp.exp(a_re.astype(F32) * dt)
    lr, li = mag * jnp.cos(a_im.astype(F32) * dt), mag * jnp.sin(a_im.astype(F32) * dt)
    den = a_re * a_re + a_im * a_im
    qr = ((lr - 1.0) * a_re + li * a_im) / den
    qi = (li * a_re - (lr - 1.0) * a_im) / den
    bbr = qr[:, :, None] * b_re - qi[:, :, None] * b_im
    bbi = qr[:, :, None] * b_im + qi[:, :, None] * b_re
    eye = jnp.eye(gl, dtype=F32)

    def pack_b(m):
        return jnp.einsum("kgph,gq->kghqp", m.reshape(S5_CHUNKS, gl, p, h), eye).reshape(
            S5_CHUNKS, gl * h, gl * p)

    def pack_c(m):
        return jnp.einsum("kghp,gq->kgpqh", m.reshape(S5_CHUNKS, gl, h, p), eye).reshape(
            S5_CHUNKS, gl * p, gl * h)

    bd = jnp.concatenate([pack_b(bbr), pack_b(bbi)], axis=-1).astype(BF16)
    cd = jnp.concatenate([pack_c(c_re.astype(F32)), -pack_c(c_im.astype(F32))], axis=1).astype(BF16)

    def slabs(v):
        return v.reshape(S5_CHUNKS, -1, LANES).transpose(1, 0, 2)

    return bd, cd, slabs(lr), slabs(li)


def _filt_kernel(f_ref, w1_ref, b1_ref, q1_ref, w2_ref, b2_ref, q2_ref, w3_ref, dec_ref,
                 bias_ref, o_ref, *, tq, n_lat):
    f = f_ref[...]
    h = jnp.sin(q1_ref[...] * (_dot_hi(f, w1_ref[...]) + b1_ref[...]))
    h = jnp.sin(q2_ref[...] * (_dot_hi(h, w2_ref[...]) + b2_ref[...]))
    h = _dot_hi(h, w3_ref[...])
    taps = h * jnp.exp(-f[:, 0:1] * jnp.abs(dec_ref[0]))
    row = pl.program_id(0) * tq + lax.broadcasted_iota(jnp.int32, (tq, 1), 0)
    taps = jnp.where(row == 0, taps + bias_ref[...], taps)
    o_ref[...] = jnp.where(row == n_lat, 0.0, taps)


def _filter_taps(feats, w1, b1, q1, w2, b2, q2, w3, decay, bias, n_lat, tq=512):
    n2, fp = feats.shape
    hid = w2.shape[0]
    c = bias.shape[-1]
    nb = n2 // tq
    half = nb // 2
    const = lambda r: (0, 0)
    return pl.pallas_call(
        functools.partial(_filt_kernel, tq=tq, n_lat=n_lat),
        grid=(nb,),
        in_specs=[pl.BlockSpec((tq, fp), lambda r: (r, 0)),
                  pl.BlockSpec((fp, hid), const), pl.BlockSpec((1, hid), const),
                  pl.BlockSpec((1, hid), const),
                  pl.BlockSpec((hid, hid), const), pl.BlockSpec((1, hid), const),
                  pl.BlockSpec((1, hid), const),
                  pl.BlockSpec((hid, c), lambda r: (0, r // half)),
                  pl.BlockSpec((1, 1, c), lambda r: (r // half, 0, 0)),
                  pl.BlockSpec((1, c), const)],
        out_specs=pl.BlockSpec((tq, c), lambda r: (r, 0)),
        out_shape=jax.ShapeDtypeStruct((n2, c), F32),
        compiler_params=_cparams(("parallel",)),
        name="hy_filter",
    )(feats, w1, b1, q1, w2, b2, q2, w3, decay, bias)


def _sconv_kernel(*refs, tm):
    mains, prevs, nexts = refs[0:3], refs[3:6], refs[6:9]
    w_refs, b_refs = refs[9:12], refs[12:15]
    x0_ref, u_ref = refs[15:17]
    i = pl.program_id(1)
    last = pl.num_programs(1) - 1
    row = lax.broadcasted_iota(jnp.int32, (tm, 1), 0)
    z = []
    for part in range(3):
        cur = mains[part][0].astype(F32)
        halo = prevs[part].shape[1]
        prev_row = jnp.where(i == 0, 0.0, prevs[part][0, halo - 1:halo, :].astype(F32))
        next_row = jnp.where(i == last, 0.0, nexts[part][0, 0:1, :].astype(F32))
        up = jnp.where(row == 0, prev_row, pltpu.roll(cur, 1, 0))
        dn = jnp.where(row == tm - 1, next_row, pltpu.roll(cur, tm - 1, 0))
        w = w_refs[part][...]
        z.append(up * w[0:1] + cur * w[1:2] + dn * w[2:3] + b_refs[part][...])
    x0_ref[0] = z[0].astype(x0_ref.dtype)
    u_ref[0] = z[1] * z[2]


def _short_conv(p_x, conv_w, conv_b, col0, c, tm):
    bsz, n, _ = p_x.shape
    halo = 16
    cb0 = col0 // c
    hb = tm // halo
    nhb = n // halo
    in_specs = []
    for part in range(3):
        in_specs.append(pl.BlockSpec((1, tm, c), lambda b, i, p=part: (b, i, cb0 + p)))
    for part in range(3):
        in_specs.append(pl.BlockSpec(
            (1, halo, c), lambda b, i, p=part: (b, jnp.maximum(i * hb - 1, 0), cb0 + p)))
    for part in range(3):
        in_specs.append(pl.BlockSpec(
            (1, halo, c), lambda b, i, p=part: (b, jnp.minimum((i + 1) * hb, nhb - 1), cb0 + p)))
    for part in range(3):
        in_specs.append(pl.BlockSpec((3, c), lambda b, i, p=part: (0, p)))
    for part in range(3):
        in_specs.append(pl.BlockSpec((1, c), lambda b, i, p=part: (0, p)))
    return pl.pallas_call(
        functools.partial(_sconv_kernel, tm=tm),
        grid=(bsz, n // tm),
        in_specs=in_specs,
        out_specs=[pl.BlockSpec((1, tm, c), lambda b, i: (b, i, 0)),
                   pl.BlockSpec((1, tm, c), lambda b, i: (b, i, 0))],
        out_shape=[jax.ShapeDtypeStruct((bsz, n, c), BF16),
                   jax.ShapeDtypeStruct((bsz, n, c), F32)],
        compiler_params=_cparams(("parallel", "parallel")),
        name="short_conv",
    )(*([p_x] * 9 + [conv_w] * 3 + [conv_b] * 3))


def _dft_tables(n1):
    n = n1 * DFT_N2
    k1 = jnp.arange(n1, dtype=jnp.int32)
    n2 = jnp.arange(DFT_N2, dtype=jnp.int32)
    ph = (k1[None, :, None] * (k1[None, None, :] * DFT_N2 + n2[:, None, None])) % n
    ang = ph.astype(F32) * F32(2.0 * math.pi / n)
    gr, gi = jnp.cos(ang), -jnp.sin(ang)
    h = n1 // 2
    g1c = jnp.concatenate([jnp.concatenate([gr[:, :, :h], -gi[:, :, :h]], axis=2),
                           jnp.concatenate([gi[:, :, :h], gr[:, :, :h]], axis=2)], axis=1)
    g1r = jnp.concatenate([gr, gi], axis=1)
    grt = jnp.swapaxes(gr, 1, 2)[:, :h, :] / n
    git = jnp.swapaxes(gi, 1, 2)[:, :h, :] / n
    g2 = jnp.concatenate([jnp.concatenate([grt, git], axis=2),
                          jnp.concatenate([-git, grt], axis=2)], axis=1)
    ph2 = (n2[:, None] * n2[None, :]) % DFT_N2
    ang2 = ph2.astype(F32) * F32(2.0 * math.pi / DFT_N2)
    fr, fi = jnp.cos(ang2), -jnp.sin(ang2)
    f2 = jnp.concatenate([jnp.concatenate([fr, -fi], axis=1),
                          jnp.concatenate([fi, fr], axis=1)], axis=0)
    f2c = jnp.concatenate([jnp.concatenate([fr, fi], axis=1),
                           jnp.concatenate([-fi, fr], axis=1)], axis=0)
    return (g1c.astype(BF16), g1r.astype(BF16), g2.astype(BF16),
            f2.astype(BF16), f2c.astype(BF16))


def _stage1(gather, g_ref, a_ref, chunk, cn, n1, pitch):
    def body(q, _):
        n2 = chunk * cn + q
        a = _dot(g_ref[q], gather(n2).astype(BF16))
        a_ref[0, pl.ds(n2, n1, stride=pitch), :] = a[:n1]
        a_ref[1, pl.ds(n2, n1, stride=pitch), :] = a[n1:]
        return 0
    lax.fori_loop(0, cn, body, 0)


def _load_a(a_ref, k1, pitch):
    rows = pl.ds(pl.multiple_of(k1 * pitch, SUBLANES), DFT_N2)
    return jnp.concatenate([a_ref[0, rows, :], a_ref[1, rows, :]], axis=0)


def _spec_kernel(t_ref, g_ref, f2_ref, o_ref, a_ref, *, n1, cn, ck, pitch, np1):
    s = pl.program_id(1)

    @pl.when(s < np1)
    def _():
        _stage1(lambda n2: t_ref[pl.ds(n2, n1, stride=DFT_N2), :], g_ref, a_ref, s, cn, n1, pitch)

    @pl.when(s >= np1)
    def _():
        def body(q, _):
            k1 = (s - np1) * ck + q
            x = _dot(f2_ref[...], _load_a(a_ref, k1, pitch).astype(BF16))
            o_ref[pl.ds(pl.multiple_of(q * 2 * DFT_N2, SUBLANES), 2 * DFT_N2), :] = x
            return 0
        lax.fori_loop(0, ck, body, 0)


def _conv_kernel(u_ref, g1_ref, kf_ref, g2_ref, f2_ref, f2c_ref, y_ref, a_ref,
                 *, n1, cn, ck, pitch, np1, np2):
    s = pl.program_id(1)
    h = n1 // 2

    @pl.when(s < np1)
    def _():
        def gather(n2):
            idx = pl.ds(n2, h, stride=DFT_N2)
            return jnp.concatenate([u_ref[0, idx, :], u_ref[1, idx, :]], axis=0)
        _stage1(gather, g1_ref, a_ref, s, cn, n1, pitch)

    @pl.when(jnp.logical_and(s >= np1, s < np1 + np2))
    def _():
        def body(q, _):
            k1 = (s - np1) * ck + q
            x = _dot(f2_ref[...], _load_a(a_ref, k1, pitch).astype(BF16))
            kf = kf_ref[pl.ds(pl.multiple_of(q * 2 * DFT_N2, SUBLANES), 2 * DFT_N2), :]
            xr, xi = x[:DFT_N2], x[DFT_N2:]
            kr, ki = kf[:DFT_N2], kf[DFT_N2:]
            y = jnp.concatenate([xr * kr - xi * ki, xr * ki + xi * kr], axis=0)
            b = _dot(f2c_ref[...], y.astype(BF16))
            rows = pl.ds(pl.multiple_of(k1 * pitch, SUBLANES), DFT_N2)
            a_ref[0, rows, :] = b[:DFT_N2]
            a_ref[1, rows, :] = b[DFT_N2:]
            return 0
        lax.fori_loop(0, ck, body, 0)

    @pl.when(s >= np1 + np2)
    def _():
        def body(q, _):
            n2 = (s - np1 - np2) * cn + q
            idx = pl.ds(n2, n1, stride=pitch)
            b = jnp.concatenate([a_ref[0, idx, :], a_ref[1, idx, :]], axis=0)
            y = _dot(g2_ref[q], b.astype(BF16))
            out = pl.ds(n2, h, stride=DFT_N2)
            y_ref[0, out, :] = y[:h]
            y_ref[1, out, :] = y[h:]
            return 0
        lax.fori_loop(0, cn, body, 0)


def _dft_sizes(n1):
    cn = 16
    ck = min(16, n1)
    return cn, ck, DFT_N2 // cn, n1 // ck, DFT_N2 + SUBLANES


def _filter_spectrum(taps, g1r, f2):
    n, c = taps.shape
    n1 = n // DFT_N2
    cn, ck, np1, np2, pitch = _dft_sizes(n1)
    return pl.pallas_call(
        functools.partial(_spec_kernel, n1=n1, cn=cn, ck=ck, pitch=pitch, np1=np1),
        grid=(c // LANES, np1 + np2),
        in_specs=[pl.BlockSpec((n, LANES), lambda j, s: (0, j), pipeline_mode=pl.Buffered(1)),
                  pl.BlockSpec((cn, 2 * n1, n1), lambda j, s: (jnp.minimum(s, np1 - 1), 0, 0)),
                  pl.BlockSpec((2 * DFT_N2, 2 * DFT_N2), lambda j, s: (0, 0))],
        out_specs=pl.BlockSpec((ck * 2 * DFT_N2, LANES),
                               lambda j, s: (jnp.maximum(s - np1, 0), j)),
        out_shape=jax.ShapeDtypeStruct((n1 * 2 * DFT_N2, c), F32),
        scratch_shapes=[pltpu.VMEM((2, n1 * pitch, LANES), F32)],
        compiler_params=_cparams(("parallel", "arbitrary")),
        name="hy_spectrum",
    )(taps, g1r, f2)


def _long_conv(u, kf, g1c, g2, f2, f2c):
    bsz, n_lat, c = u.shape
    assert bsz == 2, "the two batch elements are packed as one complex signal"
    n1 = 2 * n_lat // DFT_N2
    cn, ck, np1, np2, pitch = _dft_sizes(n1)
    return pl.pallas_call(
        functools.partial(_conv_kernel, n1=n1, cn=cn, ck=ck, pitch=pitch, np1=np1, np2=np2),
        grid=(c // LANES, np1 + np2 + np1),
        in_specs=[
            pl.BlockSpec((2, n_lat, LANES), lambda j, s: (0, 0, j), pipeline_mode=pl.Buffered(1)),
            pl.BlockSpec((cn, 2 * n1, n1), lambda j, s: (jnp.minimum(s, np1 - 1), 0, 0)),
            pl.BlockSpec((ck * 2 * DFT_N2, LANES),
                         lambda j, s: (jnp.clip(s - np1, 0, np2 - 1), j)),
            pl.BlockSpec((cn, n1, 2 * n1),
                         lambda j, s: (jnp.clip(s - np1 - np2, 0, np1 - 1), 0, 0)),
            pl.BlockSpec((2 * DFT_N2, 2 * DFT_N2), lambda j, s: (0, 0)),
            pl.BlockSpec((2 * DFT_N2, 2 * DFT_N2), lambda j, s: (0, 0)),
        ],
        out_specs=pl.BlockSpec((2, n_lat, LANES), lambda j, s: (0, 0, j),
                               pipeline_mode=pl.Buffered(1)),
        out_shape=jax.ShapeDtypeStruct((2, n_lat, c), F32),
        scratch_shapes=[pltpu.VMEM((2, n1 * pitch, LANES), F32)],
        compiler_params=_cparams(("parallel", "arbitrary")),
        name="hy_conv",
    )(u, g1c, kf, g2, f2, f2c)


def _glu_kernel(y_ref, u_ref, d_ref, wv_ref, wg_ref, o_ref, h_ref):
    @pl.when(pl.program_id(2) == 0)
    def _():
        y = y_ref[0, 0] + y_ref[1, 0] + u_ref[0].astype(F32) * d_ref[...]
        h_ref[...] = jax.nn.gelu(y, approximate=True).astype(BF16)

    h = h_ref[...]
    o_ref[0] = (_dot(h, wv_ref[...]) * jax.nn.sigmoid(_dot(h, wg_ref[...]))).astype(o_ref.dtype)


def _glu(y_ssm, p_x, s5_d, w_glu, s5w, d, tm, tn):
    bsz, n = p_x.shape[:2]
    nj = d // tn
    return pl.pallas_call(
        _glu_kernel,
        grid=(bsz, n // tm, nj),
        in_specs=[pl.BlockSpec((2, 1, tm, s5w), lambda b, i, j: (0, b, i, 0)),
                  pl.BlockSpec((1, tm, s5w), lambda b, i, j: (b, i, 0)),
                  pl.BlockSpec((1, s5w), lambda b, i, j: (0, 0)),
                  pl.BlockSpec((s5w, tn), lambda b, i, j: (0, j)),
                  pl.BlockSpec((s5w, tn), lambda b, i, j: (0, nj + j))],
        out_specs=pl.BlockSpec((1, tm, tn), lambda b, i, j: (b, i, j)),
        out_shape=jax.ShapeDtypeStruct((bsz, n, d), BF16),
        scratch_shapes=[pltpu.VMEM((tm, s5w), BF16)],
        compiler_params=_cparams(("parallel", "parallel", "arbitrary")),
        name="s5_glu",
    )(y_ssm, p_x, s5_d, w_glu, w_glu)


def _merge_kernel(x0_ref, yc_ref, w_ref, bs_ref, gs_ref, gh_ref, o_ref, h_ref):
    @pl.when(pl.program_id(2) == 0)
    def _():
        h_ref[...] = (x0_ref[0].astype(F32) * yc_ref[0]).astype(BF16)

    bh = _dot(h_ref[...], w_ref[...])
    merged = (jax.nn.sigmoid(gs_ref[0].astype(F32)) * bs_ref[0].astype(F32)
              + jax.nn.sigmoid(gh_ref[0].astype(F32)) * bh)
    o_ref[0] = merged.astype(o_ref.dtype)


def _merge(x0, y_conv, w_hy_out, branch_s5, p_x, gate_col0, tm, tn):
    bsz, n, c = x0.shape
    d = w_hy_out.shape[1]
    g0 = gate_col0 // tn
    nj = d // tn
    return pl.pallas_call(
        _merge_kernel,
        grid=(bsz, n // tm, nj),
        in_specs=[pl.BlockSpec((1, tm, c), lambda b, i, j: (b, i, 0)),
                  pl.BlockSpec((1, tm, c), lambda b, i, j: (b, i, 0)),
                  pl.BlockSpec((c, tn), lambda b, i, j: (0, j)),
                  pl.BlockSpec((1, tm, tn), lambda b, i, j: (b, i, j)),
                  pl.BlockSpec((1, tm, tn), lambda b, i, j: (b, i, g0 + j)),
                  pl.BlockSpec((1, tm, tn), lambda b, i, j: (b, i, g0 + nj + j))],
        out_specs=pl.BlockSpec((1, tm, tn), lambda b, i, j: (b, i, j)),
        out_shape=jax.ShapeDtypeStruct((bsz, n, d), BF16),
        scratch_shapes=[pltpu.VMEM((tm, c), BF16)],
        compiler_params=_cparams(("parallel", "parallel", "arbitrary")),
        name="merge",
    )(x0, y_conv, w_hy_out, branch_s5, p_x, p_x)


def _out_kernel(m_ref, w_ref, x_ref, rows_ref, cols_ref, gate_ref, o_ref, *, tm):
    xp = x_ref[0] + _pos_block(rows_ref, cols_ref, tm)
    o_ref[0] = xp + gate_ref[0] * _dot(m_ref[0], w_ref[...])


def _out_proj(merged, w_out, x, pos_tabs, gate, tm):
    bsz, n, d = x.shape
    rows_tab, cols_tab = pos_tabs
    dh = d // 2
    return pl.pallas_call(
        functools.partial(_out_kernel, tm=tm),
        grid=(bsz, n // tm),
        in_specs=[pl.BlockSpec((1, tm, d), lambda b, i: (b, i, 0)),
                  pl.BlockSpec((d, d), lambda b, i: (0, 0)),
                  pl.BlockSpec((1, tm, d), lambda b, i: (b, i, 0)),
                  pl.BlockSpec((tm // GRID_W, dh), lambda b, i: (i, 0)),
                  pl.BlockSpec((GRID_W, dh), lambda b, i: (0, 0)),
                  pl.BlockSpec((1, 1, d), lambda b, i: (b, 0, 0))],
        out_specs=pl.BlockSpec((1, tm, d), lambda b, i: (b, i, 0)),
        out_shape=jax.ShapeDtypeStruct((bsz, n, d), F32),
        compiler_params=_cparams(("parallel", "parallel")),
        name="out_proj",
    )(merged, w_out, x, rows_tab, cols_tab, gate)


def _ffn_kernel(x_ref, g_ref, sh_ref, sc_ref, gate_ref, wa_ref, wb_ref, wo_ref, nf_ref,
                o_ref, h_ref, acc_ref):
    j = pl.program_id(2)

    @pl.when(j == 0)
    def _():
        h_ref[...] = _rms_mod(x_ref[0], g_ref[...], sh_ref[0], sc_ref[0]).astype(BF16)
        acc_ref[...] = jnp.zeros_like(acc_ref)

    h = h_ref[...]
    act = _silu(_dot(h, wa_ref[...])) * _dot(h, wb_ref[...])
    acc_ref[...] += _dot(act.astype(BF16), wo_ref[...])

    @pl.when(j == pl.num_programs(2) - 1)
    def _():
        xo = x_ref[0] + gate_ref[0] * acc_ref[...]
        o_ref[0] = xo * lax.rsqrt(jnp.mean(xo * xo, axis=-1, keepdims=True) + EPS) * nf_ref[...]


def _ffn(x, g, shift, scale, gate, w_in, w_out, norm_f, tm, tf):
    bsz, n, d = x.shape
    dff = w_out.shape[0]
    nj = dff // tf
    vec = pl.BlockSpec((1, 1, d), lambda b, i, j: (b, 0, 0))
    return pl.pallas_call(
        _ffn_kernel,
        grid=(bsz, n // tm, nj),
        in_specs=[pl.BlockSpec((1, tm, d), lambda b, i, j: (b, i, 0)),
                  pl.BlockSpec((1, d), lambda b, i, j: (0, 0)),
                  vec, vec, vec,
                  pl.BlockSpec((d, tf), lambda b, i, j: (0, j)),
                  pl.BlockSpec((d, tf), lambda b, i, j: (0, nj + j)),
                  pl.BlockSpec((tf, d), lambda b, i, j: (j, 0)),
                  pl.BlockSpec((1, d), lambda b, i, j: (0, 0))],
        out_specs=pl.BlockSpec((1, tm, d), lambda b, i, j: (b, i, 0)),
        out_shape=jax.ShapeDtypeStruct((bsz, n, d), F32),
        scratch_shapes=[pltpu.VMEM((tm, d), BF16), pltpu.VMEM((tm, d), F32)],
        compiler_params=_cparams(("parallel", "parallel", "arbitrary")),
        name="ffn",
    )(x, g, shift, scale, gate, w_in, w_in, w_out, norm_f)


def _pos_tables(n_rows, d):
    quarter = d // 4
    omega = 10000.0 ** (-jnp.arange(quarter, dtype=F32) / quarter)
    ar = jnp.arange(n_rows, dtype=F32)[:, None] * omega
    ac = jnp.arange(GRID_W, dtype=F32)[:, None] * omega
    return (jnp.concatenate([jnp.sin(ar), jnp.cos(ar)], axis=-1),
            jnp.concatenate([jnp.sin(ac), jnp.cos(ac)], axis=-1))


def _filter_features(n_lat, width):
    r = jnp.arange(2 * n_lat, dtype=jnp.int32)
    pos = jnp.where(r <= n_lat, r, 2 * n_lat - r).astype(F32)
    t = pos / float(max(n_lat - 1, 1))
    w = 2.0 * math.pi * pos / n_lat
    bands = jnp.linspace(1e-4, HY_BANDS - 1, HY_BANDS, dtype=F32)
    feats = jnp.concatenate([t[:, None], jnp.cos(w[:, None] * bands), -jnp.sin(w[:, None] * bands)],
                            axis=-1)
    return jnp.pad(feats, ((0, 0), (0, width - feats.shape[1])))


def kernel(x, c, ctx, c_ctx, w_ada, b_ada, norm_mix, w_in, s5_a_re, s5_a_im, s5_log_dt,
           s5_b_re, s5_b_im, s5_c_re, s5_c_im, s5_d, w_glu, hy_conv_w, hy_conv_b,
           hy_f1_w, hy_f1_b, hy_f1_freq, hy_f2_w, hy_f2_b, hy_f2_freq, hy_f3_w, hy_decay,
           hy_bias, w_hy_out, w_out, norm_ffn, w_ffn_in, w_ffn_out, norm_f):
    bsz, n_lat, d = x.shape
    depth = w_ada.shape[0]
    assert depth == 1, "the context stream is only advanced for the single-layer trunk"
    l = 0
    s5w = s5_d.shape[-1]
    hyw = hy_bias.shape[-1]
    n_ctx = ctx.shape[1]
    tm = 512
    tn = min(1024, d)
    tb = 256
    assert n_ctx % tb == 0 and n_lat % tm == 0 and s5w == S5_CHUNKS * LANES

    pos_tabs = _pos_tables(n_lat // GRID_W, d)

    cond_t = jnp.zeros((d, SUBLANES), F32).at[:, :bsz].set(c.T).at[:, bsz].set(c_ctx)
    ada = _ada(cond_t, w_ada[l], b_ada[l][None], bsz + 1, tn=min(1024, d))

    def vec(row0, rows, part):
        v = ada[row0:row0 + rows, part * d:(part + 1) * d]
        return jnp.broadcast_to(v, (bsz, d))[:, None, :]

    shift_mix, scale_mix, gate_mix = vec(0, bsz, 0), vec(0, bsz, 1), vec(0, bsz, 2)
    shift_ffn, scale_ffn, gate_ffn = vec(0, bsz, 3), vec(0, bsz, 4), vec(0, bsz, 5)
    cshift_mix, cscale_mix = vec(bsz, 1, 0), vec(bsz, 1, 1)

    w_in_b = w_in[l].astype(BF16)
    g_mix = norm_mix[l][None]
    u_c = _in_proj(ctx, None, g_mix, cshift_mix, cscale_mix, w_in_b, s5w, tb, s5w)
    p_x = _in_proj(x, pos_tabs, g_mix, shift_mix, scale_mix, w_in_b, w_in_b.shape[1], tm, tn)

    packed = [_s5_params(s5_a_re[l, k], s5_a_im[l, k], s5_log_dt[l, k], s5_b_re[l, k],
                         s5_b_im[l, k], s5_c_re[l, k], s5_c_im[l, k]) for k in range(2)]
    bd, cd, lam_r, lam_i = (jnp.stack(t) for t in zip(*packed))
    y_ssm = _s5(u_c, p_x, bd, cd, lam_r, lam_i, s5w, tb)

    fp = 64
    feats = _filter_features(n_lat, fp)
    w1 = jnp.pad(hy_f1_w[l], ((0, fp - hy_f1_w.shape[1]), (0, 0)))
    taps = _filter_taps(feats, w1, hy_f1_b[l][None], hy_f1_freq[l][None], hy_f2_w[l],
                        hy_f2_b[l][None], hy_f2_freq[l][None], hy_f3_w[l],
                        hy_decay[l][:, None, :], hy_bias[l][None], n_lat)
    g1c, g1r, g2, f2, f2c = _dft_tables(2 * n_lat // DFT_N2)
    kf = _filter_spectrum(taps, g1r, f2)
    x0, u_hy = _short_conv(p_x, hy_conv_w[l], hy_conv_b[l][None], s5w, hyw, tm)
    y_conv = _long_conv(u_hy, kf, g1c, g2, f2, f2c)

    branch_s5 = _glu(y_ssm, p_x, s5_d[l][None], w_glu[l].astype(BF16), s5w, d, tm, tn)
    merged = _merge(x0, y_conv, w_hy_out[l].astype(BF16), branch_s5, p_x, s5w + 3 * hyw, tm, tn)
    x1 = _out_proj(merged, w_out[l].astype(BF16), x, pos_tabs, gate_mix, tm)

    return _ffn(x1, norm_ffn[l][None], shift_ffn, scale_ffn, gate_ffn,
                w_ffn_in[l].astype(BF16), w_ffn_out[l].astype(BF16), norm_f[None], tm, 512)
```

```python
import functools
import math

import jax
import jax.numpy as jnp
import numpy as np
from jax import lax
from jax.experimental import pallas as pl
from jax.experimental.pallas import tpu as pltpu

F32 = jnp.float32
BF16 = jnp.bfloat16
HIGHEST = lax.Precision.HIGHEST

GRID_W = 64
N_ADA = 6
EPS = 1e-6
HY_BANDS = 16
LANES = 128
SUBLANES = 8
DFT_N2 = LANES
VMEM_LIMIT = 56 * 1024 * 1024


def _cparams(sem):
    return pltpu.CompilerParams(dimension_semantics=sem, vmem_limit_bytes=VMEM_LIMIT)


def _dot(a, b):
    return jnp.dot(a, b, preferred_element_type=F32)


def _dot_hi(a, b):
    return jnp.dot(a, b, preferred_element_type=F32, precision=HIGHEST)


def _silu(x):
    return x * jax.nn.sigmoid(x)


def _rms_mod(x, g, shift, scale):
    y = x * lax.rsqrt(jnp.mean(x * x, axis=-1, keepdims=True) + EPS)
    return (y * g) * (1.0 + scale) + shift


def _pos_block(rows_ref, cols_ref, tm):
    nr = tm // GRID_W
    dh = rows_ref.shape[-1]
    rp = jnp.broadcast_to(rows_ref[...][:, None, :], (nr, GRID_W, dh)).reshape(tm, dh)
    cp = jnp.broadcast_to(cols_ref[...][None], (nr, GRID_W, dh)).reshape(tm, dh)
    return jnp.concatenate([rp, cp], axis=-1)


def _ada_kernel(ct_ref, w_ref, b_ref, o_ref, *, n_vec):
    sc = _silu(ct_ref[...])
    w = w_ref[...]
    rows = [jnp.sum(w * sc[:, r:r + 1], axis=0, keepdims=True) for r in range(n_vec)]
    rows += [jnp.zeros_like(rows[0])] * (SUBLANES - n_vec)
    o_ref[...] = jnp.concatenate(rows, axis=0) + b_ref[...]


def _ada(cond_t, w, b, n_vec, tn=1024):
    d, n = w.shape
    return pl.pallas_call(
        functools.partial(_ada_kernel, n_vec=n_vec),
        grid=(n // tn,),
        in_specs=[pl.BlockSpec((d, SUBLANES), lambda j: (0, 0)),
                  pl.BlockSpec((d, tn), lambda j: (0, j)),
                  pl.BlockSpec((1, tn), lambda j: (0, j))],
        out_specs=pl.BlockSpec((SUBLANES, tn), lambda j: (0, j)),
        out_shape=jax.ShapeDtypeStruct((SUBLANES, n), F32),
        compiler_params=_cparams(("arbitrary",)),
        name="ada",
    )(cond_t, w, b)


def _in_kernel(*refs, tm, use_pos):
    if use_pos:
        x_ref, rows_ref, cols_ref, g_ref, sh_ref, sc_ref, w_ref, o_ref, h_ref = refs
    else:
        x_ref, g_ref, sh_ref, sc_ref, w_ref, o_ref, h_ref = refs

    @pl.when(pl.program_id(2) == 0)
    def _():
        x = x_ref[0]
        if use_pos:
            x = x + _pos_block(rows_ref, cols_ref, tm)
        h_ref[...] = _rms_mod(x, g_ref[...], sh_ref[0], sc_ref[0]).astype(BF16)

    o_ref[0] = _dot(h_ref[...], w_ref[...]).astype(o_ref.dtype)


def _in_proj(x, pos_tabs, g, shift, scale, w, n_cols, tm, tn):
    bsz, n, d = x.shape
    use_pos = pos_tabs is not None
    in_specs = [pl.BlockSpec((1, tm, d), lambda b, i, j: (b, i, 0))]
    args = [x]
    if use_pos:
        rows_tab, cols_tab = pos_tabs
        dh = d // 2
        in_specs += [pl.BlockSpec((tm // GRID_W, dh), lambda b, i, j: (i, 0)),
                     pl.BlockSpec((GRID_W, dh), lambda b, i, j: (0, 0))]
        args += [rows_tab, cols_tab]
    in_specs += [pl.BlockSpec((1, d), lambda b, i, j: (0, 0)),
                 pl.BlockSpec((1, 1, d), lambda b, i, j: (b, 0, 0)),
                 pl.BlockSpec((1, 1, d), lambda b, i, j: (b, 0, 0)),
                 pl.BlockSpec((d, tn), lambda b, i, j: (0, j))]
    args += [g, shift, scale, w]
    return pl.pallas_call(
        functools.partial(_in_kernel, tm=tm, use_pos=use_pos),
        grid=(bsz, n // tm, n_cols // tn),
        in_specs=in_specs,
        out_specs=pl.BlockSpec((1, tm, tn), lambda b, i, j: (b, i, j)),
        out_shape=jax.ShapeDtypeStruct((bsz, n, n_cols), BF16),
        scratch_shapes=[pltpu.VMEM((tm, d), BF16)],
        compiler_params=_cparams(("parallel", "parallel", "arbitrary")),
        name="in_proj" if use_pos else "ctx_proj",
    )(*args)


S5_CHUNKS = 8


def _s5_fill(buf, lhs_of, bd_ref, k, tb, pitch):
    bu = _dot(lhs_of(k), bd_ref[0, k])
    for m in range(buf.shape[0]):
        buf[m, k * pitch:k * pitch + tb, :] = bu[:, m * LANES:(m + 1) * LANES]


def _s5_pass(d, bu_scan, st_scan, carry_ref, st_mm, bu_mm, lhs_of, y_ref, bd_ref, cd_ref,
             lam_r, lam_i, tb, pitch):
    nslab = bu_scan.shape[0]
    half = nslab // 2
    per = tb // S5_CHUNKS
    sr = [carry_ref[m] for m in range(half)]
    si = [carry_ref[half + m] for m in range(half)]
    for k in range(S5_CHUNKS):
        sk = jnp.concatenate(
            [st_mm[m, k * pitch:k * pitch + tb, :] for m in range(nslab)], axis=-1)
        y_ref[0, :, k * LANES:(k + 1) * LANES] = _dot(sk.astype(BF16), cd_ref[0, k])
        _s5_fill(bu_mm, lhs_of, bd_ref, k, tb, pitch)
        for q in range(k * per, (k + 1) * per):
            idx = pl.ds(jnp.where(d == 0, q, tb - 1 - q), SUBLANES, stride=pitch)
            for m in range(half):
                nr = lam_r[m] * sr[m] - lam_i[m] * si[m] + bu_scan[m, idx, :]
                ni = lam_r[m] * si[m] + lam_i[m] * sr[m] + bu_scan[half + m, idx, :]
                st_scan[m, idx, :] = nr
                st_scan[half + m, idx, :] = ni
                sr[m], si[m] = nr, ni
    for m in range(half):
        carry_ref[m] = sr[m]
        carry_ref[half + m] = si[m]


def _s5_kernel(uc0_ref, uca_ref, uxa_ref, ucb_ref, uxb_ref, bd_ref, cd_ref, lr_ref, li_ref,
               ya_ref, yb_ref, a_bu, a_st, b_bu, b_st, ca_ref, cb_ref, *, tb, pitch, nc, nblk):
    d = pl.program_id(0)
    i = pl.program_id(1)
    half = a_bu.shape[0] // 2
    lam_r = [lr_ref[0, m] for m in range(half)]
    lam_i = [li_ref[0, m] for m in range(half)]

    @pl.when(i == 0)
    def _():
        ca_ref[...] = jnp.zeros_like(ca_ref)
        cb_ref[...] = jnp.zeros_like(cb_ref)
        b_st[...] = jnp.zeros_like(b_st)
        for k in range(S5_CHUNKS):
            _s5_fill(a_bu, lambda k: uc0_ref[0, :, k * LANES:(k + 1) * LANES], bd_ref, k, tb, pitch)

    def lhs(uc_ref, ux_ref, pos):
        def of(k):
            cols = slice(k * LANES, (k + 1) * LANES)
            return jnp.where(pos < nc, uc_ref[0, :, cols], ux_ref[0, :, cols])
        return of

    _s5_pass(d, a_bu, a_st, ca_ref, b_st, b_bu, lhs(ucb_ref, uxb_ref, i), yb_ref,
             bd_ref, cd_ref, lam_r, lam_i, tb, pitch)

    @pl.when(i < nblk)
    def _():
        _s5_pass(d, b_bu, b_st, cb_ref, a_st, a_bu, lhs(uca_ref, uxa_ref, i + 1), ya_ref,
                 bd_ref, cd_ref, lam_r, lam_i, tb, pitch)


def _s5(u_c, p_x, bd, cd, lam_r, lam_i, s5w, tb):
    bsz, lc, _ = u_c.shape
    assert bsz == 2, "the two batch elements are the two interleaved sequences"
    n = p_x.shape[1]
    nc, nb = lc // tb, n // tb
    nblk = nc + nb
    nslab = bd.shape[-1] // LANES
    pitch = tb + SUBLANES

    def blk(d, pos, count):
        j = jnp.clip(pos, 0, count - 1)
        return jnp.where(d == 0, j, count - 1 - j)

    def u_spec(b, off, ctx):
        if ctx:
            return pl.BlockSpec((1, tb, s5w), lambda d, i: (b, blk(d, i + off, nc), 0))
        return pl.BlockSpec((1, tb, s5w), lambda d, i: (b, blk(d, i + off - nc, nb), 0))

    def y_spec(off):
        return pl.BlockSpec((1, tb, s5w), lambda d, i: (d, blk(d, i + off - nc, nb), 0))

    par = lambda d, i: (d, 0, 0, 0)
    buf = pltpu.VMEM((nslab, S5_CHUNKS * pitch, LANES), F32)
    state = pltpu.VMEM((nslab, SUBLANES, LANES), F32)
    return pl.pallas_call(
        functools.partial(_s5_kernel, tb=tb, pitch=pitch, nc=nc, nblk=nblk),
        grid=(2, nblk + 1),
        in_specs=[
            pl.BlockSpec((1, tb, s5w), lambda d, i: (0, blk(d, 0, nc), 0)),
            u_spec(0, 1, True), u_spec(0, 1, False),
            u_spec(1, 0, True), u_spec(1, 0, False),
            pl.BlockSpec((1, S5_CHUNKS, LANES, nslab * LANES), par, pipeline_mode=pl.Buffered(1)),
            pl.BlockSpec((1, S5_CHUNKS, nslab * LANES, LANES), par, pipeline_mode=pl.Buffered(1)),
            pl.BlockSpec((1, nslab // 2, SUBLANES, LANES), par),
            pl.BlockSpec((1, nslab // 2, SUBLANES, LANES), par),
        ],
        out_specs=[y_spec(0), y_spec(-1)],
        out_shape=[jax.ShapeDtypeStruct((2, n, s5w), F32)] * 2,
        scratch_shapes=[buf, buf, buf, buf, state, state],
        compiler_params=_cparams(("arbitrary", "arbitrary")),
        name="s5",
    )(u_c, u_c, p_x, u_c, p_x, bd, cd, lam_r, lam_i)


def _s5_params(a_re, a_im, log_dt, b_re, b_im, c_re, c_im):
    g, p, h = b_re.shape
    gl = g // S5_CHUNKS
    dt = jnp.exp(log_dt.astype(F32))[:, None]
    mag = jnp.exp(a_re.astype(F32) * dt)
    lr, li = mag * jnp.cos(a_im.astype(F32) * dt), mag * jnp.sin(a_im.astype(F32) * dt)
    den = a_re * a_re + a_im * a_im
    qr = ((lr - 1.0) * a_re + li * a_im) / den
    qi = (li * a_re - (lr - 1.0) * a_im) / den
    bbr = qr[:, :, None] * b_re - qi[:, :, None] * b_im
    bbi = qr[:, :, None] * b_im + qi[:, :, None] * b_re
    eye = jnp.eye(gl, dtype=F32)

    def pack_b(m):
        return jnp.einsum("kgph,gq->kghqp", m.reshape(S5_CHUNKS, gl, p, h), eye).reshape(
            S5_CHUNKS, gl * h, gl * p)

    def pack_c(m):
        return jnp.einsum("kghp,gq->kgpqh", m.reshape(S5_CHUNKS, gl, h, p), eye).reshape(
            S5_CHUNKS, gl * p, gl * h)

    bd = jnp.concatenate([pack_b(bbr), pack_b(bbi)], axis=-1).astype(BF16)
    cd = jnp.concatenate([pack_c(c_re.astype(F32)), -pack_c(c_im.astype(F32))], axis=1).astype(BF16)

    def slabs(v):
        return v.reshape(S5_CHUNKS, -1, LANES).transpose(1, 0, 2)

    return bd, cd, slabs(lr), slabs(li)


def _filt_kernel(f_ref, w1_ref, b1_ref, q1_ref, w2_ref, b2_ref, q2_ref, w3_ref, dec_ref,
                 bias_ref, o_ref, *, tq, n_lat):
    f = f_ref[...]
    h = jnp.sin(q1_ref[...] * (_dot_hi(f, w1_ref[...]) + b1_ref[...]))
    h = jnp.sin(q2_ref[...] * (_dot_hi(h, w2_ref[...]) + b2_ref[...]))
    h = _dot_hi(h, w3_ref[...])
    taps = h * jnp.exp(-f[:, 0:1] * jnp.abs(dec_ref[0]))
    row = pl.program_id(0) * tq + lax.broadcasted_iota(jnp.int32, (tq, 1), 0)
    taps = jnp.where(row == 0, taps + bias_ref[...], taps)
    o_ref[...] = jnp.where(row == n_lat, 0.0, taps)


def _filter_taps(feats, w1, b1, q1, w2, b2, q2, w3, decay, bias, n_lat, tq=512):
    n2, fp = feats.shape
    hid = w2.shape[0]
    c = bias.shape[-1]
    nb = n2 // tq
    half = nb // 2
    const = lambda r: (0, 0)
    return pl.pallas_call(
        functools.partial(_filt_kernel, tq=tq, n_lat=n_lat),
        grid=(nb,),
        in_specs=[pl.BlockSpec((tq, fp), lambda r: (r, 0)),
                  pl.BlockSpec((fp, hid), const), pl.BlockSpec((1, hid), const),
                  pl.BlockSpec((1, hid), const),
                  pl.BlockSpec((hid, hid), const), pl.BlockSpec((1, hid), const),
                  pl.BlockSpec((1, hid), const),
                  pl.BlockSpec((hid, c), lambda r: (0, r // half)),
                  pl.BlockSpec((1, 1, c), lambda r: (r // half, 0, 0)),
                  pl.BlockSpec((1, c), const)],
        out_specs=pl.BlockSpec((tq, c), lambda r: (r, 0)),
        out_shape=jax.ShapeDtypeStruct((n2, c), F32),
        compiler_params=_cparams(("parallel",)),
        name="hy_filter",
    )(feats, w1, b1, q1, w2, b2, q2, w3, decay, bias)


def _sconv_kernel(*refs, tm):
    mains, prevs, nexts = refs[0:3], refs[3:6], refs[6:9]
    w_refs, b_refs = refs[9:12], refs[12:15]
    x0_ref, u_ref = refs[15:17]
    i = pl.program_id(1)
    last = pl.num_programs(1) - 1
    row = lax.broadcasted_iota(jnp.int32, (tm, 1), 0)
    z = []
    for part in range(3):
        cur = mains[part][0].astype(F32)
        halo = prevs[part].shape[1]
        prev_row = jnp.where(i == 0, 0.0, prevs[part][0, halo - 1:halo, :].astype(F32))
        next_row = jnp.where(i == last, 0.0, nexts[part][0, 0:1, :].astype(F32))
        up = jnp.where(row == 0, prev_row, pltpu.roll(cur, 1, 0))
        dn = jnp.where(row == tm - 1, next_row, pltpu.roll(cur, tm - 1, 0))
        w = w_refs[part][...]
        z.append(up * w[0:1] + cur * w[1:2] + dn * w[2:3] + b_refs[part][...])
    x0_ref[0] = z[0].astype(x0_ref.dtype)
    u_ref[0] = z[1] * z[2]


def _short_conv(p_x, conv_w, conv_b, col0, c, tm):
    bsz, n, _ = p_x.shape
    halo = 16
    cb0 = col0 // c
    hb = tm // halo
    nhb = n // halo
    in_specs = []
    for part in range(3):
        in_specs.append(pl.BlockSpec((1, tm, c), lambda b, i, p=part: (b, i, cb0 + p)))
    for part in range(3):
        in_specs.append(pl.BlockSpec(
            (1, halo, c), lambda b, i, p=part: (b, jnp.maximum(i * hb - 1, 0), cb0 + p)))
    for part in range(3):
        in_specs.append(pl.BlockSpec(
            (1, halo, c), lambda b, i, p=part: (b, jnp.minimum((i + 1) * hb, nhb - 1), cb0 + p)))
    for part in range(3):
        in_specs.append(pl.BlockSpec((3, c), lambda b, i, p=part: (0, p)))
    for part in range(3):
        in_specs.append(pl.BlockSpec((1, c), lambda b, i, p=part: (0, p)))
    return pl.pallas_call(
        functools.partial(_sconv_kernel, tm=tm),
        grid=(bsz, n // tm),
        in_specs=in_specs,
        out_specs=[pl.BlockSpec((1, tm, c), lambda b, i: (b, i, 0)),
                   pl.BlockSpec((1, tm, c), lambda b, i: (b, i, 0))],
        out_shape=[jax.ShapeDtypeStruct((bsz, n, c), BF16),
                   jax.ShapeDtypeStruct((bsz, n, c), F32)],
        compiler_params=_cparams(("parallel", "parallel")),
        name="short_conv",
    )(*([p_x] * 9 + [conv_w] * 3 + [conv_b] * 3))


def _dft_tables(n1):
    n = n1 * DFT_N2
    k1 = jnp.arange(n1, dtype=jnp.int32)
    n2 = jnp.arange(DFT_N2, dtype=jnp.int32)
    ph = (k1[None, :, None] * (k1[None, None, :] * DFT_N2 + n2[:, None, None])) % n
    ang = ph.astype(F32) * F32(2.0 * math.pi / n)
    gr, gi = jnp.cos(ang), -jnp.sin(ang)
    h = n1 // 2
    g1c = jnp.concatenate([jnp.concatenate([gr[:, :, :h], -gi[:, :, :h]], axis=2),
                           jnp.concatenate([gi[:, :, :h], gr[:, :, :h]], axis=2)], axis=1)
    g1r = jnp.concatenate([gr, gi], axis=1)
    grt = jnp.swapaxes(gr, 1, 2)[:, :h, :] / n
    git = jnp.swapaxes(gi, 1, 2)[:, :h, :] / n
    g2 = jnp.concatenate([jnp.concatenate([grt, git], axis=2),
                          jnp.concatenate([-git, grt], axis=2)], axis=1)
    ph2 = (n2[:, None] * n2[None, :]) % DFT_N2
    ang2 = ph2.astype(F32) * F32(2.0 * math.pi / DFT_N2)
    fr, fi = jnp.cos(ang2), -jnp.sin(ang2)
    f2 = jnp.concatenate([jnp.concatenate([fr, -fi], axis=1),
                          jnp.concatenate([fi, fr], axis=1)], axis=0)
    f2c = jnp.concatenate([jnp.concatenate([fr, fi], axis=1),
                           jnp.concatenate([-fi, fr], axis=1)], axis=0)
    return (g1c.astype(BF16), g1r.astype(BF16), g2.astype(BF16),
            f2.astype(BF16), f2c.astype(BF16))


def _stage1(gather, g_ref, a_ref, chunk, cn, n1, pitch):
    for q in range(cn):
        n2 = chunk * cn + q
        a = _dot(g_ref[q], gather(n2).astype(BF16))
        a_ref[0, pl.ds(n2, n1, stride=pitch), :] = a[:n1]
        a_ref[1, pl.ds(n2, n1, stride=pitch), :] = a[n1:]


def _a_rows(chunk, ck, q, pitch):
    return pl.ds(pl.multiple_of(chunk * (ck * pitch), SUBLANES) + q * pitch, DFT_N2)


def _load_a_chunk(a_ref, chunk, ck, pitch):
    tiles = []
    for q in range(ck):
        rows = _a_rows(chunk, ck, q, pitch)
        tiles.append(jnp.concatenate([a_ref[0, rows, :], a_ref[1, rows, :]], axis=0).astype(BF16))
    return jnp.concatenate(tiles, axis=1)


def _spec_kernel(t_ref, g_ref, f2_ref, o_ref, a_ref, *, n1, cn, ck, pitch, np1):
    s = pl.program_id(1)

    @pl.when(s < np1)
    def _():
        _stage1(lambda n2: t_ref[pl.ds(n2, n1, stride=DFT_N2), :], g_ref, a_ref, s, cn, n1, pitch)

    @pl.when(s >= np1)
    def _():
        o_ref[0] = _dot(f2_ref[...], _load_a_chunk(a_ref, s - np1, ck, pitch))


def _conv_kernel(u_ref, g1_ref, kf_ref, g2_ref, f2_ref, f2c_ref, y_ref, a_ref,
                 *, n1, cn, ck, pitch, np1, np2):
    s = pl.program_id(1)
    h = n1 // 2

    @pl.when(s < np1)
    def _():
        def gather(n2):
            idx = pl.ds(n2, h, stride=DFT_N2)
            return jnp.concatenate([u_ref[0, idx, :], u_ref[1, idx, :]], axis=0)
        _stage1(gather, g1_ref, a_ref, s, cn, n1, pitch)

    @pl.when(jnp.logical_and(s >= np1, s < np1 + np2))
    def _():
        chunk = s - np1
        x = _dot(f2_ref[...], _load_a_chunk(a_ref, chunk, ck, pitch))
        xr, xi = x[:DFT_N2], x[DFT_N2:]
        kr, ki = kf_ref[0, :DFT_N2, :], kf_ref[0, DFT_N2:, :]
        y = jnp.concatenate([xr * kr - xi * ki, xr * ki + xi * kr], axis=0)
        b = _dot(f2c_ref[...], y.astype(BF16))
        for q in range(ck):
            rows = _a_rows(chunk, ck, q, pitch)
            a_ref[0, rows, :] = b[:DFT_N2, q * LANES:(q + 1) * LANES]
            a_ref[1, rows, :] = b[DFT_N2:, q * LANES:(q + 1) * LANES]

    @pl.when(s >= np1 + np2)
    def _():
        for q in range(cn):
            n2 = (s - np1 - np2) * cn + q
            idx = pl.ds(n2, n1, stride=pitch)
            b = jnp.concatenate([a_ref[0, idx, :], a_ref[1, idx, :]], axis=0)
            y = _dot(g2_ref[q], b.astype(BF16))
            out = pl.ds(n2, h, stride=DFT_N2)
            y_ref[0, out, :] = y[:h]
            y_ref[1, out, :] = y[h:]


def _dft_sizes(n1):
    cn = 16
    ck = min(16, n1)
    return cn, ck, DFT_N2 // cn, n1 // ck, DFT_N2 + SUBLANES


def _filter_spectrum(taps, g1r, f2):
    n, c = taps.shape
    n1 = n // DFT_N2
    cn, ck, np1, np2, pitch = _dft_sizes(n1)
    return pl.pallas_call(
        functools.partial(_spec_kernel, n1=n1, cn=cn, ck=ck, pitch=pitch, np1=np1),
        grid=(c // LANES, np1 + np2),
        in_specs=[pl.BlockSpec((n, LANES), lambda j, s: (0, j), pipeline_mode=pl.Buffered(1)),
                  pl.BlockSpec((cn, 2 * n1, n1), lambda j, s: (jnp.minimum(s, np1 - 1), 0, 0)),
                  pl.BlockSpec((2 * DFT_N2, 2 * DFT_N2), lambda j, s: (0, 0))],
        out_specs=pl.BlockSpec((1, 2 * DFT_N2, ck * LANES),
                               lambda j, s: (j, 0, jnp.maximum(s - np1, 0))),
        out_shape=jax.ShapeDtypeStruct((c // LANES, 2 * DFT_N2, n1 * LANES), F32),
        scratch_shapes=[pltpu.VMEM((2, n1 * pitch, LANES), F32)],
        compiler_params=_cparams(("parallel", "arbitrary")),
        name="hy_spectrum",
    )(taps, g1r, f2)


def _long_conv(u, kf, g1c, g2, f2, f2c):
    bsz, n_lat, c = u.shape
    assert bsz == 2, "the two batch elements are packed as one complex signal"
    n1 = 2 * n_lat // DFT_N2
    cn, ck, np1, np2, pitch = _dft_sizes(n1)
    return pl.pallas_call(
        functools.partial(_conv_kernel, n1=n1, cn=cn, ck=ck, pitch=pitch, np1=np1, np2=np2),
        grid=(c // LANES, np1 + np2 + np1),
        in_specs=[
            pl.BlockSpec((2, n_lat, LANES), lambda j, s: (0, 0, j), pipeline_mode=pl.Buffered(1)),
            pl.BlockSpec((cn, 2 * n1, n1), lambda j, s: (jnp.minimum(s, np1 - 1), 0, 0)),
            pl.BlockSpec((1, 2 * DFT_N2, ck * LANES),
                         lambda j, s: (j, 0, jnp.clip(s - np1, 0, np2 - 1))),
            pl.BlockSpec((cn, n1, 2 * n1),
                         lambda j, s: (jnp.clip(s - np1 - np2, 0, np1 - 1), 0, 0)),
            pl.BlockSpec((2 * DFT_N2, 2 * DFT_N2), lambda j, s: (0, 0)),
            pl.BlockSpec((2 * DFT_N2, 2 * DFT_N2), lambda j, s: (0, 0)),
        ],
        out_specs=pl.BlockSpec((2, n_lat, LANES), lambda j, s: (0, 0, j),
                               pipeline_mode=pl.Buffered(1)),
        out_shape=jax.ShapeDtypeStruct((2, n_lat, c), F32),
        scratch_shapes=[pltpu.VMEM((2, n1 * pitch, LANES), F32)],
        compiler_params=_cparams(("parallel", "arbitrary")),
        name="hy_conv",
    )(u, g1c, kf, g2, f2, f2c)


def _glu_kernel(ya_ref, yb_ref, u_ref, d_ref, wv_ref, wg_ref, o_ref, h_ref):
    @pl.when(pl.program_id(1) == 0)
    def _():
        for b, y_ref in enumerate((ya_ref, yb_ref)):
            y = y_ref[0] + y_ref[1] + u_ref[b].astype(F32) * d_ref[...]
            h_ref[b] = jax.nn.gelu(y, approximate=True).astype(BF16)

    for b in range(2):
        h = h_ref[b]
        o_ref[b] = (_dot(h, wv_ref[...])
                    * jax.nn.sigmoid(_dot(h, wg_ref[...]))).astype(o_ref.dtype)


def _glu(y_ssm, p_x, s5_d, w_glu, s5w, d, tm, tn):
    bsz, n = p_x.shape[:2]
    nj = d // tn
    y_spec = pl.BlockSpec((2, tm, s5w), lambda i, j: (0, i, 0))
    return pl.pallas_call(
        _glu_kernel,
        grid=(n // tm, nj),
        in_specs=[y_spec, y_spec,
                  pl.BlockSpec((bsz, tm, s5w), lambda i, j: (0, i, 0)),
                  pl.BlockSpec((1, s5w), lambda i, j: (0, 0)),
                  pl.BlockSpec((s5w, tn), lambda i, j: (0, j)),
                  pl.BlockSpec((s5w, tn), lambda i, j: (0, nj + j))],
        out_specs=pl.BlockSpec((bsz, tm, tn), lambda i, j: (0, i, j)),
        out_shape=jax.ShapeDtypeStruct((bsz, n, d), BF16),
        scratch_shapes=[pltpu.VMEM((bsz, tm, s5w), BF16)],
        compiler_params=_cparams(("parallel", "arbitrary")),
        name="s5_glu",
    )(y_ssm[0], y_ssm[1], p_x, s5_d, w_glu, w_glu)


def _merge_kernel(x0_ref, yc_ref, w_ref, bs_ref, gs_ref, gh_ref, o_ref, h_ref):
    @pl.when(pl.program_id(2) == 0)
    def _():
        h_ref[...] = (x0_ref[0].astype(F32) * yc_ref[0]).astype(BF16)

    bh = _dot(h_ref[...], w_ref[...])
    merged = (jax.nn.sigmoid(gs_ref[0].astype(F32)) * bs_ref[0].astype(F32)
              + jax.nn.sigmoid(gh_ref[0].astype(F32)) * bh)
    o_ref[0] = merged.astype(o_ref.dtype)


def _merge(x0, y_conv, w_hy_out, branch_s5, p_x, gate_col0, tm, tn):
    bsz, n, c = x0.shape
    d = w_hy_out.shape[1]
    g0 = gate_col0 // tn
    nj = d // tn
    return pl.pallas_call(
        _merge_kernel,
        grid=(bsz, n // tm, nj),
        in_specs=[pl.BlockSpec((1, tm, c), lambda b, i, j: (b, i, 0)),
                  pl.BlockSpec((1, tm, c), lambda b, i, j: (b, i, 0)),
                  pl.BlockSpec((c, tn), lambda b, i, j: (0, j)),
                  pl.BlockSpec((1, tm, tn), lambda b, i, j: (b, i, j)),
                  pl.BlockSpec((1, tm, tn), lambda b, i, j: (b, i, g0 + j)),
                  pl.BlockSpec((1, tm, tn), lambda b, i, j: (b, i, g0 + nj + j))],
        out_specs=pl.BlockSpec((1, tm, tn), lambda b, i, j: (b, i, j)),
        out_shape=jax.ShapeDtypeStruct((bsz, n, d), BF16),
        scratch_shapes=[pltpu.VMEM((tm, c), BF16)],
        compiler_params=_cparams(("parallel", "parallel", "arbitrary")),
        name="merge",
    )(x0, y_conv, w_hy_out, branch_s5, p_x, p_x)


def _out_kernel(m_ref, w_ref, x_ref, rows_ref, cols_ref, gate_ref, o_ref, *, tm):
    xp = x_ref[0] + _pos_block(rows_ref, cols_ref, tm)
    o_ref[0] = xp + gate_ref[0] * _dot(m_ref[0], w_ref[...])


def _out_proj(merged, w_out, x, pos_tabs, gate, tm):
    bsz, n, d = x.shape
    rows_tab, cols_tab = pos_tabs
    dh = d // 2
    return pl.pallas_call(
        functools.partial(_out_kernel, tm=tm),
        grid=(bsz, n // tm),
        in_specs=[pl.BlockSpec((1, tm, d), lambda b, i: (b, i, 0)),
                  pl.BlockSpec((d, d), lambda b, i: (0, 0)),
                  pl.BlockSpec((1, tm, d), lambda b, i: (b, i, 0)),
                  pl.BlockSpec((tm // GRID_W, dh), lambda b, i: (i, 0)),
                  pl.BlockSpec((GRID_W, dh), lambda b, i: (0, 0)),
                  pl.BlockSpec((1, 1, d), lambda b, i: (b, 0, 0))],
        out_specs=pl.BlockSpec((1, tm, d), lambda b, i: (b, i, 0)),
        out_shape=jax.ShapeDtypeStruct((bsz, n, d), F32),
        compiler_params=_cparams(("parallel", "parallel")),
        name="out_proj",
    )(merged, w_out, x, rows_tab, cols_tab, gate)


def _ffn_kernel(x_ref, g_ref, sh_ref, sc_ref, gate_ref, wa_ref, wb_ref, wo_ref, nf_ref,
                o_ref, h_ref, acc_ref):
    j = pl.program_id(2)

    @pl.when(j == 0)
    def _():
        h_ref[...] = _rms_mod(x_ref[0], g_ref[...], sh_ref[0], sc_ref[0]).astype(BF16)
        acc_ref[...] = jnp.zeros_like(acc_ref)

    h = h_ref[...]
    act = _silu(_dot(h, wa_ref[...])) * _dot(h, wb_ref[...])
    acc_ref[...] += _dot(act.astype(BF16), wo_ref[...])

    @pl.when(j == pl.num_programs(2) - 1)
    def _():
        xo = x_ref[0] + gate_ref[0] * acc_ref[...]
        o_ref[0] = xo * lax.rsqrt(jnp.mean(xo * xo, axis=-1, keepdims=True) + EPS) * nf_ref[...]


def _ffn(x, g, shift, scale, gate, w_in, w_out, norm_f, tm, tf):
    bsz, n, d = x.shape
    dff = w_out.shape[0]
    nj = dff // tf
    vec = pl.BlockSpec((1, 1, d), lambda b, i, j: (b, 0, 0))
    return pl.pallas_call(
        _ffn_kernel,
        grid=(bsz, n // tm, nj),
        in_specs=[pl.BlockSpec((1, tm, d), lambda b, i, j: (b, i, 0)),
                  pl.BlockSpec((1, d), lambda b, i, j: (0, 0)),
                  vec, vec, vec,
                  pl.BlockSpec((d, tf), lambda b, i, j: (0, j)),
                  pl.BlockSpec((d, tf), lambda b, i, j: (0, nj + j)),
                  pl.BlockSpec((tf, d), lambda b, i, j: (j, 0)),
                  pl.BlockSpec((1, d), lambda b, i, j: (0, 0))],
        out_specs=pl.BlockSpec((1, tm, d), lambda b, i, j: (b, i, 0)),
        out_shape=jax.ShapeDtypeStruct((bsz, n, d), F32),
        scratch_shapes=[pltpu.VMEM((tm, d), BF16), pltpu.VMEM((tm, d), F32)],
        compiler_params=_cparams(("parallel", "parallel", "arbitrary")),
        name="ffn",
    )(x, g, shift, scale, gate, w_in, w_in, w_out, norm_f)


def _pos_tables(n_rows, d):
    quarter = d // 4
    omega = 10000.0 ** (-jnp.arange(quarter, dtype=F32) / quarter)
    ar = jnp.arange(n_rows, dtype=F32)[:, None] * omega
    ac = jnp.arange(GRID_W, dtype=F32)[:, None] * omega
    return (jnp.concatenate([jnp.sin(ar), jnp.cos(ar)], axis=-1),
            jnp.concatenate([jnp.sin(ac), jnp.cos(ac)], axis=-1))


def _filter_features(n_lat, width):
    r = jnp.arange(2 * n_lat, dtype=jnp.int32)
    pos = jnp.where(r <= n_lat, r, 2 * n_lat - r).astype(F32)
    t = pos / float(max(n_lat - 1, 1))
    w = 2.0 * math.pi * pos / n_lat
    bands = jnp.linspace(1e-4, HY_BANDS - 1, HY_BANDS, dtype=F32)
    feats = jnp.concatenate([t[:, None], jnp.cos(w[:, None] * bands), -jnp.sin(w[:, None] * bands)],
                            axis=-1)
    return jnp.pad(feats, ((0, 0), (0, width - feats.shape[1])))


def kernel(x, c, ctx, c_ctx, w_ada, b_ada, norm_mix, w_in, s5_a_re, s5_a_im, s5_log_dt,
           s5_b_re, s5_b_im, s5_c_re, s5_c_im, s5_d, w_glu, hy_conv_w, hy_conv_b,
           hy_f1_w, hy_f1_b, hy_f1_freq, hy_f2_w, hy_f2_b, hy_f2_freq, hy_f3_w, hy_decay,
           hy_bias, w_hy_out, w_out, norm_ffn, w_ffn_in, w_ffn_out, norm_f):
    bsz, n_lat, d = x.shape
    depth = w_ada.shape[0]
    assert depth == 1, "the context stream is only advanced for the single-layer trunk"
    l = 0
    s5w = s5_d.shape[-1]
    hyw = hy_bias.shape[-1]
    n_ctx = ctx.shape[1]
    tm = 512
    tn = min(1024, d)
    tb = 256
    assert n_ctx % tb == 0 and n_lat % tm == 0 and s5w == S5_CHUNKS * LANES

    pos_tabs = _pos_tables(n_lat // GRID_W, d)

    cond_t = jnp.zeros((d, SUBLANES), F32).at[:, :bsz].set(c.T).at[:, bsz].set(c_ctx)
    ada = _ada(cond_t, w_ada[l], b_ada[l][None], bsz + 1, tn=min(1024, d))

    def vec(row0, rows, part):
        v = ada[row0:row0 + rows, part * d:(part + 1) * d]
        return jnp.broadcast_to(v, (bsz, d))[:, None, :]

    shift_mix, scale_mix, gate_mix = vec(0, bsz, 0), vec(0, bsz, 1), vec(0, bsz, 2)
    shift_ffn, scale_ffn, gate_ffn = vec(0, bsz, 3), vec(0, bsz, 4), vec(0, bsz, 5)
    cshift_mix, cscale_mix = vec(bsz, 1, 0), vec(bsz, 1, 1)

    w_in_b = w_in[l].astype(BF16)
    g_mix = norm_mix[l][None]
    u_c = _in_proj(ctx, None, g_mix, cshift_mix, cscale_mix, w_in_b, s5w, tb, s5w)
    p_x = _in_proj(x, pos_tabs, g_mix, shift_mix, scale_mix, w_in_b, w_in_b.shape[1], tm, tn)

    packed = [_s5_params(s5_a_re[l, k], s5_a_im[l, k], s5_log_dt[l, k], s5_b_re[l, k],
                         s5_b_im[l, k], s5_c_re[l, k], s5_c_im[l, k]) for k in range(2)]
    bd, cd, lam_r, lam_i = (jnp.stack(t) for t in zip(*packed))
    y_ssm = _s5(u_c, p_x, bd, cd, lam_r, lam_i, s5w, tb)

    fp = 64
    feats = _filter_features(n_lat, fp)
    w1 = jnp.pad(hy_f1_w[l], ((0, fp - hy_f1_w.shape[1]), (0, 0)))
    taps = _filter_taps(feats, w1, hy_f1_b[l][None], hy_f1_freq[l][None], hy_f2_w[l],
                        hy_f2_b[l][None], hy_f2_freq[l][None], hy_f3_w[l],
                        hy_decay[l][:, None, :], hy_bias[l][None], n_lat)
    g1c, g1r, g2, f2, f2c = _dft_tables(2 * n_lat // DFT_N2)
    kf = _filter_spectrum(taps, g1r, f2)
    x0, u_hy = _short_conv(p_x, hy_conv_w[l], hy_conv_b[l][None], s5w, hyw, tm)
    y_conv = _long_conv(u_hy, kf, g1c, g2, f2, f2c)

    branch_s5 = _glu(y_ssm, p_x, s5_d[l][None], w_glu[l].astype(BF16), s5w, d, tm, tn)
    merged = _merge(x0, y_conv, w_hy_out[l].astype(BF16), branch_s5, p_x, s5w + 3 * hyw, tm, tn)
    x1 = _out_proj(merged, w_out[l].astype(BF16), x, pos_tabs, gate_mix, tm)

    return _ffn(x1, norm_ffn[l][None], shift_ffn, scale_ffn, gate_ffn,
                w_ffn_in[l].astype(BF16), w_ffn_out[l].astype(BF16), norm_f[None], tm, 512)
```

```python
import functools
import math

import jax
import jax.numpy as jnp
import numpy as np
from jax import lax
from jax.experimental import pallas as pl
from jax.experimental.pallas import tpu as pltpu

F32 = jnp.float32
BF16 = jnp.bfloat16
HIGHEST = lax.Precision.HIGHEST

GRID_W = 64
N_ADA = 6
EPS = 1e-6
HY_BANDS = 16
LANES = 128
SUBLANES = 8
DFT_N2 = LANES
VMEM_LIMIT = 56 * 1024 * 1024


def _cparams(sem):
    return pltpu.CompilerParams(dimension_semantics=sem, vmem_limit_bytes=VMEM_LIMIT)


def _dot(a, b):
    return jnp.dot(a, b, preferred_element_type=F32)


def _dot_hi(a, b):
    return jnp.dot(a, b, preferred_element_type=F32, precision=HIGHEST)


def _silu(x):
    return x * jax.nn.sigmoid(x)


def _rms_mod(x, g, shift, scale):
    y = x * lax.rsqrt(jnp.mean(x * x, axis=-1, keepdims=True) + EPS)
    return (y * g) * (1.0 + scale) + shift


def _pos_block(rows_ref, cols_ref, tm):
    nr = tm // GRID_W
    dh = rows_ref.shape[-1]
    rp = jnp.broadcast_to(rows_ref[...][:, None, :], (nr, GRID_W, dh)).reshape(tm, dh)
    cp = jnp.broadcast_to(cols_ref[...][None], (nr, GRID_W, dh)).reshape(tm, dh)
    return jnp.concatenate([rp, cp], axis=-1)


def _ada_kernel(ct_ref, w_ref, b_ref, o_ref, *, n_vec):
    sc = _silu(ct_ref[...])
    w = w_ref[...]
    rows = [jnp.sum(w * sc[:, r:r + 1], axis=0, keepdims=True) for r in range(n_vec)]
    rows += [jnp.zeros_like(rows[0])] * (SUBLANES - n_vec)
    o_ref[...] = jnp.concatenate(rows, axis=0) + b_ref[...]


def _ada(cond_t, w, b, n_vec, tn=1024):
    d, n = w.shape
    return pl.pallas_call(
        functools.partial(_ada_kernel, n_vec=n_vec),
        grid=(n // tn,),
        in_specs=[pl.BlockSpec((d, SUBLANES), lambda j: (0, 0)),
                  pl.BlockSpec((d, tn), lambda j: (0, j)),
                  pl.BlockSpec((1, tn), lambda j: (0, j))],
        out_specs=pl.BlockSpec((SUBLANES, tn), lambda j: (0, j)),
        out_shape=jax.ShapeDtypeStruct((SUBLANES, n), F32),
        compiler_params=_cparams(("arbitrary",)),
        name="ada",
    )(cond_t, w, b)


def _in_kernel(*refs, tm, use_pos):
    if use_pos:
        x_ref, rows_ref, cols_ref, g_ref, sh_ref, sc_ref, w_ref, o_ref, h_ref = refs
    else:
        x_ref, g_ref, sh_ref, sc_ref, w_ref, o_ref, h_ref = refs

    @pl.when(pl.program_id(2) == 0)
    def _():
        x = x_ref[0]
        if use_pos:
            x = x + _pos_block(rows_ref, cols_ref, tm)
        h_ref[...] = _rms_mod(x, g_ref[...], sh_ref[0], sc_ref[0]).astype(BF16)

    o_ref[0] = _dot(h_ref[...], w_ref[...]).astype(o_ref.dtype)


def _in_proj(x, pos_tabs, g, shift, scale, w, n_cols, tm, tn):
    bsz, n, d = x.shape
    use_pos = pos_tabs is not None
    in_specs = [pl.BlockSpec((1, tm, d), lambda b, i, j: (b, i, 0))]
    args = [x]
    if use_pos:
        rows_tab, cols_tab = pos_tabs
        dh = d // 2
        in_specs += [pl.BlockSpec((tm // GRID_W, dh), lambda b, i, j: (i, 0)),
                     pl.BlockSpec((GRID_W, dh), lambda b, i, j: (0, 0))]
        args += [rows_tab, cols_tab]
    in_specs += [pl.BlockSpec((1, d), lambda b, i, j: (0, 0)),
                 pl.BlockSpec((1, 1, d), lambda b, i, j: (b, 0, 0)),
                 pl.BlockSpec((1, 1, d), lambda b, i, j: (b, 0, 0)),
                 pl.BlockSpec((d, tn), lambda b, i, j: (0, j))]
    args += [g, shift, scale, w]
    return pl.pallas_call(
        functools.partial(_in_kernel, tm=tm, use_pos=use_pos),
        grid=(bsz, n // tm, n_cols // tn),
        in_specs=in_specs,
        out_specs=pl.BlockSpec((1, tm, tn), lambda b, i, j: (b, i, j)),
        out_shape=jax.ShapeDtypeStruct((bsz, n, n_cols), BF16),
        scratch_shapes=[pltpu.VMEM((tm, d), BF16)],
        compiler_params=_cparams(("parallel", "parallel", "arbitrary")),
        name="in_proj" if use_pos else "ctx_proj",
    )(*args)


S5_CHUNKS = 8


def _s5_fill(buf, lhs_of, bd_ref, k, tb, pitch):
    bu = _dot(lhs_of(k), bd_ref[0, k])
    for m in range(buf.shape[0]):
        buf[m, k * pitch:k * pitch + tb, :] = bu[:, m * LANES:(m + 1) * LANES]


def _s5_pass(d, bu_scan, st_scan, carry_ref, st_mm, bu_mm, lhs_of, y_ref, bd_ref, cd_ref,
             lam_r, lam_i, tb, pitch):
    nslab = bu_scan.shape[0]
    half = nslab // 2
    per = tb // S5_CHUNKS
    sr = [carry_ref[m] for m in range(half)]
    si = [carry_ref[half + m] for m in range(half)]
    for k in range(S5_CHUNKS):
        sk = jnp.concatenate(
            [st_mm[m, k * pitch:k * pitch + tb, :] for m in range(nslab)], axis=-1)
        y_ref[0, :, k * LANES:(k + 1) * LANES] = _dot(sk.astype(BF16), cd_ref[0, k])
        _s5_fill(bu_mm, lhs_of, bd_ref, k, tb, pitch)
        for q in range(k * per, (k + 1) * per):
            idx = pl.ds(jnp.where(d == 0, q, tb - 1 - q), SUBLANES, stride=pitch)
            for m in range(half):
                nr = lam_r[m] * sr[m] - lam_i[m] * si[m] + bu_scan[m, idx, :]
                ni = lam_r[m] * si[m] + lam_i[m] * sr[m] + bu_scan[half + m, idx, :]
                st_scan[m, idx, :] = nr
                st_scan[half + m, idx, :] = ni
                sr[m], si[m] = nr, ni
    for m in range(half):
        carry_ref[m] = sr[m]
        carry_ref[half + m] = si[m]


def _s5_kernel(uc0_ref, uca_ref, uxa_ref, ucb_ref, uxb_ref, bd_ref, cd_ref, lr_ref, li_ref,
               ya_ref, yb_ref, a_bu, a_st, b_bu, b_st, ca_ref, cb_ref, *, tb, pitch, nc, nblk):
    d = pl.program_id(0)
    i = pl.program_id(1)
    half = a_bu.shape[0] // 2
    lam_r = [lr_ref[0, m] for m in range(half)]
    lam_i = [li_ref[0, m] for m in range(half)]

    @pl.when(i == 0)
    def _():
        ca_ref[...] = jnp.zeros_like(ca_ref)
        cb_ref[...] = jnp.zeros_like(cb_ref)
        b_st[...] = jnp.zeros_like(b_st)
        for k in range(S5_CHUNKS):
            _s5_fill(a_bu, lambda k: uc0_ref[0, :, k * LANES:(k + 1) * LANES], bd_ref, k, tb, pitch)

    def lhs(uc_ref, ux_ref, pos):
        def of(k):
            cols = slice(k * LANES, (k + 1) * LANES)
            return jnp.where(pos < nc, uc_ref[0, :, cols], ux_ref[0, :, cols])
        return of

    _s5_pass(d, a_bu, a_st, ca_ref, b_st, b_bu, lhs(ucb_ref, uxb_ref, i), yb_ref,
             bd_ref, cd_ref, lam_r, lam_i, tb, pitch)

    @pl.when(i < nblk)
    def _():
        _s5_pass(d, b_bu, b_st, cb_ref, a_st, a_bu, lhs(uca_ref, uxa_ref, i + 1), ya_ref,
                 bd_ref, cd_ref, lam_r, lam_i, tb, pitch)


def _s5(u_c, p_x, bd, cd, lam_r, lam_i, s5w, tb):
    bsz, lc, _ = u_c.shape
    assert bsz == 2, "the two batch elements are the two interleaved sequences"
    n = p_x.shape[1]
    nc, nb = lc // tb, n // tb
    nblk = nc + nb
    nslab = bd.shape[-1] // LANES
    pitch = tb + SUBLANES

    def blk(d, pos, count):
        j = jnp.clip(pos, 0, count - 1)
        return jnp.where(d == 0, j, count - 1 - j)

    def u_spec(b, off, ctx):
        if ctx:
            return pl.BlockSpec((1, tb, s5w), lambda d, i: (b, blk(d, i + off, nc), 0))
        return pl.BlockSpec((1, tb, s5w), lambda d, i: (b, blk(d, i + off - nc, nb), 0))

    def y_spec(off):
        return pl.BlockSpec((1, tb, s5w), lambda d, i: (d, blk(d, i + off - nc, nb), 0))

    par = lambda d, i: (d, 0, 0, 0)
    buf = pltpu.VMEM((nslab, S5_CHUNKS * pitch, LANES), F32)
    state = pltpu.VMEM((nslab, SUBLANES, LANES), F32)
    return pl.pallas_call(
        functools.partial(_s5_kernel, tb=tb, pitch=pitch, nc=nc, nblk=nblk),
        grid=(2, nblk + 1),
        in_specs=[
            pl.BlockSpec((1, tb, s5w), lambda d, i: (0, blk(d, 0, nc), 0)),
            u_spec(0, 1, True), u_spec(0, 1, False),
            u_spec(1, 0, True), u_spec(1, 0, False),
            pl.BlockSpec((1, S5_CHUNKS, LANES, nslab * LANES), par, pipeline_mode=pl.Buffered(1)),
            pl.BlockSpec((1, S5_CHUNKS, nslab * LANES, LANES), par, pipeline_mode=pl.Buffered(1)),
            pl.BlockSpec((1, nslab // 2, SUBLANES, LANES), par),
            pl.BlockSpec((1, nslab // 2, SUBLANES, LANES), par),
        ],
        out_specs=[y_spec(0), y_spec(-1)],
        out_shape=[jax.ShapeDtypeStruct((2, n, s5w), F32)] * 2,
        scratch_shapes=[buf, buf, buf, buf, state, state],
        compiler_params=_cparams(("arbitrary", "arbitrary")),
        name="s5",
    )(u_c, u_c, p_x, u_c, p_x, bd, cd, lam_r, lam_i)


def _s5_params(a_re, a_im, log_dt, b_re, b_im, c_re, c_im):
    g, p, h = b_re.shape
    gl = g // S5_CHUNKS
    dt = jnp.exp(log_dt.astype(F32))[:, None]
    mag = jnp.exp(a_re.astype(F32) * dt)
    lr, li = mag * jnp.cos(a_im.astype(F32) * dt), mag * jnp.sin(a_im.astype(F32) * dt)
    den = a_re * a_re + a_im * a_im
    qr = ((lr - 1.0) * a_re + li * a_im) / den
    qi = (li * a_re - (lr - 1.0) * a_im) / den
    bbr = qr[:, :, None] * b_re - qi[:, :, None] * b_im
    bbi = qr[:, :, None] * b_im + qi[:, :, None] * b_re
    eye = jnp.eye(gl, dtype=F32)

    def pack_b(m):
        return jnp.einsum("kgph,gq->kghqp", m.reshape(S5_CHUNKS, gl, p, h), eye).reshape(
            S5_CHUNKS, gl * h, gl * p)

    def pack_c(m):
        return jnp.einsum("kghp,gq->kgpqh", m.reshape(S5_CHUNKS, gl, h, p), eye).reshape(
            S5_CHUNKS, gl * p, gl * h)

    bd = jnp.concatenate([pack_b(bbr), pack_b(bbi)], axis=-1).astype(BF16)
    cd = jnp.concatenate([pack_c(c_re.astype(F32)), -pack_c(c_im.astype(F32))], axis=1).astype(BF16)

    def slabs(v):
        return v.reshape(S5_CHUNKS, -1, LANES).transpose(1, 0, 2)

    return bd, cd, slabs(lr), slabs(li)


def _filt_kernel(f_ref, w1_ref, b1_ref, q1_ref, w2_ref, b2_ref, q2_ref, w3_ref, dec_ref,
                 bias_ref, o_ref, *, tq, n_lat):
    f = f_ref[...]
    h = jnp.sin(q1_ref[...] * (_dot_hi(f, w1_ref[...]) + b1_ref[...]))
    h = jnp.sin(q2_ref[...] * (_dot_hi(h, w2_ref[...]) + b2_ref[...]))
    h = _dot_hi(h, w3_ref[...])
    taps = h * jnp.exp(-f[:, 0:1] * jnp.abs(dec_ref[0]))
    row = pl.program_id(0) * tq + lax.broadcasted_iota(jnp.int32, (tq, 1), 0)
    taps = jnp.where(row == 0, taps + bias_ref[...], taps)
    o_ref[...] = jnp.where(row == n_lat, 0.0, taps)


def _filter_taps(feats, w1, b1, q1, w2, b2, q2, w3, decay, bias, n_lat, tq=512):
    n2, fp = feats.shape
    hid = w2.shape[0]
    c = bias.shape[-1]
    nb = n2 // tq
    half = nb // 2
    const = lambda r: (0, 0)
    return pl.pallas_call(
        functools.partial(_filt_kernel, tq=tq, n_lat=n_lat),
        grid=(nb,),
        in_specs=[pl.BlockSpec((tq, fp), lambda r: (r, 0)),
                  pl.BlockSpec((fp, hid), const), pl.BlockSpec((1, hid), const),
                  pl.BlockSpec((1, hid), const),
                  pl.BlockSpec((hid, hid), const), pl.BlockSpec((1, hid), const),
                  pl.BlockSpec((1, hid), const),
                  pl.BlockSpec((hid, c), lambda r: (0, r // half)),
                  pl.BlockSpec((1, 1, c), lambda r: (r // half, 0, 0)),
                  pl.BlockSpec((1, c), const)],
        out_specs=pl.BlockSpec((tq, c), lambda r: (r, 0)),
        out_shape=jax.ShapeDtypeStruct((n2, c), F32),
        compiler_params=_cparams(("parallel",)),
        name="hy_filter",
    )(feats, w1, b1, q1, w2, b2, q2, w3, decay, bias)


def _sconv_kernel(*refs, tm):
    mains, prevs, nexts = refs[0:3], refs[3:6], refs[6:9]
    w_refs, b_refs = refs[9:12], refs[12:15]
    x0_ref, u_ref = refs[15:17]
    i = pl.program_id(1)
    last = pl.num_programs(1) - 1
    row = lax.broadcasted_iota(jnp.int32, (tm, 1), 0)
    z = []
    for part in range(3):
        cur = mains[part][0].astype(F32)
        halo = prevs[part].shape[1]
        prev_row = jnp.where(i == 0, 0.0, prevs[part][0, halo - 1:halo, :].astype(F32))
        next_row = jnp.where(i == last, 0.0, nexts[part][0, 0:1, :].astype(F32))
        up = jnp.where(row == 0, prev_row, pltpu.roll(cur, 1, 0))
        dn = jnp.where(row == tm - 1, next_row, pltpu.roll(cur, tm - 1, 0))
        w = w_refs[part][...]
        z.append(up * w[0:1] + cur * w[1:2] + dn * w[2:3] + b_refs[part][...])
    x0_ref[0] = z[0].astype(x0_ref.dtype)
    u_ref[0] = z[1] * z[2]


def _short_conv(p_x, conv_w, conv_b, col0, c, tm):
    bsz, n, _ = p_x.shape
    halo = 16
    cb0 = col0 // c
    hb = tm // halo
    nhb = n // halo
    in_specs = []
    for part in range(3):
        in_specs.append(pl.BlockSpec((1, tm, c), lambda b, i, p=part: (b, i, cb0 + p)))
    for part in range(3):
        in_specs.append(pl.BlockSpec(
            (1, halo, c), lambda b, i, p=part: (b, jnp.maximum(i * hb - 1, 0), cb0 + p)))
    for part in range(3):
        in_specs.append(pl.BlockSpec(
            (1, halo, c), lambda b, i, p=part: (b, jnp.minimum((i + 1) * hb, nhb - 1), cb0 + p)))
    for part in range(3):
        in_specs.append(pl.BlockSpec((3, c), lambda b, i, p=part: (0, p)))
    for part in range(3):
        in_specs.append(pl.BlockSpec((1, c), lambda b, i, p=part: (0, p)))
    return pl.pallas_call(
        functools.partial(_sconv_kernel, tm=tm),
        grid=(bsz, n // tm),
        in_specs=in_specs,
        out_specs=[pl.BlockSpec((1, tm, c), lambda b, i: (b, i, 0)),
                   pl.BlockSpec((1, tm, c), lambda b, i: (b, i, 0))],
        out_shape=[jax.ShapeDtypeStruct((bsz, n, c), BF16),
                   jax.ShapeDtypeStruct((bsz, n, c), F32)],
        compiler_params=_cparams(("parallel", "parallel")),
        name="short_conv",
    )(*([p_x] * 9 + [conv_w] * 3 + [conv_b] * 3))


def _dft_tables(n1):
    n = n1 * DFT_N2
    k1 = jnp.arange(n1, dtype=jnp.int32)
    n2 = jnp.arange(DFT_N2, dtype=jnp.int32)

    def unit(ph, period):
        ang = (ph % period).astype(F32) * F32(2.0 * math.pi / period)
        return jnp.cos(ang), -jnp.sin(ang)

    ar, ai = unit(k1[:, None] * k1[None, :], n1)
    tr, ti = unit(n2[:, None] * k1[None, :], n)
    gr = ar[None] * tr[:, :, None] - ai[None] * ti[:, :, None]
    gi = ar[None] * ti[:, :, None] + ai[None] * tr[:, :, None]
    h = n1 // 2
    g1c = jnp.concatenate([jnp.concatenate([gr[:, :, :h], -gi[:, :, :h]], axis=2),
                           jnp.concatenate([gi[:, :, :h], gr[:, :, :h]], axis=2)], axis=1)
    g1r = jnp.concatenate([gr, gi], axis=1)
    grt = jnp.swapaxes(gr, 1, 2)[:, :h, :] / n
    git = jnp.swapaxes(gi, 1, 2)[:, :h, :] / n
    g2 = jnp.concatenate([jnp.concatenate([grt, git], axis=2),
                          jnp.concatenate([-git, grt], axis=2)], axis=1)
    ph2 = (n2[:, None] * n2[None, :]) % DFT_N2
    ang2 = ph2.astype(F32) * F32(2.0 * math.pi / DFT_N2)
    fr, fi = jnp.cos(ang2), -jnp.sin(ang2)
    f2 = jnp.concatenate([jnp.concatenate([fr, -fi], axis=1),
                          jnp.concatenate([fi, fr], axis=1)], axis=0)
    f2c = jnp.concatenate([jnp.concatenate([fr, fi], axis=1),
                           jnp.concatenate([-fi, fr], axis=1)], axis=0)
    return (g1c.astype(BF16), g1r.astype(BF16), g2.astype(BF16),
            f2.astype(BF16), f2c.astype(BF16))


def _stage1(gather, g_ref, a_ref, chunk, cn, n1, pitch):
    for q in range(cn):
        n2 = chunk * cn + q
        a = _dot(g_ref[q], gather(n2).astype(BF16))
        a_ref[0, pl.ds(n2, n1, stride=pitch), :] = a[:n1]
        a_ref[1, pl.ds(n2, n1, stride=pitch), :] = a[n1:]


def _a_rows(chunk, ck, q, pitch):
    return pl.ds(pl.multiple_of(chunk * (ck * pitch), SUBLANES) + q * pitch, DFT_N2)


def _load_a_chunk(a_ref, chunk, ck, pitch):
    tiles = []
    for q in range(ck):
        rows = _a_rows(chunk, ck, q, pitch)
        tiles.append(jnp.concatenate([a_ref[0, rows, :], a_ref[1, rows, :]], axis=0).astype(BF16))
    return jnp.concatenate(tiles, axis=1)


def _spec_kernel(t_ref, g_ref, f2_ref, o_ref, a_ref, *, n1, cn, ck, pitch, np1):
    s = pl.program_id(1)

    @pl.when(s < np1)
    def _():
        _stage1(lambda n2: t_ref[pl.ds(n2, n1, stride=DFT_N2), :], g_ref, a_ref, s, cn, n1, pitch)

    @pl.when(s >= np1)
    def _():
        o_ref[0] = _dot(f2_ref[...], _load_a_chunk(a_ref, s - np1, ck, pitch))


def _conv_kernel(u_ref, g1_ref, kf_ref, g2_ref, f2_ref, f2c_ref, y_ref, a_ref,
                 *, n1, cn, ck, pitch, np1, np2):
    s = pl.program_id(1)
    h = n1 // 2

    @pl.when(s < np1)
    def _():
        def gather(n2):
            idx = pl.ds(n2, h, stride=DFT_N2)
            return jnp.concatenate([u_ref[0, idx, :], u_ref[1, idx, :]], axis=0)
        _stage1(gather, g1_ref, a_ref, s, cn, n1, pitch)

    @pl.when(jnp.logical_and(s >= np1, s < np1 + np2))
    def _():
        chunk = s - np1
        x = _dot(f2_ref[...], _load_a_chunk(a_ref, chunk, ck, pitch))
        xr, xi = x[:DFT_N2], x[DFT_N2:]
        kr, ki = kf_ref[0, :DFT_N2, :], kf_ref[0, DFT_N2:, :]
        y = jnp.concatenate([xr * kr - xi * ki, xr * ki + xi * kr], axis=0)
        b = _dot(f2c_ref[...], y.astype(BF16))
        for q in range(ck):
            rows = _a_rows(chunk, ck, q, pitch)
            a_ref[0, rows, :] = b[:DFT_N2, q * LANES:(q + 1) * LANES]
            a_ref[1, rows, :] = b[DFT_N2:, q * LANES:(q + 1) * LANES]

    @pl.when(s >= np1 + np2)
    def _():
        for q in range(cn):
            n2 = (s - np1 - np2) * cn + q
            idx = pl.ds(n2, n1, stride=pitch)
            b = jnp.concatenate([a_ref[0, idx, :], a_ref[1, idx, :]], axis=0)
            y = _dot(g2_ref[q], b.astype(BF16))
            out = pl.ds(n2, h, stride=DFT_N2)
            y_ref[0, out, :] = y[:h]
            y_ref[1, out, :] = y[h:]


def _dft_sizes(n1):
    cn = 16
    ck = min(16, n1)
    return cn, ck, DFT_N2 // cn, n1 // ck, DFT_N2 + SUBLANES


def _filter_spectrum(taps, g1r, f2):
    n, c = taps.shape
    n1 = n // DFT_N2
    cn, ck, np1, np2, pitch = _dft_sizes(n1)
    return pl.pallas_call(
        functools.partial(_spec_kernel, n1=n1, cn=cn, ck=ck, pitch=pitch, np1=np1),
        grid=(c // LANES, np1 + np2),
        in_specs=[pl.BlockSpec((n, LANES), lambda j, s: (0, j), pipeline_mode=pl.Buffered(1)),
                  pl.BlockSpec((cn, 2 * n1, n1), lambda j, s: (jnp.minimum(s, np1 - 1), 0, 0)),
                  pl.BlockSpec((2 * DFT_N2, 2 * DFT_N2), lambda j, s: (0, 0))],
        out_specs=pl.BlockSpec((1, 2 * DFT_N2, ck * LANES),
                               lambda j, s: (j, 0, jnp.maximum(s - np1, 0))),
        out_shape=jax.ShapeDtypeStruct((c // LANES, 2 * DFT_N2, n1 * LANES), F32),
        scratch_shapes=[pltpu.VMEM((2, n1 * pitch, LANES), F32)],
        compiler_params=_cparams(("parallel", "arbitrary")),
        name="hy_spectrum",
    )(taps, g1r, f2)


def _long_conv(u, kf, g1c, g2, f2, f2c):
    bsz, n_lat, c = u.shape
    assert bsz == 2, "the two batch elements are packed as one complex signal"
    n1 = 2 * n_lat // DFT_N2
    cn, ck, np1, np2, pitch = _dft_sizes(n1)
    return pl.pallas_call(
        functools.partial(_conv_kernel, n1=n1, cn=cn, ck=ck, pitch=pitch, np1=np1, np2=np2),
        grid=(c // LANES, np1 + np2 + np1),
        in_specs=[
            pl.BlockSpec((2, n_lat, LANES), lambda j, s: (0, 0, j), pipeline_mode=pl.Buffered(1)),
            pl.BlockSpec((cn, 2 * n1, n1), lambda j, s: (jnp.minimum(s, np1 - 1), 0, 0)),
            pl.BlockSpec((1, 2 * DFT_N2, ck * LANES),
                         lambda j, s: (j, 0, jnp.clip(s - np1, 0, np2 - 1))),
            pl.BlockSpec((cn, n1, 2 * n1),
                         lambda j, s: (jnp.clip(s - np1 - np2, 0, np1 - 1), 0, 0)),
            pl.BlockSpec((2 * DFT_N2, 2 * DFT_N2), lambda j, s: (0, 0)),
            pl.BlockSpec((2 * DFT_N2, 2 * DFT_N2), lambda j, s: (0, 0)),
        ],
        out_specs=pl.BlockSpec((2, n_lat, LANES), lambda j, s: (0, 0, j),
                               pipeline_mode=pl.Buffered(1)),
        out_shape=jax.ShapeDtypeStruct((2, n_lat, c), F32),
        scratch_shapes=[pltpu.VMEM((2, n1 * pitch, LANES), F32)],
        compiler_params=_cparams(("parallel", "arbitrary")),
        name="hy_conv",
    )(u, g1c, kf, g2, f2, f2c)


def _glu_kernel(ya_ref, yb_ref, u_ref, d_ref, wv_ref, wg_ref, o_ref, h_ref):
    @pl.when(pl.program_id(1) == 0)
    def _():
        for b, y_ref in enumerate((ya_ref, yb_ref)):
            y = y_ref[0] + y_ref[1] + u_ref[b].astype(F32) * d_ref[...]
            h_ref[b] = jax.nn.gelu(y, approximate=True).astype(BF16)

    for b in range(2):
        h = h_ref[b]
        o_ref[b] = (_dot(h, wv_ref[...])
                    * jax.nn.sigmoid(_dot(h, wg_ref[...]))).astype(o_ref.dtype)


def _glu(y_ssm, p_x, s5_d, w_glu, s5w, d, tm, tn):
    bsz, n = p_x.shape[:2]
    nj = d // tn
    y_spec = pl.BlockSpec((2, tm, s5w), lambda i, j: (0, i, 0))
    return pl.pallas_call(
        _glu_kernel,
        grid=(n // tm, nj),
        in_specs=[y_spec, y_spec,
                  pl.BlockSpec((bsz, tm, s5w), lambda i, j: (0, i, 0)),
                  pl.BlockSpec((1, s5w), lambda i, j: (0, 0)),
                  pl.BlockSpec((s5w, tn), lambda i, j: (0, j)),
                  pl.BlockSpec((s5w, tn), lambda i, j: (0, nj + j))],
        out_specs=pl.BlockSpec((bsz, tm, tn), lambda i, j: (0, i, j)),
        out_shape=jax.ShapeDtypeStruct((bsz, n, d), BF16),
        scratch_shapes=[pltpu.VMEM((bsz, tm, s5w), BF16)],
        compiler_params=_cparams(("parallel", "arbitrary")),
        name="s5_glu",
    )(y_ssm[0], y_ssm[1], p_x, s5_d, w_glu, w_glu)


def _merge_kernel(x0_ref, yc_ref, w_ref, bs_ref, gs_ref, gh_ref, o_ref, h_ref):
    @pl.when(pl.program_id(2) == 0)
    def _():
        h_ref[...] = (x0_ref[0].astype(F32) * yc_ref[0]).astype(BF16)

    bh = _dot(h_ref[...], w_ref[...])
    merged = (jax.nn.sigmoid(gs_ref[0].astype(F32)) * bs_ref[0].astype(F32)
              + jax.nn.sigmoid(gh_ref[0].astype(F32)) * bh)
    o_ref[0] = merged.astype(o_ref.dtype)


def _merge(x0, y_conv, w_hy_out, branch_s5, p_x, gate_col0, tm, tn):
    bsz, n, c = x0.shape
    d = w_hy_out.shape[1]
    g0 = gate_col0 // tn
    nj = d // tn
    return pl.pallas_call(
        _merge_kernel,
        grid=(bsz, n // tm, nj),
        in_specs=[pl.BlockSpec((1, tm, c), lambda b, i, j: (b, i, 0)),
                  pl.BlockSpec((1, tm, c), lambda b, i, j: (b, i, 0)),
                  pl.BlockSpec((c, tn), lambda b, i, j: (0, j)),
                  pl.BlockSpec((1, tm, tn), lambda b, i, j: (b, i, j)),
                  pl.BlockSpec((1, tm, tn), lambda b, i, j: (b, i, g0 + j)),
                  pl.BlockSpec((1, tm, tn), lambda b, i, j: (b, i, g0 + nj + j))],
        out_specs=pl.BlockSpec((1, tm, tn), lambda b, i, j: (b, i, j)),
        out_shape=jax.ShapeDtypeStruct((bsz, n, d), BF16),
        scratch_shapes=[pltpu.VMEM((tm, c), BF16)],
        compiler_params=_cparams(("parallel", "parallel", "arbitrary")),
        name="merge",
    )(x0, y_conv, w_hy_out, branch_s5, p_x, p_x)


def _out_kernel(m_ref, w_ref, x_ref, rows_ref, cols_ref, gate_ref, o_ref, *, tm):
    xp = x_ref[0] + _pos_block(rows_ref, cols_ref, tm)
    o_ref[0] = xp + gate_ref[0] * _dot(m_ref[0], w_ref[...])


def _out_proj(merged, w_out, x, pos_tabs, gate, tm):
    bsz, n, d = x.shape
    rows_tab, cols_tab = pos_tabs
    dh = d // 2
    return pl.pallas_call(
        functools.partial(_out_kernel, tm=tm),
        grid=(bsz, n // tm),
        in_specs=[pl.BlockSpec((1, tm, d), lambda b, i: (b, i, 0)),
                  pl.BlockSpec((d, d), lambda b, i: (0, 0)),
                  pl.BlockSpec((1, tm, d), lambda b, i: (b, i, 0)),
                  pl.BlockSpec((tm // GRID_W, dh), lambda b, i: (i, 0)),
                  pl.BlockSpec((GRID_W, dh), lambda b, i: (0, 0)),
                  pl.BlockSpec((1, 1, d), lambda b, i: (b, 0, 0))],
        out_specs=pl.BlockSpec((1, tm, d), lambda b, i: (b, i, 0)),
        out_shape=jax.ShapeDtypeStruct((bsz, n, d), F32),
        compiler_params=_cparams(("parallel", "parallel")),
        name="out_proj",
    )(merged, w_out, x, rows_tab, cols_tab, gate)


def _mixer_kernel(ya_ref, yb_ref, u_ref, x0_ref, yc_ref, gs_ref, gh_ref, x_ref, rows_ref,
                  cols_ref, d_ref, gate_ref, wglu_ref, why_ref, wout_ref, o_ref, *, tm, cw):
    first = pl.program_id(0) == 0
    y = (jnp.where(first, ya_ref[0], yb_ref[0]) + jnp.where(first, ya_ref[1], yb_ref[1])
         + u_ref[0].astype(F32) * d_ref[...])
    h = jax.nn.gelu(y, approximate=True).astype(BF16)
    hx = (x0_ref[0].astype(F32) * yc_ref[0]).astype(BF16)
    d = o_ref.shape[-1]
    merged = []
    for c in range(d // cw):
        cols = slice(c * cw, (c + 1) * cw)
        glu_v = _dot(h, wglu_ref[:, cols])
        glu_g = _dot(h, wglu_ref[:, d + c * cw:d + (c + 1) * cw])
        branch_hy = _dot(hx, why_ref[:, cols])
        m = (jax.nn.sigmoid(gs_ref[0, :, cols].astype(F32)) * (glu_v * jax.nn.sigmoid(glu_g))
             + jax.nn.sigmoid(gh_ref[0, :, cols].astype(F32)) * branch_hy)
        merged.append(m.astype(BF16))
    merged = jnp.concatenate(merged, axis=-1)
    nr = tm // GRID_W
    dh = d // 2
    rp = jnp.broadcast_to(rows_ref[0][:, None, :], (nr, GRID_W, dh)).reshape(tm, dh)
    cp = jnp.broadcast_to(cols_ref[...][None], (nr, GRID_W, dh)).reshape(tm, dh)
    for c in range(d // cw):
        cols = slice(c * cw, (c + 1) * cw)
        pos = rp[:, cols] if (c + 1) * cw <= dh else cp[:, c * cw - dh:(c + 1) * cw - dh]
        o_ref[0, :, cols] = (x_ref[0, :, cols] + pos
                             + gate_ref[0, :, cols] * _dot(merged, wout_ref[:, cols]))


def _mixer(y_ssm, p_x, x0, y_conv, x, pos_tabs, s5_d, gate, w_glu, w_hy_out, w_out,
           s5w, gate_col0, tm):
    bsz, n, d = x.shape
    assert bsz == 2, "the S5 readouts arrive as one array per batch element"
    hyw = x0.shape[-1]
    rows_tab, cols_tab = pos_tabs
    nr = tm // GRID_W
    dh = d // 2
    ni = n // tm
    g0 = gate_col0 // d
    cw = min(512, dh)
    resident = dict(pipeline_mode=pl.Buffered(1))
    tok = lambda width, col: pl.BlockSpec((1, tm, width), lambda b, i: (b, i, col))
    return pl.pallas_call(
        functools.partial(_mixer_kernel, tm=tm, cw=cw),
        grid=(bsz, ni),
        in_specs=[
            pl.BlockSpec((2, tm, s5w), lambda b, i: (0, jnp.where(b == 0, i, ni - 1), 0)),
            pl.BlockSpec((2, tm, s5w), lambda b, i: (0, jnp.where(b == 0, 0, i), 0)),
            tok(s5w, 0), tok(hyw, 0), tok(hyw, 0), tok(d, g0), tok(d, g0 + 1), tok(d, 0),
            pl.BlockSpec((1, nr, dh), lambda b, i: (i, 0, 0)),
            pl.BlockSpec((GRID_W, dh), lambda b, i: (0, 0)),
            pl.BlockSpec((1, s5w), lambda b, i: (0, 0)),
            pl.BlockSpec((1, 1, d), lambda b, i: (b, 0, 0)),
            pl.BlockSpec((s5w, 2 * d), lambda b, i: (0, 0), **resident),
            pl.BlockSpec((hyw, d), lambda b, i: (0, 0), **resident),
            pl.BlockSpec((d, d), lambda b, i: (0, 0), **resident),
        ],
        out_specs=pl.BlockSpec((1, tm, d), lambda b, i: (b, i, 0)),
        out_shape=jax.ShapeDtypeStruct((bsz, n, d), F32),
        compiler_params=_cparams(("arbitrary", "arbitrary")),
        name="mixer",
    )(y_ssm[0], y_ssm[1], p_x, x0, y_conv, p_x, p_x, x,
      rows_tab.reshape(-1, nr, dh), cols_tab, s5_d, gate, w_glu, w_hy_out, w_out)


def _ffn_kernel(x_ref, g_ref, sh_ref, sc_ref, gate_ref, wa_ref, wb_ref, wo_ref, nf_ref,
                o_ref, h_ref, acc_ref):
    j = pl.program_id(2)

    @pl.when(j == 0)
    def _():
        h_ref[...] = _rms_mod(x_ref[0], g_ref[...], sh_ref[0], sc_ref[0]).astype(BF16)
        acc_ref[...] = jnp.zeros_like(acc_ref)

    h = h_ref[...]
    act = _silu(_dot(h, wa_ref[...])) * _dot(h, wb_ref[...])
    acc_ref[...] += _dot(act.astype(BF16), wo_ref[...])

    @pl.when(j == pl.num_programs(2) - 1)
    def _():
        xo = x_ref[0] + gate_ref[0] * acc_ref[...]
        o_ref[0] = xo * lax.rsqrt(jnp.mean(xo * xo, axis=-1, keepdims=True) + EPS) * nf_ref[...]


def _ffn(x, g, shift, scale, gate, w_in, w_out, norm_f, tm, tf):
    bsz, n, d = x.shape
    dff = w_out.shape[0]
    nj = dff // tf
    vec = pl.BlockSpec((1, 1, d), lambda b, i, j: (b, 0, 0))
    return pl.pallas_call(
        _ffn_kernel,
        grid=(bsz, n // tm, nj),
        in_specs=[pl.BlockSpec((1, tm, d), lambda b, i, j: (b, i, 0)),
                  pl.BlockSpec((1, d), lambda b, i, j: (0, 0)),
                  vec, vec, vec,
                  pl.BlockSpec((d, tf), lambda b, i, j: (0, j)),
                  pl.BlockSpec((d, tf), lambda b, i, j: (0, nj + j)),
                  pl.BlockSpec((tf, d), lambda b, i, j: (j, 0)),
                  pl.BlockSpec((1, d), lambda b, i, j: (0, 0))],
        out_specs=pl.BlockSpec((1, tm, d), lambda b, i, j: (b, i, 0)),
        out_shape=jax.ShapeDtypeStruct((bsz, n, d), F32),
        scratch_shapes=[pltpu.VMEM((tm, d), BF16), pltpu.VMEM((tm, d), F32)],
        compiler_params=_cparams(("parallel", "parallel", "arbitrary")),
        name="ffn",
    )(x, g, shift, scale, gate, w_in, w_in, w_out, norm_f)


def _pos_tables(n_rows, d):
    quarter = d // 4
    omega = 10000.0 ** (-jnp.arange(quarter, dtype=F32) / quarter)
    ar = jnp.arange(n_rows, dtype=F32)[:, None] * omega
    ac = jnp.arange(GRID_W, dtype=F32)[:, None] * omega
    return (jnp.concatenate([jnp.sin(ar), jnp.cos(ar)], axis=-1),
            jnp.concatenate([jnp.sin(ac), jnp.cos(ac)], axis=-1))


def _filter_features(n_lat, width):
    r = jnp.arange(2 * n_lat, dtype=jnp.int32)
    pos = jnp.where(r <= n_lat, r, 2 * n_lat - r).astype(F32)
    t = pos / float(max(n_lat - 1, 1))
    w = 2.0 * math.pi * pos / n_lat
    bands = jnp.linspace(1e-4, HY_BANDS - 1, HY_BANDS, dtype=F32)
    feats = jnp.concatenate([t[:, None], jnp.cos(w[:, None] * bands), -jnp.sin(w[:, None] * bands)],
                            axis=-1)
    return jnp.pad(feats, ((0, 0), (0, width - feats.shape[1])))


def kernel(x, c, ctx, c_ctx, w_ada, b_ada, norm_mix, w_in, s5_a_re, s5_a_im, s5_log_dt,
           s5_b_re, s5_b_im, s5_c_re, s5_c_im, s5_d, w_glu, hy_conv_w, hy_conv_b,
           hy_f1_w, hy_f1_b, hy_f1_freq, hy_f2_w, hy_f2_b, hy_f2_freq, hy_f3_w, hy_decay,
           hy_bias, w_hy_out, w_out, norm_ffn, w_ffn_in, w_ffn_out, norm_f):
    bsz, n_lat, d = x.shape
    depth = w_ada.shape[0]
    assert depth == 1, "the context stream is only advanced for the single-layer trunk"
    l = 0
    s5w = s5_d.shape[-1]
    hyw = hy_bias.shape[-1]
    n_ctx = ctx.shape[1]
    tm = 512
    tn = min(1024, d)
    tb = 256
    assert n_ctx % tb == 0 and n_lat % tm == 0 and s5w == S5_CHUNKS * LANES

    pos_tabs = _pos_tables(n_lat // GRID_W, d)

    cond_t = jnp.zeros((d, SUBLANES), F32).at[:, :bsz].set(c.T).at[:, bsz].set(c_ctx)
    ada = _ada(cond_t, w_ada[l], b_ada[l][None], bsz + 1, tn=min(1024, d))

    def vec(row0, rows, part):
        v = ada[row0:row0 + rows, part * d:(part + 1) * d]
        return jnp.broadcast_to(v, (bsz, d))[:, None, :]

    shift_mix, scale_mix, gate_mix = vec(0, bsz, 0), vec(0, bsz, 1), vec(0, bsz, 2)
    shift_ffn, scale_ffn, gate_ffn = vec(0, bsz, 3), vec(0, bsz, 4), vec(0, bsz, 5)
    cshift_mix, cscale_mix = vec(bsz, 1, 0), vec(bsz, 1, 1)

    w_in_b = w_in[l].astype(BF16)
    g_mix = norm_mix[l][None]
    u_c = _in_proj(ctx, None, g_mix, cshift_mix, cscale_mix, w_in_b, s5w, tb, s5w)
    p_x = _in_proj(x, pos_tabs, g_mix, shift_mix, scale_mix, w_in_b, w_in_b.shape[1], tm, tn)

    packed = [_s5_params(s5_a_re[l, k], s5_a_im[l, k], s5_log_dt[l, k], s5_b_re[l, k],
                         s5_b_im[l, k], s5_c_re[l, k], s5_c_im[l, k]) for k in range(2)]
    bd, cd, lam_r, lam_i = (jnp.stack(t) for t in zip(*packed))
    y_ssm = _s5(u_c, p_x, bd, cd, lam_r, lam_i, s5w, tb)

    fp = 64
    feats = _filter_features(n_lat, fp)
    w1 = jnp.pad(hy_f1_w[l], ((0, fp - hy_f1_w.shape[1]), (0, 0)))
    taps = _filter_taps(feats, w1, hy_f1_b[l][None], hy_f1_freq[l][None], hy_f2_w[l],
                        hy_f2_b[l][None], hy_f2_freq[l][None], hy_f3_w[l],
                        hy_decay[l][:, None, :], hy_bias[l][None], n_lat)
    g1c, g1r, g2, f2, f2c = _dft_tables(2 * n_lat // DFT_N2)
    kf = _filter_spectrum(taps, g1r, f2)
    x0, u_hy = _short_conv(p_x, hy_conv_w[l], hy_conv_b[l][None], s5w, hyw, tm)
    y_conv = _long_conv(u_hy, kf, g1c, g2, f2, f2c)

    x1 = _mixer(y_ssm, p_x, x0, y_conv, x, pos_tabs, s5_d[l][None], gate_mix,
                w_glu[l].astype(BF16), w_hy_out[l].astype(BF16), w_out[l].astype(BF16),
                s5w, s5w + 3 * hyw, 256)

    return _ffn(x1, norm_ffn[l][None], shift_ffn, scale_ffn, gate_ffn,
                w_ffn_in[l].astype(BF16), w_ffn_out[l].astype(BF16), norm_f[None], tm, 512)
```

```python
import functools
import math

import jax
import jax.numpy as jnp
import numpy as np
from jax import lax
from jax.experimental import pallas as pl
from jax.experimental.pallas import tpu as pltpu

F32 = jnp.float32
BF16 = jnp.bfloat16
HIGHEST = lax.Precision.HIGHEST

GRID_W = 64
N_ADA = 6
EPS = 1e-6
HY_BANDS = 16
LANES = 128
SUBLANES = 8
DFT_N2 = LANES
VMEM_LIMIT = 56 * 1024 * 1024


def _cparams(sem):
    return pltpu.CompilerParams(dimension_semantics=sem, vmem_limit_bytes=VMEM_LIMIT)


def _dot(a, b):
    return jnp.dot(a, b, preferred_element_type=F32)


def _dot_hi(a, b):
    return jnp.dot(a, b, preferred_element_type=F32, precision=HIGHEST)


def _silu(x):
    return x * jax.nn.sigmoid(x)


def _rms_mod(x, g, shift, scale):
    y = x * lax.rsqrt(jnp.mean(x * x, axis=-1, keepdims=True) + EPS)
    return (y * g) * (1.0 + scale) + shift


def _pos_block(rows_ref, cols_ref, tm):
    nr = tm // GRID_W
    dh = rows_ref.shape[-1]
    rp = jnp.broadcast_to(rows_ref[...][:, None, :], (nr, GRID_W, dh)).reshape(tm, dh)
    cp = jnp.broadcast_to(cols_ref[...][None], (nr, GRID_W, dh)).reshape(tm, dh)
    return jnp.concatenate([rp, cp], axis=-1)


def _ada_kernel(ct_ref, w_ref, b_ref, o_ref, *, n_vec):
    sc = _silu(ct_ref[...])
    w = w_ref[...]
    rows = [jnp.sum(w * sc[:, r:r + 1], axis=0, keepdims=True) for r in range(n_vec)]
    rows += [jnp.zeros_like(rows[0])] * (SUBLANES - n_vec)
    o_ref[...] = jnp.concatenate(rows, axis=0) + b_ref[...]


def _ada(cond_t, w, b, n_vec, tn=1024):
    d, n = w.shape
    return pl.pallas_call(
        functools.partial(_ada_kernel, n_vec=n_vec),
        grid=(n // tn,),
        in_specs=[pl.BlockSpec((d, SUBLANES), lambda j: (0, 0)),
                  pl.BlockSpec((d, tn), lambda j: (0, j)),
                  pl.BlockSpec((1, tn), lambda j: (0, j))],
        out_specs=pl.BlockSpec((SUBLANES, tn), lambda j: (0, j)),
        out_shape=jax.ShapeDtypeStruct((SUBLANES, n), F32),
        compiler_params=_cparams(("arbitrary",)),
        name="ada",
    )(cond_t, w, b)


def _in_kernel(*refs, tm, use_pos):
    if use_pos:
        x_ref, rows_ref, cols_ref, g_ref, sh_ref, sc_ref, w_ref, o_ref, h_ref = refs
    else:
        x_ref, g_ref, sh_ref, sc_ref, w_ref, o_ref, h_ref = refs

    @pl.when(pl.program_id(2) == 0)
    def _():
        x = x_ref[0]
        if use_pos:
            x = x + _pos_block(rows_ref, cols_ref, tm)
        h_ref[...] = _rms_mod(x, g_ref[...], sh_ref[0], sc_ref[0]).astype(BF16)

    o_ref[0] = _dot(h_ref[...], w_ref[...]).astype(o_ref.dtype)


def _in_proj(x, pos_tabs, g, shift, scale, w, n_cols, tm, tn):
    bsz, n, d = x.shape
    use_pos = pos_tabs is not None
    in_specs = [pl.BlockSpec((1, tm, d), lambda b, i, j: (b, i, 0))]
    args = [x]
    if use_pos:
        rows_tab, cols_tab = pos_tabs
        dh = d // 2
        in_specs += [pl.BlockSpec((tm // GRID_W, dh), lambda b, i, j: (i, 0)),
                     pl.BlockSpec((GRID_W, dh), lambda b, i, j: (0, 0))]
        args += [rows_tab, cols_tab]
    in_specs += [pl.BlockSpec((1, d), lambda b, i, j: (0, 0)),
                 pl.BlockSpec((1, 1, d), lambda b, i, j: (b, 0, 0)),
                 pl.BlockSpec((1, 1, d), lambda b, i, j: (b, 0, 0)),
                 pl.BlockSpec((d, tn), lambda b, i, j: (0, j))]
    args += [g, shift, scale, w]
    return pl.pallas_call(
        functools.partial(_in_kernel, tm=tm, use_pos=use_pos),
        grid=(bsz, n // tm, n_cols // tn),
        in_specs=in_specs,
        out_specs=pl.BlockSpec((1, tm, tn), lambda b, i, j: (b, i, j)),
        out_shape=jax.ShapeDtypeStruct((bsz, n, n_cols), BF16),
        scratch_shapes=[pltpu.VMEM((tm, d), BF16)],
        compiler_params=_cparams(("parallel", "parallel", "arbitrary")),
        name="in_proj" if use_pos else "ctx_proj",
    )(*args)


PRO_CHUNKS = 8


def _next_block(b, i, ni, nb):
    t = jnp.minimum(b * ni + i + 1, nb * ni - 1)
    return t // ni, t % ni


def _in_pipe_kernel(x0_ref, rows0_ref, sh0_ref, sc0_ref, xn_ref, rowsn_ref, shn_ref, scn_ref,
                    cols_ref, g_ref, w_ref, o_ref, h_ref, *, tm):
    b, i, j = pl.program_id(0), pl.program_id(1), pl.program_id(2)
    par = (b * pl.num_programs(1) + i) % 2
    cr = tm // PRO_CHUNKS
    dh = cols_ref.shape[-1]

    @pl.when(jnp.logical_and(jnp.logical_and(b == 0, i == 0), j == 0))
    def _():
        x = x0_ref[0] + _pos_block(rows0_ref, cols_ref, tm)
        h_ref[0] = _rms_mod(x, g_ref[...], sh0_ref[0], sc0_ref[0]).astype(BF16)

    o_ref[0] = _dot(h_ref[par], w_ref[...]).astype(o_ref.dtype)

    c = jnp.minimum(j, PRO_CHUNKS - 1)
    rows = pl.ds(pl.multiple_of(c * cr, cr), cr)
    pos = jnp.concatenate([jnp.broadcast_to(rowsn_ref[pl.ds(c, 1), :], (cr, dh)), cols_ref[...]],
                          axis=-1)
    xc = xn_ref[0, rows, :] + pos
    h_ref[1 - par, rows, :] = _rms_mod(xc, g_ref[...], shn_ref[0], scn_ref[0]).astype(BF16)


def _in_proj_latent(x, pos_tabs, g, shift, scale, w, tm, tn):
    bsz, n, d = x.shape
    n_cols = w.shape[1]
    rows_tab, cols_tab = pos_tabs
    dh = d // 2
    ni = n // tm
    nr = tm // GRID_W
    assert tm // PRO_CHUNKS == GRID_W and n_cols // tn >= PRO_CHUNKS
    nxt = lambda b, i: _next_block(b, i, ni, bsz)
    once = dict(pipeline_mode=pl.Buffered(1))
    return pl.pallas_call(
        functools.partial(_in_pipe_kernel, tm=tm),
        grid=(bsz, ni, n_cols // tn),
        in_specs=[
            pl.BlockSpec((1, tm, d), lambda b, i, j: (0, 0, 0), **once),
            pl.BlockSpec((nr, dh), lambda b, i, j: (0, 0)),
            pl.BlockSpec((1, 1, d), lambda b, i, j: (0, 0, 0)),
            pl.BlockSpec((1, 1, d), lambda b, i, j: (0, 0, 0)),
            pl.BlockSpec((1, tm, d), lambda b, i, j: nxt(b, i) + (0,)),
            pl.BlockSpec((nr, dh), lambda b, i, j: (nxt(b, i)[1], 0)),
            pl.BlockSpec((1, 1, d), lambda b, i, j: (nxt(b, i)[0], 0, 0)),
            pl.BlockSpec((1, 1, d), lambda b, i, j: (nxt(b, i)[0], 0, 0)),
            pl.BlockSpec((GRID_W, dh), lambda b, i, j: (0, 0)),
            pl.BlockSpec((1, d), lambda b, i, j: (0, 0)),
            pl.BlockSpec((d, tn), lambda b, i, j: (0, j)),
        ],
        out_specs=pl.BlockSpec((1, tm, tn), lambda b, i, j: (b, i, j)),
        out_shape=jax.ShapeDtypeStruct((bsz, n, n_cols), BF16),
        scratch_shapes=[pltpu.VMEM((2, tm, d), BF16)],
        compiler_params=_cparams(("arbitrary", "arbitrary", "arbitrary")),
        name="in_proj",
    )(x, rows_tab, shift, scale, x, rows_tab, shift, scale, cols_tab, g, w)


S5_CHUNKS = 8


def _s5_fill(buf, lhs_of, bd_ref, k, tb, pitch):
    lhs = lhs_of(k)
    lo, hi = _s5_rows(k, tb, pitch)
    if lo != k * pitch:
        z = jnp.zeros((k * pitch - lo, LANES), F32)
        lhs = jnp.concatenate([z, lhs.astype(F32), z], axis=0).astype(BF16)
    bu = _dot(lhs, bd_ref[0, k])
    for m in range(buf.shape[0]):
        buf[m, lo:hi, :] = bu[:, m * LANES:(m + 1) * LANES]


def _s5_rows(k, tb, pitch):
    pad = pitch - tb
    if (k * pitch) % SUBLANES == 0:
        return k * pitch, k * pitch + tb
    return k * pitch - pad, k * pitch + tb + pad


def _s5_pass(d, bu_scan, st_scan, carry_ref, st_mm, bu_mm, lhs_of, y_ref, bd_ref, cd_ref,
             lam_r, lam_i, tb, pitch):
    nslab = bu_scan.shape[0]
    half = nslab // 2
    per = tb // S5_CHUNKS
    sr = [carry_ref[m] for m in range(half)]
    si = [carry_ref[half + m] for m in range(half)]
    for k in range(S5_CHUNKS):
        lo, hi = _s5_rows(k, tb, pitch)
        sk = jnp.concatenate([st_mm[m, lo:hi, :] for m in range(nslab)], axis=-1)
        yk = _dot(sk.astype(BF16), cd_ref[0, k])
        y_ref[0, :, k * LANES:(k + 1) * LANES] = yk[k * pitch - lo:k * pitch - lo + tb]
        _s5_fill(bu_mm, lhs_of, bd_ref, k, tb, pitch)
        for q in range(k * per, (k + 1) * per):
            idx = pl.ds(jnp.where(d == 0, q, tb - 1 - q), SUBLANES, stride=pitch)
            for m in range(half):
                nr = lam_r[m] * sr[m] - lam_i[m] * si[m] + bu_scan[m, idx, :]
                ni = lam_r[m] * si[m] + lam_i[m] * sr[m] + bu_scan[half + m, idx, :]
                st_scan[m, idx, :] = nr
                st_scan[half + m, idx, :] = ni
                sr[m], si[m] = nr, ni
    for m in range(half):
        carry_ref[m] = sr[m]
        carry_ref[half + m] = si[m]


def _s5_kernel(uc0_ref, uca_ref, uxa_ref, ucb_ref, uxb_ref, bd_ref, cd_ref, lr_ref, li_ref,
               ya_ref, yb_ref, a_bu, a_st, b_bu, b_st, ca_ref, cb_ref, *, tb, pitch, nc, nblk):
    d = pl.program_id(0)
    i = pl.program_id(1)
    half = a_bu.shape[0] // 2
    lam_r = [lr_ref[0, m] for m in range(half)]
    lam_i = [li_ref[0, m] for m in range(half)]

    @pl.when(i == 0)
    def _():
        ca_ref[...] = jnp.zeros_like(ca_ref)
        cb_ref[...] = jnp.zeros_like(cb_ref)
        a_st[...] = jnp.zeros_like(a_st)
        b_st[...] = jnp.zeros_like(b_st)
        for k in range(S5_CHUNKS):
            _s5_fill(a_bu, lambda k: uc0_ref[0, :, k * LANES:(k + 1) * LANES], bd_ref, k, tb, pitch)

    def lhs(uc_ref, ux_ref, pos):
        def of(k):
            cols = slice(k * LANES, (k + 1) * LANES)
            return jnp.where(pos < nc, uc_ref[0, :, cols], ux_ref[0, :, cols])
        return of

    _s5_pass(d, a_bu, a_st, ca_ref, b_st, b_bu, lhs(ucb_ref, uxb_ref, i), yb_ref,
             bd_ref, cd_ref, lam_r, lam_i, tb, pitch)

    @pl.when(i < nblk)
    def _():
        _s5_pass(d, b_bu, b_st, cb_ref, a_st, a_bu, lhs(uca_ref, uxa_ref, i + 1), ya_ref,
                 bd_ref, cd_ref, lam_r, lam_i, tb, pitch)


def _s5(u_c, p_x, bd, cd, lam_r, lam_i, s5w, tb):
    bsz, lc, _ = u_c.shape
    assert bsz == 2, "the two batch elements are the two interleaved sequences"
    n = p_x.shape[1]
    nc, nb = lc // tb, n // tb
    nblk = nc + nb
    nslab = bd.shape[-1] // LANES
    pitch = tb + SUBLANES // 2

    def blk(d, pos, count):
        j = jnp.clip(pos, 0, count - 1)
        return jnp.where(d == 0, j, count - 1 - j)

    def u_spec(b, off, ctx):
        if ctx:
            return pl.BlockSpec((1, tb, s5w), lambda d, i: (b, blk(d, i + off, nc), 0))
        return pl.BlockSpec((1, tb, s5w), lambda d, i: (b, blk(d, i + off - nc, nb), 0))

    def y_spec(off):
        return pl.BlockSpec((1, tb, s5w), lambda d, i: (d, blk(d, i + off - nc, nb), 0))

    par = lambda d, i: (d, 0, 0, 0)
    buf = pltpu.VMEM((nslab, S5_CHUNKS * pitch, LANES), F32)
    state = pltpu.VMEM((nslab, SUBLANES, LANES), F32)
    return pl.pallas_call(
        functools.partial(_s5_kernel, tb=tb, pitch=pitch, nc=nc, nblk=nblk),
        grid=(2, nblk + 1),
        in_specs=[
            pl.BlockSpec((1, tb, s5w), lambda d, i: (0, blk(d, 0, nc), 0)),
            u_spec(0, 1, True), u_spec(0, 1, False),
            u_spec(1, 0, True), u_spec(1, 0, False),
            pl.BlockSpec((1, S5_CHUNKS, LANES, nslab * LANES), par, pipeline_mode=pl.Buffered(1)),
            pl.BlockSpec((1, S5_CHUNKS, nslab * LANES, LANES), par, pipeline_mode=pl.Buffered(1)),
            pl.BlockSpec((1, nslab // 2, SUBLANES, LANES), par),
            pl.BlockSpec((1, nslab // 2, SUBLANES, LANES), par),
        ],
        out_specs=[y_spec(0), y_spec(-1)],
        out_shape=[jax.ShapeDtypeStruct((2, n, s5w), F32)] * 2,
        scratch_shapes=[buf, buf, buf, buf, state, state],
        compiler_params=_cparams(("arbitrary", "arbitrary")),
        name="s5",
    )(u_c, u_c, p_x, u_c, p_x, bd, cd, lam_r, lam_i)


def _s5_params(a_re, a_im, log_dt, b_re, b_im, c_re, c_im):
    g, p, h = b_re.shape
    gl = g // S5_CHUNKS
    dt = jnp.exp(log_dt.astype(F32))[:, None]
    mag = jnp.exp(a_re.astype(F32) * dt)
    lr, li = mag * jnp.cos(a_im.astype(F32) * dt), mag * jnp.sin(a_im.astype(F32) * dt)
    den = a_re * a_re + a_im * a_im
    qr = ((lr - 1.0) * a_re + li * a_im) / den
    qi = (li * a_re - (lr - 1.0) * a_im) / den
    bbr = qr[:, :, None] * b_re - qi[:, :, None] * b_im
    bbi = qr[:, :, None] * b_im + qi[:, :, None] * b_re
    eye = jnp.eye(gl, dtype=F32)

    def pack_b(m):
        return jnp.einsum("kgph,gq->kghqp", m.reshape(S5_CHUNKS, gl, p, h), eye).reshape(
            S5_CHUNKS, gl * h, gl * p)

    def pack_c(m):
        return jnp.einsum("kghp,gq->kgpqh", m.reshape(S5_CHUNKS, gl, h, p), eye).reshape(
            S5_CHUNKS, gl * p, gl * h)

    bd = jnp.concatenate([pack_b(bbr), pack_b(bbi)], axis=-1).astype(BF16)
    cd = jnp.concatenate([pack_c(c_re.astype(F32)), -pack_c(c_im.astype(F32))], axis=1).astype(BF16)

    def slabs(v):
        return v.reshape(S5_CHUNKS, -1, LANES).transpose(1, 0, 2)

    return bd, cd, slabs(lr), slabs(li)


def _filt_kernel(f_ref, w1_ref, b1_ref, q1_ref, w2_ref, b2_ref, q2_ref, w3_ref, dec_ref,
                 bias_ref, o_ref, *, tq, n_lat):
    f = f_ref[...]
    h = jnp.sin(q1_ref[...] * (_dot_hi(f, w1_ref[...]) + b1_ref[...]))
    h = jnp.sin(q2_ref[...] * (_dot_hi(h, w2_ref[...]) + b2_ref[...]))
    h = _dot_hi(h, w3_ref[...])
    taps = h * jnp.exp(-f[:, 0:1] * jnp.abs(dec_ref[0]))
    row = pl.program_id(0) * tq + lax.broadcasted_iota(jnp.int32, (tq, 1), 0)
    taps = jnp.where(row == 0, taps + bias_ref[...], taps)
    o_ref[...] = jnp.where(row == n_lat, 0.0, taps)


def _filter_taps(feats, w1, b1, q1, w2, b2, q2, w3, decay, bias, n_lat, tq=512):
    n2, fp = feats.shape
    hid = w2.shape[0]
    c = bias.shape[-1]
    nb = n2 // tq
    half = nb // 2
    const = lambda r: (0, 0)
    return pl.pallas_call(
        functools.partial(_filt_kernel, tq=tq, n_lat=n_lat),
        grid=(nb,),
        in_specs=[pl.BlockSpec((tq, fp), lambda r: (r, 0)),
                  pl.BlockSpec((fp, hid), const), pl.BlockSpec((1, hid), const),
                  pl.BlockSpec((1, hid), const),
                  pl.BlockSpec((hid, hid), const), pl.BlockSpec((1, hid), const),
                  pl.BlockSpec((1, hid), const),
                  pl.BlockSpec((hid, c), lambda r: (0, r // half)),
                  pl.BlockSpec((1, 1, c), lambda r: (r // half, 0, 0)),
                  pl.BlockSpec((1, c), const)],
        out_specs=pl.BlockSpec((tq, c), lambda r: (r, 0)),
        out_shape=jax.ShapeDtypeStruct((n2, c), F32),
        compiler_params=_cparams(("parallel",)),
        name="hy_filter",
    )(feats, w1, b1, q1, w2, b2, q2, w3, decay, bias)


def _sconv_kernel(*refs, tm):
    mains, prevs, nexts = refs[0:3], refs[3:6], refs[6:9]
    w_refs, b_refs = refs[9:12], refs[12:15]
    x0_ref, u_ref = refs[15:17]
    i = pl.program_id(1)
    last = pl.num_programs(1) - 1
    row = lax.broadcasted_iota(jnp.int32, (tm, 1), 0)
    z = []
    for part in range(3):
        cur = mains[part][0].astype(F32)
        halo = prevs[part].shape[1]
        prev_row = jnp.where(i == 0, 0.0, prevs[part][0, halo - 1:halo, :].astype(F32))
        next_row = jnp.where(i == last, 0.0, nexts[part][0, 0:1, :].astype(F32))
        up = jnp.where(row == 0, prev_row, pltpu.roll(cur, 1, 0))
        dn = jnp.where(row == tm - 1, next_row, pltpu.roll(cur, tm - 1, 0))
        w = w_refs[part][...]
        z.append(up * w[0:1] + cur * w[1:2] + dn * w[2:3] + b_refs[part][...])
    x0_ref[0] = z[0].astype(x0_ref.dtype)
    u_ref[0] = z[1] * z[2]


def _short_conv(p_x, conv_w, conv_b, col0, c, tm):
    bsz, n, _ = p_x.shape
    halo = 16
    cb0 = col0 // c
    hb = tm // halo
    nhb = n // halo
    in_specs = []
    for part in range(3):
        in_specs.append(pl.BlockSpec((1, tm, c), lambda b, i, p=part: (b, i, cb0 + p)))
    for part in range(3):
        in_specs.append(pl.BlockSpec(
            (1, halo, c), lambda b, i, p=part: (b, jnp.maximum(i * hb - 1, 0), cb0 + p)))
    for part in range(3):
        in_specs.append(pl.BlockSpec(
            (1, halo, c), lambda b, i, p=part: (b, jnp.minimum((i + 1) * hb, nhb - 1), cb0 + p)))
    for part in range(3):
        in_specs.append(pl.BlockSpec((3, c), lambda b, i, p=part: (0, p)))
    for part in range(3):
        in_specs.append(pl.BlockSpec((1, c), lambda b, i, p=part: (0, p)))
    return pl.pallas_call(
        functools.partial(_sconv_kernel, tm=tm),
        grid=(bsz, n // tm),
        in_specs=in_specs,
        out_specs=[pl.BlockSpec((1, tm, c), lambda b, i: (b, i, 0)),
                   pl.BlockSpec((1, tm, c), lambda b, i: (b, i, 0))],
        out_shape=[jax.ShapeDtypeStruct((bsz, n, c), BF16),
                   jax.ShapeDtypeStruct((bsz, n, c), F32)],
        compiler_params=_cparams(("parallel", "parallel")),
        name="short_conv",
    )(*([p_x] * 9 + [conv_w] * 3 + [conv_b] * 3))


def _dft_tables(n1):
    n = n1 * DFT_N2
    k1 = jnp.arange(n1, dtype=jnp.int32)
    n2 = jnp.arange(DFT_N2, dtype=jnp.int32)

    def unit(ph, period):
        ang = (ph % period).astype(F32) * F32(2.0 * math.pi / period)
        return jnp.cos(ang), -jnp.sin(ang)

    ar, ai = unit(k1[:, None] * k1[None, :], n1)
    tr, ti = unit(n2[:, None] * k1[None, :], n)
    gr = ar[None] * tr[:, :, None] - ai[None] * ti[:, :, None]
    gi = ar[None] * ti[:, :, None] + ai[None] * tr[:, :, None]
    h = n1 // 2
    g1c = jnp.concatenate([jnp.concatenate([gr[:, :, :h], -gi[:, :, :h]], axis=2),
                           jnp.concatenate([gi[:, :, :h], gr[:, :, :h]], axis=2)], axis=1)
    g1r = jnp.concatenate([gr, gi], axis=1)
    grt = jnp.swapaxes(gr, 1, 2)[:, :h, :] / n
    git = jnp.swapaxes(gi, 1, 2)[:, :h, :] / n
    g2 = jnp.concatenate([jnp.concatenate([grt, git], axis=2),
                          jnp.concatenate([-git, grt], axis=2)], axis=1)
    ph2 = (n2[:, None] * n2[None, :]) % DFT_N2
    ang2 = ph2.astype(F32) * F32(2.0 * math.pi / DFT_N2)
    fr, fi = jnp.cos(ang2), -jnp.sin(ang2)
    f2 = jnp.concatenate([jnp.concatenate([fr, -fi], axis=1),
                          jnp.concatenate([fi, fr], axis=1)], axis=0)
    f2c = jnp.concatenate([jnp.concatenate([fr, fi], axis=1),
                           jnp.concatenate([-fi, fr], axis=1)], axis=0)
    return (g1c.astype(BF16), g1r.astype(BF16), g2.astype(BF16),
            f2.astype(BF16), f2c.astype(BF16))


def _stage1(gather, g_ref, a_ref, chunk, cn, n1, pitch):
    for q in range(cn):
        n2 = chunk * cn + q
        a = _dot(g_ref[q], gather(n2).astype(BF16))
        a_ref[0, pl.ds(n2, n1, stride=pitch), :] = a[:n1]
        a_ref[1, pl.ds(n2, n1, stride=pitch), :] = a[n1:]


def _a_rows(chunk, ck, q, pitch):
    return pl.ds(pl.multiple_of(chunk * (ck * pitch), SUBLANES) + q * pitch, DFT_N2)


def _load_a_chunk(a_ref, chunk, ck, pitch):
    tiles = []
    for q in range(ck):
        rows = _a_rows(chunk, ck, q, pitch)
        tiles.append(jnp.concatenate([a_ref[0, rows, :], a_ref[1, rows, :]], axis=0).astype(BF16))
    return jnp.concatenate(tiles, axis=1)


def _spec_kernel(t_ref, g_ref, f2_ref, o_ref, a_ref, *, n1, cn, ck, pitch, np1):
    s = pl.program_id(1)

    @pl.when(s < np1)
    def _():
        _stage1(lambda n2: t_ref[pl.ds(n2, n1, stride=DFT_N2), :], g_ref, a_ref, s, cn, n1, pitch)

    @pl.when(s >= np1)
    def _():
        o_ref[0] = _dot(f2_ref[...], _load_a_chunk(a_ref, s - np1, ck, pitch))


def _conv_kernel(u_ref, g1_ref, kf_ref, g2_ref, f2_ref, f2c_ref, y_ref, a_ref,
                 *, n1, cn, ck, pitch, np1, np2):
    s = pl.program_id(1)
    h = n1 // 2

    @pl.when(s < np1)
    def _():
        def gather(n2):
            idx = pl.ds(n2, h, stride=DFT_N2)
            return jnp.concatenate([u_ref[0, idx, :], u_ref[1, idx, :]], axis=0)
        _stage1(gather, g1_ref, a_ref, s, cn, n1, pitch)

    @pl.when(jnp.logical_and(s >= np1, s < np1 + np2))
    def _():
        chunk = s - np1
        x = _dot(f2_ref[...], _load_a_chunk(a_ref, chunk, ck, pitch))
        xr, xi = x[:DFT_N2], x[DFT_N2:]
        kr, ki = kf_ref[0, :DFT_N2, :], kf_ref[0, DFT_N2:, :]
        y = jnp.concatenate([xr * kr - xi * ki, xr * ki + xi * kr], axis=0)
        b = _dot(f2c_ref[...], y.astype(BF16))
        for q in range(ck):
            rows = _a_rows(chunk, ck, q, pitch)
            a_ref[0, rows, :] = b[:DFT_N2, q * LANES:(q + 1) * LANES]
            a_ref[1, rows, :] = b[DFT_N2:, q * LANES:(q + 1) * LANES]

    @pl.when(s >= np1 + np2)
    def _():
        for q in range(cn):
            n2 = (s - np1 - np2) * cn + q
            idx = pl.ds(n2, n1, stride=pitch)
            b = jnp.concatenate([a_ref[0, idx, :], a_ref[1, idx, :]], axis=0)
            y = _dot(g2_ref[q], b.astype(BF16))
            out = pl.ds(n2, h, stride=DFT_N2)
            y_ref[0, out, :] = y[:h]
            y_ref[1, out, :] = y[h:]


def _dft_sizes(n1):
    cn = 16
    ck = min(16, n1)
    return cn, ck, DFT_N2 // cn, n1 // ck, DFT_N2 + SUBLANES


def _filter_spectrum(taps, g1r, f2):
    n, c = taps.shape
    n1 = n // DFT_N2
    cn, ck, np1, np2, pitch = _dft_sizes(n1)
    return pl.pallas_call(
        functools.partial(_spec_kernel, n1=n1, cn=cn, ck=ck, pitch=pitch, np1=np1),
        grid=(c // LANES, np1 + np2),
        in_specs=[pl.BlockSpec((n, LANES), lambda j, s: (0, j), pipeline_mode=pl.Buffered(1)),
                  pl.BlockSpec((cn, 2 * n1, n1), lambda j, s: (jnp.minimum(s, np1 - 1), 0, 0)),
                  pl.BlockSpec((2 * DFT_N2, 2 * DFT_N2), lambda j, s: (0, 0))],
        out_specs=pl.BlockSpec((1, 2 * DFT_N2, ck * LANES),
                               lambda j, s: (j, 0, jnp.maximum(s - np1, 0))),
        out_shape=jax.ShapeDtypeStruct((c // LANES, 2 * DFT_N2, n1 * LANES), F32),
        scratch_shapes=[pltpu.VMEM((2, n1 * pitch, LANES), F32)],
        compiler_params=_cparams(("parallel", "arbitrary")),
        name="hy_spectrum",
    )(taps, g1r, f2)


def _long_conv(u, kf, g1c, g2, f2, f2c):
    bsz, n_lat, c = u.shape
    assert bsz == 2, "the two batch elements are packed as one complex signal"
    n1 = 2 * n_lat // DFT_N2
    cn, ck, np1, np2, pitch = _dft_sizes(n1)
    return pl.pallas_call(
        functools.partial(_conv_kernel, n1=n1, cn=cn, ck=ck, pitch=pitch, np1=np1, np2=np2),
        grid=(c // LANES, np1 + np2 + np1),
        in_specs=[
            pl.BlockSpec((2, n_lat, LANES), lambda j, s: (0, 0, j), pipeline_mode=pl.Buffered(1)),
            pl.BlockSpec((cn, 2 * n1, n1), lambda j, s: (jnp.minimum(s, np1 - 1), 0, 0)),
            pl.BlockSpec((1, 2 * DFT_N2, ck * LANES),
                         lambda j, s: (j, 0, jnp.clip(s - np1, 0, np2 - 1))),
            pl.BlockSpec((cn, n1, 2 * n1),
                         lambda j, s: (jnp.clip(s - np1 - np2, 0, np1 - 1), 0, 0)),
            pl.BlockSpec((2 * DFT_N2, 2 * DFT_N2), lambda j, s: (0, 0)),
            pl.BlockSpec((2 * DFT_N2, 2 * DFT_N2), lambda j, s: (0, 0)),
        ],
        out_specs=pl.BlockSpec((2, n_lat, LANES), lambda j, s: (0, 0, j),
                               pipeline_mode=pl.Buffered(1)),
        out_shape=jax.ShapeDtypeStruct((2, n_lat, c), F32),
        scratch_shapes=[pltpu.VMEM((2, n1 * pitch, LANES), F32)],
        compiler_params=_cparams(("parallel", "arbitrary")),
        name="hy_conv",
    )(u, g1c, kf, g2, f2, f2c)


def _glu_kernel(ya_ref, yb_ref, u_ref, d_ref, wv_ref, wg_ref, o_ref, h_ref):
    @pl.when(pl.program_id(1) == 0)
    def _():
        for b, y_ref in enumerate((ya_ref, yb_ref)):
            y = y_ref[0] + y_ref[1] + u_ref[b].astype(F32) * d_ref[...]
            h_ref[b] = jax.nn.gelu(y, approximate=True).astype(BF16)

    for b in range(2):
        h = h_ref[b]
        o_ref[b] = (_dot(h, wv_ref[...])
                    * jax.nn.sigmoid(_dot(h, wg_ref[...]))).astype(o_ref.dtype)


def _glu(y_ssm, p_x, s5_d, w_glu, s5w, d, tm, tn):
    bsz, n = p_x.shape[:2]
    nj = d // tn
    y_spec = pl.BlockSpec((2, tm, s5w), lambda i, j: (0, i, 0))
    return pl.pallas_call(
        _glu_kernel,
        grid=(n // tm, nj),
        in_specs=[y_spec, y_spec,
                  pl.BlockSpec((bsz, tm, s5w), lambda i, j: (0, i, 0)),
                  pl.BlockSpec((1, s5w), lambda i, j: (0, 0)),
                  pl.BlockSpec((s5w, tn), lambda i, j: (0, j)),
                  pl.BlockSpec((s5w, tn), lambda i, j: (0, nj + j))],
        out_specs=pl.BlockSpec((bsz, tm, tn), lambda i, j: (0, i, j)),
        out_shape=jax.ShapeDtypeStruct((bsz, n, d), BF16),
        scratch_shapes=[pltpu.VMEM((bsz, tm, s5w), BF16)],
        compiler_params=_cparams(("parallel", "arbitrary")),
        name="s5_glu",
    )(y_ssm[0], y_ssm[1], p_x, s5_d, w_glu, w_glu)


def _merge_kernel(x0_ref, yc_ref, w_ref, bs_ref, gs_ref, gh_ref, o_ref, h_ref):
    @pl.when(pl.program_id(2) == 0)
    def _():
        h_ref[...] = (x0_ref[0].astype(F32) * yc_ref[0]).astype(BF16)

    bh = _dot(h_ref[...], w_ref[...])
    merged = (jax.nn.sigmoid(gs_ref[0].astype(F32)) * bs_ref[0].astype(F32)
              + jax.nn.sigmoid(gh_ref[0].astype(F32)) * bh)
    o_ref[0] = merged.astype(o_ref.dtype)


def _merge(x0, y_conv, w_hy_out, branch_s5, p_x, gate_col0, tm, tn):
    bsz, n, c = x0.shape
    d = w_hy_out.shape[1]
    g0 = gate_col0 // tn
    nj = d // tn
    return pl.pallas_call(
        _merge_kernel,
        grid=(bsz, n // tm, nj),
        in_specs=[pl.BlockSpec((1, tm, c), lambda b, i, j: (b, i, 0)),
                  pl.BlockSpec((1, tm, c), lambda b, i, j: (b, i, 0)),
                  pl.BlockSpec((c, tn), lambda b, i, j: (0, j)),
                  pl.BlockSpec((1, tm, tn), lambda b, i, j: (b, i, j)),
                  pl.BlockSpec((1, tm, tn), lambda b, i, j: (b, i, g0 + j)),
                  pl.BlockSpec((1, tm, tn), lambda b, i, j: (b, i, g0 + nj + j))],
        out_specs=pl.BlockSpec((1, tm, tn), lambda b, i, j: (b, i, j)),
        out_shape=jax.ShapeDtypeStruct((bsz, n, d), BF16),
        scratch_shapes=[pltpu.VMEM((tm, c), BF16)],
        compiler_params=_cparams(("parallel", "parallel", "arbitrary")),
        name="merge",
    )(x0, y_conv, w_hy_out, branch_s5, p_x, p_x)


def _out_kernel(m_ref, w_ref, x_ref, rows_ref, cols_ref, gate_ref, o_ref, *, tm):
    xp = x_ref[0] + _pos_block(rows_ref, cols_ref, tm)
    o_ref[0] = xp + gate_ref[0] * _dot(m_ref[0], w_ref[...])


def _out_proj(merged, w_out, x, pos_tabs, gate, tm):
    bsz, n, d = x.shape
    rows_tab, cols_tab = pos_tabs
    dh = d // 2
    return pl.pallas_call(
        functools.partial(_out_kernel, tm=tm),
        grid=(bsz, n // tm),
        in_specs=[pl.BlockSpec((1, tm, d), lambda b, i: (b, i, 0)),
                  pl.BlockSpec((d, d), lambda b, i: (0, 0)),
                  pl.BlockSpec((1, tm, d), lambda b, i: (b, i, 0)),
                  pl.BlockSpec((tm // GRID_W, dh), lambda b, i: (i, 0)),
                  pl.BlockSpec((GRID_W, dh), lambda b, i: (0, 0)),
                  pl.BlockSpec((1, 1, d), lambda b, i: (b, 0, 0))],
        out_specs=pl.BlockSpec((1, tm, d), lambda b, i: (b, i, 0)),
        out_shape=jax.ShapeDtypeStruct((bsz, n, d), F32),
        compiler_params=_cparams(("parallel", "parallel")),
        name="out_proj",
    )(merged, w_out, x, rows_tab, cols_tab, gate)


def _mixer_kernel(ya_ref, yb_ref, u_ref, x0_ref, yc_ref, gs_ref, gh_ref, x_ref, rows_ref,
                  cols_ref, d_ref, gate_ref, wglu_ref, why_ref, wout_ref, o_ref, *, tm, cw):
    first = pl.program_id(0) == 0
    y = (jnp.where(first, ya_ref[0], yb_ref[0]) + jnp.where(first, ya_ref[1], yb_ref[1])
         + u_ref[0].astype(F32) * d_ref[...])
    h = jax.nn.gelu(y, approximate=True).astype(BF16)
    hx = (x0_ref[0].astype(F32) * yc_ref[0]).astype(BF16)
    d = o_ref.shape[-1]
    merged = []
    for c in range(d // cw):
        cols = slice(c * cw, (c + 1) * cw)
        glu_v = _dot(h, wglu_ref[:, cols])
        glu_g = _dot(h, wglu_ref[:, d + c * cw:d + (c + 1) * cw])
        branch_hy = _dot(hx, why_ref[:, cols])
        m = (jax.nn.sigmoid(gs_ref[0, :, cols].astype(F32)) * (glu_v * jax.nn.sigmoid(glu_g))
             + jax.nn.sigmoid(gh_ref[0, :, cols].astype(F32)) * branch_hy)
        merged.append(m.astype(BF16))
    merged = jnp.concatenate(merged, axis=-1)
    nr = tm // GRID_W
    dh = d // 2
    rp = jnp.broadcast_to(rows_ref[0][:, None, :], (nr, GRID_W, dh)).reshape(tm, dh)
    cp = jnp.broadcast_to(cols_ref[...][None], (nr, GRID_W, dh)).reshape(tm, dh)
    for c in range(d // cw):
        cols = slice(c * cw, (c + 1) * cw)
        pos = rp[:, cols] if (c + 1) * cw <= dh else cp[:, c * cw - dh:(c + 1) * cw - dh]
        o_ref[0, :, cols] = (x_ref[0, :, cols] + pos
                             + gate_ref[0, :, cols] * _dot(merged, wout_ref[:, cols]))


def _mixer(y_ssm, p_x, x0, y_conv, x, pos_tabs, s5_d, gate, w_glu, w_hy_out, w_out,
           s5w, gate_col0, tm):
    bsz, n, d = x.shape
    assert bsz == 2, "the S5 readouts arrive as one array per batch element"
    hyw = x0.shape[-1]
    rows_tab, cols_tab = pos_tabs
    nr = tm // GRID_W
    dh = d // 2
    ni = n // tm
    g0 = gate_col0 // d
    cw = min(512, dh)
    resident = dict(pipeline_mode=pl.Buffered(1))
    tok = lambda width, col: pl.BlockSpec((1, tm, width), lambda b, i: (b, i, col))
    return pl.pallas_call(
        functools.partial(_mixer_kernel, tm=tm, cw=cw),
        grid=(bsz, ni),
        in_specs=[
            pl.BlockSpec((2, tm, s5w), lambda b, i: (0, jnp.where(b == 0, i, ni - 1), 0)),
            pl.BlockSpec((2, tm, s5w), lambda b, i: (0, jnp.where(b == 0, 0, i), 0)),
            tok(s5w, 0), tok(hyw, 0), tok(hyw, 0), tok(d, g0), tok(d, g0 + 1), tok(d, 0),
            pl.BlockSpec((1, nr, dh), lambda b, i: (i, 0, 0)),
            pl.BlockSpec((GRID_W, dh), lambda b, i: (0, 0)),
            pl.BlockSpec((1, s5w), lambda b, i: (0, 0)),
            pl.BlockSpec((1, 1, d), lambda b, i: (b, 0, 0)),
            pl.BlockSpec((s5w, 2 * d), lambda b, i: (0, 0), **resident),
            pl.BlockSpec((hyw, d), lambda b, i: (0, 0), **resident),
            pl.BlockSpec((d, d), lambda b, i: (0, 0), **resident),
        ],
        out_specs=pl.BlockSpec((1, tm, d), lambda b, i: (b, i, 0)),
        out_shape=jax.ShapeDtypeStruct((bsz, n, d), F32),
        compiler_params=_cparams(("arbitrary", "arbitrary")),
        name="mixer",
    )(y_ssm[0], y_ssm[1], p_x, x0, y_conv, p_x, p_x, x,
      rows_tab.reshape(-1, nr, dh), cols_tab, s5_d, gate, w_glu, w_hy_out, w_out)


def _ffn_kernel(x_ref, g_ref, sh_ref, sc_ref, gate_ref, wa_ref, wb_ref, wo_ref, nf_ref,
                o_ref, h_ref, acc_ref):
    j = pl.program_id(2)

    @pl.when(j == 0)
    def _():
        h_ref[...] = _rms_mod(x_ref[0], g_ref[...], sh_ref[0], sc_ref[0]).astype(BF16)
        acc_ref[...] = jnp.zeros_like(acc_ref)

    h = h_ref[...]
    act = _silu(_dot(h, wa_ref[...])) * _dot(h, wb_ref[...])
    acc_ref[...] += _dot(act.astype(BF16), wo_ref[...])

    @pl.when(j == pl.num_programs(2) - 1)
    def _():
        xo = x_ref[0] + gate_ref[0] * acc_ref[...]
        o_ref[0] = xo * lax.rsqrt(jnp.mean(xo * xo, axis=-1, keepdims=True) + EPS) * nf_ref[...]


def _ffn(x, g, shift, scale, gate, w_in, w_out, norm_f, tm, tf):
    bsz, n, d = x.shape
    dff = w_out.shape[0]
    nj = dff // tf
    vec = pl.BlockSpec((1, 1, d), lambda b, i, j: (b, 0, 0))
    return pl.pallas_call(
        _ffn_kernel,
        grid=(bsz, n // tm, nj),
        in_specs=[pl.BlockSpec((1, tm, d), lambda b, i, j: (b, i, 0)),
                  pl.BlockSpec((1, d), lambda b, i, j: (0, 0)),
                  vec, vec, vec,
                  pl.BlockSpec((d, tf), lambda b, i, j: (0, j)),
                  pl.BlockSpec((d, tf), lambda b, i, j: (0, nj + j)),
                  pl.BlockSpec((tf, d), lambda b, i, j: (j, 0)),
                  pl.BlockSpec((1, d), lambda b, i, j: (0, 0))],
        out_specs=pl.BlockSpec((1, tm, d), lambda b, i, j: (b, i, 0)),
        out_shape=jax.ShapeDtypeStruct((bsz, n, d), F32),
        scratch_shapes=[pltpu.VMEM((tm, d), BF16), pltpu.VMEM((tm, d), F32)],
        compiler_params=_cparams(("parallel", "parallel", "arbitrary")),
        name="ffn",
    )(x, g, shift, scale, gate, w_in, w_in, w_out, norm_f)


def _pos_tables(n_rows, d):
    quarter = d // 4
    omega = 10000.0 ** (-jnp.arange(quarter, dtype=F32) / quarter)
    ar = jnp.arange(n_rows, dtype=F32)[:, None] * omega
    ac = jnp.arange(GRID_W, dtype=F32)[:, None] * omega
    return (jnp.concatenate([jnp.sin(ar), jnp.cos(ar)], axis=-1),
            jnp.concatenate([jnp.sin(ac), jnp.cos(ac)], axis=-1))


def _filter_features(n_lat, width):
    r = jnp.arange(2 * n_lat, dtype=jnp.int32)
    pos = jnp.where(r <= n_lat, r, 2 * n_lat - r).astype(F32)
    t = pos / float(max(n_lat - 1, 1))
    w = 2.0 * math.pi * pos / n_lat
    bands = jnp.linspace(1e-4, HY_BANDS - 1, HY_BANDS, dtype=F32)
    feats = jnp.concatenate([t[:, None], jnp.cos(w[:, None] * bands), -jnp.sin(w[:, None] * bands)],
                            axis=-1)
    return jnp.pad(feats, ((0, 0), (0, width - feats.shape[1])))


def kernel(x, c, ctx, c_ctx, w_ada, b_ada, norm_mix, w_in, s5_a_re, s5_a_im, s5_log_dt,
           s5_b_re, s5_b_im, s5_c_re, s5_c_im, s5_d, w_glu, hy_conv_w, hy_conv_b,
           hy_f1_w, hy_f1_b, hy_f1_freq, hy_f2_w, hy_f2_b, hy_f2_freq, hy_f3_w, hy_decay,
           hy_bias, w_hy_out, w_out, norm_ffn, w_ffn_in, w_ffn_out, norm_f):
    bsz, n_lat, d = x.shape
    depth = w_ada.shape[0]
    assert depth == 1, "the context stream is only advanced for the single-layer trunk"
    l = 0
    s5w = s5_d.shape[-1]
    hyw = hy_bias.shape[-1]
    n_ctx = ctx.shape[1]
    tm = 512
    tn = min(1024, d)
    tb = 256
    assert n_ctx % tb == 0 and n_lat % tm == 0 and s5w == S5_CHUNKS * LANES

    pos_tabs = _pos_tables(n_lat // GRID_W, d)

    cond_t = jnp.zeros((d, SUBLANES), F32).at[:, :bsz].set(c.T).at[:, bsz].set(c_ctx)
    ada = _ada(cond_t, w_ada[l], b_ada[l][None], bsz + 1, tn=min(1024, d))

    def vec(row0, rows, part):
        v = ada[row0:row0 + rows, part * d:(part + 1) * d]
        return jnp.broadcast_to(v, (bsz, d))[:, None, :]

    shift_mix, scale_mix, gate_mix = vec(0, bsz, 0), vec(0, bsz, 1), vec(0, bsz, 2)
    shift_ffn, scale_ffn, gate_ffn = vec(0, bsz, 3), vec(0, bsz, 4), vec(0, bsz, 5)
    cshift_mix, cscale_mix = vec(bsz, 1, 0), vec(bsz, 1, 1)

    w_in_b = w_in[l].astype(BF16)
    g_mix = norm_mix[l][None]
    u_c = _in_proj(ctx, None, g_mix, cshift_mix, cscale_mix, w_in_b, s5w, tb, s5w)
    p_x = _in_proj_latent(x, pos_tabs, g_mix, shift_mix, scale_mix, w_in_b, tm, tn)

    packed = [_s5_params(s5_a_re[l, k], s5_a_im[l, k], s5_log_dt[l, k], s5_b_re[l, k],
                         s5_b_im[l, k], s5_c_re[l, k], s5_c_im[l, k]) for k in range(2)]
    bd, cd, lam_r, lam_i = (jnp.stack(t) for t in zip(*packed))
    y_ssm = _s5(u_c, p_x, bd, cd, lam_r, lam_i, s5w, tb)

    fp = 64
    feats = _filter_features(n_lat, fp)
    w1 = jnp.pad(hy_f1_w[l], ((0, fp - hy_f1_w.shape[1]), (0, 0)))
    taps = _filter_taps(feats, w1, hy_f1_b[l][None], hy_f1_freq[l][None], hy_f2_w[l],
                        hy_f2_b[l][None], hy_f2_freq[l][None], hy_f3_w[l],
                        hy_decay[l][:, None, :], hy_bias[l][None], n_lat)
    g1c, g1r, g2, f2, f2c = _dft_tables(2 * n_lat // DFT_N2)
    kf = _filter_spectrum(taps, g1r, f2)
    x0, u_hy = _short_conv(p_x, hy_conv_w[l], hy_conv_b[l][None], s5w, hyw, tm)
    y_conv = _long_conv(u_hy, kf, g1c, g2, f2, f2c)

    x1 = _mixer(y_ssm, p_x, x0, y_conv, x, pos_tabs, s5_d[l][None], gate_mix,
                w_glu[l].astype(BF16), w_hy_out[l].astype(BF16), w_out[l].astype(BF16),
                s5w, s5w + 3 * hyw, 256)

    return _ffn(x1, norm_ffn[l][None], shift_ffn, scale_ffn, gate_ffn,
                w_ffn_in[l].astype(BF16), w_ffn_out[l].astype(BF16), norm_f[None], tm, 512)
```

```python
import functools
import math

import jax
import jax.numpy as jnp
from jax import lax
from jax.experimental import pallas as pl
from jax.experimental.pallas import tpu as pltpu

F32 = jnp.float32
BF16 = jnp.bfloat16
HIGHEST = lax.Precision.HIGHEST

GRID_W = 64
N_ADA = 6
EPS = 1e-6
HY_BANDS = 16
LANES = 128
SUBLANES = 8
DFT_N2 = LANES
GROUP_PITCH = DFT_N2 + SUBLANES // 2
VMEM_LIMIT = 56 * 1024 * 1024


def _cparams(sem):
    return pltpu.CompilerParams(dimension_semantics=sem, vmem_limit_bytes=VMEM_LIMIT)


def _dot(a, b):
    return jnp.dot(a, b, preferred_element_type=F32)


def _dot_hi(a, b):
    return jnp.dot(a, b, preferred_element_type=F32, precision=HIGHEST)


def _silu(x):
    return x * jax.nn.sigmoid(x)


def _rms_mod(x, g, shift, scale):
    y = x * lax.rsqrt(jnp.mean(x * x, axis=-1, keepdims=True) + EPS)
    return (y * g) * (1.0 + scale) + shift


def _pos_block(rows, cols, tm):
    nr = tm // GRID_W
    dh = rows.shape[-1]
    rp = jnp.broadcast_to(rows[:, None, :], (nr, GRID_W, dh)).reshape(tm, dh)
    cp = jnp.broadcast_to(cols[None], (nr, GRID_W, dh)).reshape(tm, dh)
    return rp, cp


def _store_padded(ref, lead, val):
    pad = jnp.zeros((GROUP_PITCH - DFT_N2, val.shape[-1]), val.dtype)
    for g in range(val.shape[0] // DFT_N2):
        ref[lead, g * GROUP_PITCH:g * GROUP_PITCH + DFT_N2, :] = val[g * DFT_N2:(g + 1) * DFT_N2]
        ref[lead, g * GROUP_PITCH + DFT_N2:(g + 1) * GROUP_PITCH, :] = pad


def _ada_kernel(ct_ref, w_ref, b_ref, o_ref, *, n_vec):
    sc = _silu(ct_ref[...])
    w = w_ref[...]
    rows = [jnp.sum(w * sc[:, r:r + 1], axis=0, keepdims=True) for r in range(n_vec)]
    rows += [jnp.zeros_like(rows[0])] * (SUBLANES - n_vec)
    o_ref[...] = jnp.concatenate(rows, axis=0) + b_ref[...]


def _ada(cond_t, w, b, n_vec, tn=1024):
    d, n = w.shape
    return pl.pallas_call(
        functools.partial(_ada_kernel, n_vec=n_vec),
        grid=(n // tn,),
        in_specs=[pl.BlockSpec((d, SUBLANES), lambda j: (0, 0)),
                  pl.BlockSpec((d, tn), lambda j: (0, j)),
                  pl.BlockSpec((1, tn), lambda j: (0, j))],
        out_specs=pl.BlockSpec((SUBLANES, tn), lambda j: (0, j)),
        out_shape=jax.ShapeDtypeStruct((SUBLANES, n), F32),
        compiler_params=_cparams(("arbitrary",)),
        name="ada",
    )(cond_t, w, b)


def _ctx_kernel(x_ref, g_ref, sh_ref, sc_ref, w_ref, o_ref):
    h = _rms_mod(x_ref[0], g_ref[...], sh_ref[0], sc_ref[0]).astype(BF16)
    o_ref[0] = _dot(h, w_ref[...]).astype(o_ref.dtype)


def _ctx_proj(x, g, shift, scale, w, n_cols, tm):
    bsz, n, d = x.shape
    vec = pl.BlockSpec((1, 1, d), lambda b, i: (b, 0, 0))
    return pl.pallas_call(
        _ctx_kernel,
        grid=(bsz, n // tm),
        in_specs=[pl.BlockSpec((1, tm, d), lambda b, i: (b, i, 0)),
                  pl.BlockSpec((1, d), lambda b, i: (0, 0)), vec, vec,
                  pl.BlockSpec((d, n_cols), lambda b, i: (0, 0))],
        out_specs=pl.BlockSpec((1, tm, n_cols), lambda b, i: (b, i, 0)),
        out_shape=jax.ShapeDtypeStruct((bsz, n, n_cols), BF16),
        compiler_params=_cparams(("parallel", "parallel")),
        name="ctx_proj",
    )(x, g, shift, scale, w)


PRO_CHUNKS = 8


def _next_block(b, i, ni, nb):
    t = jnp.minimum(b * ni + i + 1, nb * ni - 1)
    return t // ni, t % ni


def _in_pipe_kernel(x0_ref, rows0_ref, sh0_ref, sc0_ref, xn_ref, rowsn_ref, shn_ref, scn_ref,
                    cols_ref, g_ref, w_ref, o_ref, h_ref, *, tm):
    b, i, j = pl.program_id(0), pl.program_id(1), pl.program_id(2)
    par = (b * pl.num_programs(1) + i) % 2
    cr = tm // PRO_CHUNKS
    dh = cols_ref.shape[-1]

    @pl.when(jnp.logical_and(jnp.logical_and(b == 0, i == 0), j == 0))
    def _():
        x = x0_ref[0] + jnp.concatenate(_pos_block(rows0_ref[...], cols_ref[...], tm), axis=-1)
        h_ref[0] = _rms_mod(x, g_ref[...], sh0_ref[0], sc0_ref[0]).astype(BF16)

    o_ref[0] = _dot(h_ref[par], w_ref[...]).astype(o_ref.dtype)

    c = jnp.minimum(j, PRO_CHUNKS - 1)
    rows = pl.ds(pl.multiple_of(c * cr, cr), cr)
    pos = jnp.concatenate([jnp.broadcast_to(rowsn_ref[pl.ds(c, 1), :], (cr, dh)), cols_ref[...]],
                          axis=-1)
    xc = xn_ref[0, rows, :] + pos
    h_ref[1 - par, rows, :] = _rms_mod(xc, g_ref[...], shn_ref[0], scn_ref[0]).astype(BF16)


def _in_proj_latent(x, pos_tabs, g, shift, scale, w, tm, tn):
    bsz, n, d = x.shape
    n_cols = w.shape[1]
    rows_tab, cols_tab = pos_tabs
    dh = d // 2
    ni = n // tm
    nr = tm // GRID_W
    assert tm // PRO_CHUNKS == GRID_W and n_cols // tn >= PRO_CHUNKS
    nxt = lambda b, i: _next_block(b, i, ni, bsz)
    once = dict(pipeline_mode=pl.Buffered(1))
    return pl.pallas_call(
        functools.partial(_in_pipe_kernel, tm=tm),
        grid=(bsz, ni, n_cols // tn),
        in_specs=[
            pl.BlockSpec((1, tm, d), lambda b, i, j: (0, 0, 0), **once),
            pl.BlockSpec((nr, dh), lambda b, i, j: (0, 0)),
            pl.BlockSpec((1, 1, d), lambda b, i, j: (0, 0, 0)),
            pl.BlockSpec((1, 1, d), lambda b, i, j: (0, 0, 0)),
            pl.BlockSpec((1, tm, d), lambda b, i, j: nxt(b, i) + (0,)),
            pl.BlockSpec((nr, dh), lambda b, i, j: (nxt(b, i)[1], 0)),
            pl.BlockSpec((1, 1, d), lambda b, i, j: (nxt(b, i)[0], 0, 0)),
            pl.BlockSpec((1, 1, d), lambda b, i, j: (nxt(b, i)[0], 0, 0)),
            pl.BlockSpec((GRID_W, dh), lambda b, i, j: (0, 0)),
            pl.BlockSpec((1, d), lambda b, i, j: (0, 0)),
            pl.BlockSpec((d, tn), lambda b, i, j: (0, j)),
        ],
        out_specs=pl.BlockSpec((1, tm, tn), lambda b, i, j: (b, i, j)),
        out_shape=jax.ShapeDtypeStruct((bsz, n, n_cols), BF16),
        scratch_shapes=[pltpu.VMEM((2, tm, d), BF16)],
        compiler_params=_cparams(("arbitrary", "arbitrary", "arbitrary")),
        name="in_proj",
    )(x, rows_tab, shift, scale, x, rows_tab, shift, scale, cols_tab, g, w)


S5_CHUNKS = 8


def _s5_rows(k, tb, pitch):
    pad = pitch - tb
    if (k * pitch) % SUBLANES == 0:
        return k * pitch, k * pitch + tb
    return k * pitch - pad, k * pitch + tb + pad


def _s5_fill(buf, lhs_of, bd_ref, k, tb, pitch):
    lhs = lhs_of(k)
    lo, hi = _s5_rows(k, tb, pitch)
    if lo != k * pitch:
        z = jnp.zeros((k * pitch - lo, LANES), F32)
        lhs = jnp.concatenate([z, lhs.astype(F32), z], axis=0).astype(BF16)
    bu = _dot(lhs, bd_ref[0, k])
    for m in range(buf.shape[0]):
        buf[m, lo:hi, :] = bu[:, m * LANES:(m + 1) * LANES]


def _s5_pass(d, bu_scan, st_scan, carry_ref, st_mm, bu_mm, lhs_of, y_ref, bd_ref, cd_ref,
             lam_r, lam_i, tb, pitch):
    nslab = bu_scan.shape[0]
    half = nslab // 2
    per = tb // S5_CHUNKS
    sr = [carry_ref[m] for m in range(half)]
    si = [carry_ref[half + m] for m in range(half)]

    def scan_steps(q0, q1):
        for q in range(q0, q1):
            idx = pl.ds(jnp.where(d == 0, q, tb - 1 - q), SUBLANES, stride=pitch)
            for m in range(half):
                nr = lam_r[m] * sr[m] - lam_i[m] * si[m] + bu_scan[m, idx, :]
                ni = lam_r[m] * si[m] + lam_i[m] * sr[m] + bu_scan[half + m, idx, :]
                st_scan[m, idx, :] = nr
                st_scan[half + m, idx, :] = ni
                sr[m], si[m] = nr, ni

    for k in range(S5_CHUNKS):
        lo, hi = _s5_rows(k, tb, pitch)
        sk = jnp.concatenate([st_mm[m, lo:hi, :] for m in range(nslab)], axis=-1)
        yk = _dot(sk.astype(BF16), cd_ref[0, k])
        y_ref[0, :, k * LANES:(k + 1) * LANES] = yk[k * pitch - lo:k * pitch - lo + tb]
        scan_steps(k * per, k * per + per // 2)
        _s5_fill(bu_mm, lhs_of, bd_ref, k, tb, pitch)
        scan_steps(k * per + per // 2, (k + 1) * per)
    for m in range(half):
        carry_ref[m] = sr[m]
        carry_ref[half + m] = si[m]


def _s5_kernel(uc0_ref, uca_ref, uxa_ref, ucb_ref, uxb_ref, bd_ref, cd_ref, lr_ref, li_ref,
               ya_ref, yb_ref, a_bu, a_st, b_bu, b_st, ca_ref, cb_ref, *, tb, pitch, nc, nblk):
    d = pl.program_id(0)
    i = pl.program_id(1)
    half = a_bu.shape[0] // 2
    lam_r = [lr_ref[0, m] for m in range(half)]
    lam_i = [li_ref[0, m] for m in range(half)]

    @pl.when(i == 0)
    def _():
        ca_ref[...] = jnp.zeros_like(ca_ref)
        cb_ref[...] = jnp.zeros_like(cb_ref)
        a_st[...] = jnp.zeros_like(a_st)
        b_st[...] = jnp.zeros_like(b_st)
        for k in range(S5_CHUNKS):
            _s5_fill(a_bu, lambda k: uc0_ref[0, :, k * LANES:(k + 1) * LANES], bd_ref, k, tb, pitch)

    def lhs(uc_ref, ux_ref, pos):
        def of(k):
            cols = slice(k * LANES, (k + 1) * LANES)
            return jnp.where(pos < nc, uc_ref[0, :, cols], ux_ref[0, :, cols])
        return of

    _s5_pass(d, a_bu, a_st, ca_ref, b_st, b_bu, lhs(ucb_ref, uxb_ref, i), yb_ref,
             bd_ref, cd_ref, lam_r, lam_i, tb, pitch)

    @pl.when(i < nblk)
    def _():
        _s5_pass(d, b_bu, b_st, cb_ref, a_st, a_bu, lhs(uca_ref, uxa_ref, i + 1), ya_ref,
                 bd_ref, cd_ref, lam_r, lam_i, tb, pitch)


def _s5(u_c, p_x, bd, cd, lam_r, lam_i, s5w, tb):
    bsz, lc, _ = u_c.shape
    assert bsz == 2, "the two batch elements are the two interleaved sequences"
    n = p_x.shape[1]
    nc, nb = lc // tb, n // tb
    nblk = nc + nb
    nslab = bd.shape[-1] // LANES
    pitch = tb + SUBLANES // 2

    def blk(d, pos, count):
        j = jnp.clip(pos, 0, count - 1)
        return jnp.where(d == 0, j, count - 1 - j)

    def u_spec(b, off, ctx):
        if ctx:
            return pl.BlockSpec((1, tb, s5w), lambda d, i: (b, blk(d, i + off, nc), 0))
        return pl.BlockSpec((1, tb, s5w), lambda d, i: (b, blk(d, i + off - nc, nb), 0))

    def y_spec(off):
        return pl.BlockSpec((1, tb, s5w), lambda d, i: (d, blk(d, i + off - nc, nb), 0))

    par = lambda d, i: (d, 0, 0, 0)
    buf = pltpu.VMEM((nslab, S5_CHUNKS * pitch, LANES), F32)
    state = pltpu.VMEM((nslab, SUBLANES, LANES), F32)
    return pl.pallas_call(
        functools.partial(_s5_kernel, tb=tb, pitch=pitch, nc=nc, nblk=nblk),
        grid=(2, nblk + 1),
        in_specs=[
            pl.BlockSpec((1, tb, s5w), lambda d, i: (0, blk(d, 0, nc), 0)),
            u_spec(0, 1, True), u_spec(0, 1, False),
            u_spec(1, 0, True), u_spec(1, 0, False),
            pl.BlockSpec((1, S5_CHUNKS, LANES, nslab * LANES), par, pipeline_mode=pl.Buffered(1)),
            pl.BlockSpec((1, S5_CHUNKS, nslab * LANES, LANES), par, pipeline_mode=pl.Buffered(1)),
            pl.BlockSpec((1, nslab // 2, SUBLANES, LANES), par),
            pl.BlockSpec((1, nslab // 2, SUBLANES, LANES), par),
        ],
        out_specs=[y_spec(0), y_spec(-1)],
        out_shape=[jax.ShapeDtypeStruct((2, n, s5w), F32)] * 2,
        scratch_shapes=[buf, buf, buf, buf, state, state],
        compiler_params=_cparams(("arbitrary", "arbitrary")),
        name="s5",
    )(u_c, u_c, p_x, u_c, p_x, bd, cd, lam_r, lam_i)


def _s5_params(a_re, a_im, log_dt, b_re, b_im, c_re, c_im):
    g, p, h = b_re.shape
    gl = g // S5_CHUNKS
    dt = jnp.exp(log_dt.astype(F32))[:, None]
    mag = jnp.exp(a_re.astype(F32) * dt)
    lr, li = mag * jnp.cos(a_im.astype(F32) * dt), mag * jnp.sin(a_im.astype(F32) * dt)
    den = a_re * a_re + a_im * a_im
    qr = ((lr - 1.0) * a_re + li * a_im) / den
    qi = (li * a_re - (lr - 1.0) * a_im) / den
    bbr = qr[:, :, None] * b_re - qi[:, :, None] * b_im
    bbi = qr[:, :, None] * b_im + qi[:, :, None] * b_re
    eye = jnp.eye(gl, dtype=F32)

    def pack_b(m):
        return jnp.einsum("kgph,gq->kghqp", m.reshape(S5_CHUNKS, gl, p, h), eye).reshape(
            S5_CHUNKS, gl * h, gl * p)

    def pack_c(m):
        return jnp.einsum("kghp,gq->kgpqh", m.reshape(S5_CHUNKS, gl, h, p), eye).reshape(
            S5_CHUNKS, gl * p, gl * h)

    bd = jnp.concatenate([pack_b(bbr), pack_b(bbi)], axis=-1).astype(BF16)
    cd = jnp.concatenate([pack_c(c_re.astype(F32)), -pack_c(c_im.astype(F32))], axis=1).astype(BF16)

    def slabs(v):
        return v.reshape(S5_CHUNKS, -1, LANES).transpose(1, 0, 2)

    return bd, cd, slabs(lr), slabs(li)


def _filt_kernel(f_ref, w1_ref, b1_ref, q1_ref, w2_ref, b2_ref, q2_ref, w3_ref, dec_ref,
                 bias_ref, o_ref, *, tq):
    f = f_ref[...]
    h = jnp.sin(q1_ref[...] * (_dot_hi(f, w1_ref[...]) + b1_ref[...]))
    h = jnp.sin(q2_ref[...] * (_dot_hi(h, w2_ref[...]) + b2_ref[...]))
    h = _dot_hi(h, w3_ref[...])
    c = bias_ref.shape[-1]
    t = f[:, 0:1]
    lag = pl.program_id(0) * tq + lax.broadcasted_iota(jnp.int32, (tq, 1), 0)
    fwd = h[:, :c] * jnp.exp(-t * jnp.abs(dec_ref[0:1, :]))
    bwd = h[:, c:] * jnp.exp(-t * jnp.abs(dec_ref[1:2, :]))
    fwd = jnp.where(lag == 0, fwd + bias_ref[...], fwd)
    bwd = jnp.where(lag == 0, 0.0, bwd)
    _store_padded(o_ref, 0, fwd)
    _store_padded(o_ref, 1, bwd)


def _filter_taps(feats, w1, b1, q1, w2, b2, q2, w3, decay, bias, tq=512):
    n_lat, fp = feats.shape
    hid = w2.shape[0]
    c = bias.shape[-1]
    rows = tq // DFT_N2 * GROUP_PITCH
    const = lambda r: (0, 0)
    return pl.pallas_call(
        functools.partial(_filt_kernel, tq=tq),
        grid=(n_lat // tq,),
        in_specs=[pl.BlockSpec((tq, fp), lambda r: (r, 0)),
                  pl.BlockSpec((fp, hid), const), pl.BlockSpec((1, hid), const),
                  pl.BlockSpec((1, hid), const),
                  pl.BlockSpec((hid, hid), const), pl.BlockSpec((1, hid), const),
                  pl.BlockSpec((1, hid), const),
                  pl.BlockSpec((hid, 2 * c), const),
                  pl.BlockSpec((2, c), const),
                  pl.BlockSpec((1, c), const)],
        out_specs=pl.BlockSpec((2, rows, c), lambda r: (0, r, 0)),
        out_shape=jax.ShapeDtypeStruct((2, n_lat // DFT_N2 * GROUP_PITCH, c), F32),
        compiler_params=_cparams(("parallel",)),
        name="hy_filter",
    )(feats, w1, b1, q1, w2, b2, q2, w3, decay, bias)


def _sconv_kernel(*refs, tm):
    mains, prevs, nexts = refs[0:3], refs[3:6], refs[6:9]
    w_refs, b_refs = refs[9:12], refs[12:15]
    x0_ref, u_ref = refs[15:17]
    i = pl.program_id(1)
    last = pl.num_programs(1) - 1
    row = lax.broadcasted_iota(jnp.int32, (tm, 1), 0)
    z = []
    for part in range(3):
        cur = mains[part][0].astype(F32)
        halo = prevs[part].shape[1]
        prev_row = jnp.where(i == 0, 0.0, prevs[part][0, halo - 1:halo, :].astype(F32))
        next_row = jnp.where(i == last, 0.0, nexts[part][0, 0:1, :].astype(F32))
        up = jnp.where(row == 0, prev_row, pltpu.roll(cur, 1, 0))
        dn = jnp.where(row == tm - 1, next_row, pltpu.roll(cur, tm - 1, 0))
        w = w_refs[part][...]
        z.append(up * w[0:1] + cur * w[1:2] + dn * w[2:3] + b_refs[part][...])
    x0_ref[0] = z[0].astype(x0_ref.dtype)
    _store_padded(u_ref, 0, z[1] * z[2])


def _short_conv(p_x, conv_w, conv_b, col0, c, tm):
    bsz, n, _ = p_x.shape
    halo = 16
    cb0 = col0 // c
    hb = tm // halo
    nhb = n // halo
    in_specs = []
    for part in range(3):
        in_specs.append(pl.BlockSpec((1, tm, c), lambda b, i, p=part: (b, i, cb0 + p)))
    for part in range(3):
        in_specs.append(pl.BlockSpec(
            (1, halo, c), lambda b, i, p=part: (b, jnp.maximum(i * hb - 1, 0), cb0 + p)))
    for part in range(3):
        in_specs.append(pl.BlockSpec(
            (1, halo, c), lambda b, i, p=part: (b, jnp.minimum((i + 1) * hb, nhb - 1), cb0 + p)))
    for part in range(3):
        in_specs.append(pl.BlockSpec((3, c), lambda b, i, p=part: (0, p)))
    for part in range(3):
        in_specs.append(pl.BlockSpec((1, c), lambda b, i, p=part: (0, p)))
    return pl.pallas_call(
        functools.partial(_sconv_kernel, tm=tm),
        grid=(bsz, n // tm),
        in_specs=in_specs,
        out_specs=[pl.BlockSpec((1, tm, c), lambda b, i: (b, i, 0)),
                   pl.BlockSpec((1, tm // DFT_N2 * GROUP_PITCH, c), lambda b, i: (b, i, 0))],
        out_shape=[jax.ShapeDtypeStruct((bsz, n, c), BF16),
                   jax.ShapeDtypeStruct((bsz, n // DFT_N2 * GROUP_PITCH, c), F32)],
        compiler_params=_cparams(("parallel", "parallel")),
        name="short_conv",
    )(*([p_x] * 9 + [conv_w] * 3 + [conv_b] * 3))


def _dft_tables(n1):
    n = n1 * DFT_N2
    k1 = jnp.arange(n1, dtype=jnp.int32)
    n2 = jnp.arange(DFT_N2, dtype=jnp.int32)

    def unit(ph, period):
        ang = (ph % period).astype(F32) * F32(2.0 * math.pi / period)
        return jnp.cos(ang), -jnp.sin(ang)

    ar, ai = unit(k1[:, None] * k1[None, :], n1)
    tr, ti = unit(n2[:, None] * k1[None, :], n)
    gr = ar[None] * tr[:, :, None] - ai[None] * ti[:, :, None]
    gi = ar[None] * ti[:, :, None] + ai[None] * tr[:, :, None]
    h = n1 // 2
    g1c = jnp.concatenate([jnp.concatenate([gr[:, :, :h], -gi[:, :, :h]], axis=2),
                           jnp.concatenate([gi[:, :, :h], gr[:, :, :h]], axis=2)], axis=1)
    m = jnp.arange(h, dtype=jnp.int32)

    def fold(g):
        up = jnp.take(g, n1 - 1 - m, axis=2)
        up0 = jnp.take(g[0], (n1 - m) % n1, axis=1) * (m != 0).astype(F32)
        return jnp.concatenate([g[:, :, :h], up.at[0].set(up0)], axis=2)

    g1f = jnp.concatenate([fold(gr), fold(gi)], axis=1)
    grt = jnp.swapaxes(gr, 1, 2)[:, :h, :] / n
    git = jnp.swapaxes(gi, 1, 2)[:, :h, :] / n
    g2 = jnp.concatenate([jnp.concatenate([grt, git], axis=2),
                          jnp.concatenate([-git, grt], axis=2)], axis=1)
    fr, fi = unit(n2[:, None] * n2[None, :], DFT_N2)
    f2 = jnp.concatenate([jnp.concatenate([fr, -fi], axis=1),
                          jnp.concatenate([fi, fr], axis=1)], axis=0)
    f2c = jnp.concatenate([jnp.concatenate([fr, fi], axis=1),
                           jnp.concatenate([-fi, fr], axis=1)], axis=0)
    return (g1c.astype(BF16), g1f.astype(BF16), g2.astype(BF16),
            f2.astype(BF16), f2c.astype(BF16))


def _stage1(gather, g_ref, a_ref, chunk, cn, n1, pitch):
    for q in range(cn):
        n2 = chunk * cn + q
        a = _dot(g_ref[q], gather(n2).astype(BF16))
        a_ref[0, pl.ds(n2, n1, stride=pitch), :] = a[:n1]
        a_ref[1, pl.ds(n2, n1, stride=pitch), :] = a[n1:]


def _a_rows(chunk, ck, q, pitch):
    return pl.ds(pl.multiple_of(chunk * (ck * pitch), SUBLANES) + q * pitch, DFT_N2)


def _load_a_chunk(a_ref, chunk, ck, pitch):
    tiles = []
    for q in range(ck):
        rows = _a_rows(chunk, ck, q, pitch)
        tiles.append(jnp.concatenate([a_ref[0, rows, :], a_ref[1, rows, :]], axis=0).astype(BF16))
    return jnp.concatenate(tiles, axis=1)


def _spec_kernel(t_ref, g_ref, f2_ref, o_ref, a_ref, *, n1, cn, ck, pitch, np1):
    s = pl.program_id(1)
    h = n1 // 2

    @pl.when(s < np1)
    def _():
        def gather(n2):
            back = (DFT_N2 - n2) % DFT_N2
            return jnp.concatenate([t_ref[0, pl.ds(n2, h, stride=GROUP_PITCH), :],
                                    t_ref[1, pl.ds(back, h, stride=GROUP_PITCH), :]], axis=0)
        _stage1(gather, g_ref, a_ref, s, cn, n1, pitch)

    @pl.when(s >= np1)
    def _():
        o_ref[0] = _dot(f2_ref[...], _load_a_chunk(a_ref, s - np1, ck, pitch))


def _conv_kernel(u_ref, g1_ref, kf_ref, g2_ref, f2_ref, f2c_ref, y_ref, a_ref,
                 *, n1, cn, ck, pitch, np1, np2):
    s = pl.program_id(1)
    h = n1 // 2

    @pl.when(s == 0)
    def _():
        y_ref[...] = jnp.zeros_like(y_ref)

    @pl.when(s < np1)
    def _():
        def gather(n2):
            idx = pl.ds(n2, h, stride=GROUP_PITCH)
            return jnp.concatenate([u_ref[0, idx, :], u_ref[1, idx, :]], axis=0)
        _stage1(gather, g1_ref, a_ref, s, cn, n1, pitch)

    @pl.when(jnp.logical_and(s >= np1, s < np1 + np2))
    def _():
        chunk = s - np1
        x = _dot(f2_ref[...], _load_a_chunk(a_ref, chunk, ck, pitch))
        xr, xi = x[:DFT_N2], x[DFT_N2:]
        kr, ki = kf_ref[0, :DFT_N2, :], kf_ref[0, DFT_N2:, :]
        y = jnp.concatenate([xr * kr - xi * ki, xr * ki + xi * kr], axis=0)
        b = _dot(f2c_ref[...], y.astype(BF16))
        for q in range(ck):
            rows = _a_rows(chunk, ck, q, pitch)
            a_ref[0, rows, :] = b[:DFT_N2, q * LANES:(q + 1) * LANES]
            a_ref[1, rows, :] = b[DFT_N2:, q * LANES:(q + 1) * LANES]

    @pl.when(s >= np1 + np2)
    def _():
        for q in range(cn):
            n2 = (s - np1 - np2) * cn + q
            idx = pl.ds(n2, n1, stride=pitch)
            b = jnp.concatenate([a_ref[0, idx, :], a_ref[1, idx, :]], axis=0)
            y = _dot(g2_ref[q], b.astype(BF16))
            out = pl.ds(n2, h, stride=GROUP_PITCH)
            y_ref[0, out, :] = y[:h]
            y_ref[1, out, :] = y[h:]


def _dft_sizes(n1):
    cn = 32
    ck = min(16, n1)
    return cn, ck, DFT_N2 // cn, n1 // ck, DFT_N2 + SUBLANES


def _filter_spectrum(taps, g1f, f2):
    _, rows, c = taps.shape
    n1 = 2 * rows // GROUP_PITCH
    cn, ck, np1, np2, pitch = _dft_sizes(n1)
    return pl.pallas_call(
        functools.partial(_spec_kernel, n1=n1, cn=cn, ck=ck, pitch=pitch, np1=np1),
        grid=(c // LANES, np1 + np2),
        in_specs=[pl.BlockSpec((2, rows, LANES), lambda j, s: (0, 0, j),
                               pipeline_mode=pl.Buffered(1)),
                  pl.BlockSpec((cn, 2 * n1, n1), lambda j, s: (jnp.minimum(s, np1 - 1), 0, 0)),
                  pl.BlockSpec((2 * DFT_N2, 2 * DFT_N2), lambda j, s: (0, 0))],
        out_specs=pl.BlockSpec((1, 2 * DFT_N2, ck * LANES),
                               lambda j, s: (j, 0, jnp.maximum(s - np1, 0))),
        out_shape=jax.ShapeDtypeStruct((c // LANES, 2 * DFT_N2, n1 * LANES), F32),
        scratch_shapes=[pltpu.VMEM((2, n1 * pitch, LANES), F32)],
        compiler_params=_cparams(("parallel", "arbitrary")),
        name="hy_spectrum",
    )(taps, g1f, f2)


def _long_conv(u, kf, g1c, g2, f2, f2c):
    bsz, rows, c = u.shape
    assert bsz == 2, "the two batch elements are packed as one complex signal"
    n1 = 2 * rows // GROUP_PITCH
    cn, ck, np1, np2, pitch = _dft_sizes(n1)
    once = dict(pipeline_mode=pl.Buffered(1))
    return pl.pallas_call(
        functools.partial(_conv_kernel, n1=n1, cn=cn, ck=ck, pitch=pitch, np1=np1, np2=np2),
        grid=(c // LANES, np1 + np2 + np1),
        in_specs=[
            pl.BlockSpec((2, rows, LANES), lambda j, s: (0, 0, j), **once),
            pl.BlockSpec((cn, 2 * n1, n1), lambda j, s: (jnp.minimum(s, np1 - 1), 0, 0)),
            pl.BlockSpec((1, 2 * DFT_N2, ck * LANES),
                         lambda j, s: (j, 0, jnp.clip(s - np1, 0, np2 - 1))),
            pl.BlockSpec((cn, n1, 2 * n1),
                         lambda j, s: (jnp.clip(s - np1 - np2, 0, np1 - 1), 0, 0)),
            pl.BlockSpec((2 * DFT_N2, 2 * DFT_N2), lambda j, s: (0, 0)),
            pl.BlockSpec((2 * DFT_N2, 2 * DFT_N2), lambda j, s: (0, 0)),
        ],
        out_specs=pl.BlockSpec((2, rows, LANES), lambda j, s: (0, 0, j), **once),
        out_shape=jax.ShapeDtypeStruct((2, rows, c), F32),
        scratch_shapes=[pltpu.VMEM((2, n1 * pitch, LANES), F32)],
        compiler_params=_cparams(("parallel", "arbitrary")),
        name="hy_conv",
    )(u, g1c, kf, g2, f2, f2c)


def _mixer_kernel(ya_ref, yb_ref, u_ref, x0_ref, yc_ref, gs_ref, gh_ref, x_ref, rows_ref,
                  cols_ref, d_ref, gate_ref, wglu_ref, why_ref, wout_ref, o_ref, *, tm, cw):
    first = pl.program_id(0) == 0
    y = (jnp.where(first, ya_ref[0], yb_ref[0]) + jnp.where(first, ya_ref[1], yb_ref[1])
         + u_ref[0].astype(F32) * d_ref[...])
    h = jax.nn.gelu(y, approximate=True).astype(BF16)
    y_conv = jnp.concatenate(
        [yc_ref[0, g * GROUP_PITCH:g * GROUP_PITCH + DFT_N2, :] for g in range(tm // DFT_N2)],
        axis=0)
    hx = (x0_ref[0].astype(F32) * y_conv).astype(BF16)
    d = o_ref.shape[-1]
    merged = []
    for c in range(d // cw):
        cols = slice(c * cw, (c + 1) * cw)
        glu_v = _dot(h, wglu_ref[:, cols])
        glu_g = _dot(h, wglu_ref[:, d + c * cw:d + (c + 1) * cw])
        branch_hy = _dot(hx, why_ref[:, cols])
        m = (jax.nn.sigmoid(gs_ref[0, :, cols].astype(F32)) * (glu_v * jax.nn.sigmoid(glu_g))
             + jax.nn.sigmoid(gh_ref[0, :, cols].astype(F32)) * branch_hy)
        merged.append(m.astype(BF16))
    merged = jnp.concatenate(merged, axis=-1)
    dh = d // 2
    rp, cp = _pos_block(rows_ref[0], cols_ref[...], tm)
    for c in range(d // cw):
        cols = slice(c * cw, (c + 1) * cw)
        pos = rp[:, cols] if (c + 1) * cw <= dh else cp[:, c * cw - dh:(c + 1) * cw - dh]
        o_ref[0, :, cols] = (x_ref[0, :, cols] + pos
                             + gate_ref[0, :, cols] * _dot(merged, wout_ref[:, cols]))


def _mixer(y_ssm, p_x, x0, y_conv, x, pos_tabs, s5_d, gate, w_glu, w_hy_out, w_out,
           s5w, gate_col0, tm):
    bsz, n, d = x.shape
    assert bsz == 2, "the S5 readouts arrive as one array per batch element"
    hyw = x0.shape[-1]
    rows_tab, cols_tab = pos_tabs
    nr = tm // GRID_W
    dh = d // 2
    ni = n // tm
    g0 = gate_col0 // d
    cw = min(512, dh)
    resident = dict(pipeline_mode=pl.Buffered(1))
    tok = lambda width, col: pl.BlockSpec((1, tm, width), lambda b, i: (b, i, col))
    return pl.pallas_call(
        functools.partial(_mixer_kernel, tm=tm, cw=cw),
        grid=(bsz, ni),
        in_specs=[
            pl.BlockSpec((2, tm, s5w), lambda b, i: (0, jnp.where(b == 0, i, ni - 1), 0)),
            pl.BlockSpec((2, tm, s5w), lambda b, i: (0, jnp.where(b == 0, 0, i), 0)),
            tok(s5w, 0), tok(hyw, 0),
            pl.BlockSpec((1, tm // DFT_N2 * GROUP_PITCH, hyw), lambda b, i: (b, i, 0)),
            tok(d, g0), tok(d, g0 + 1), tok(d, 0),
            pl.BlockSpec((1, nr, dh), lambda b, i: (i, 0, 0)),
            pl.BlockSpec((GRID_W, dh), lambda b, i: (0, 0)),
            pl.BlockSpec((1, s5w), lambda b, i: (0, 0)),
            pl.BlockSpec((1, 1, d), lambda b, i: (b, 0, 0)),
            pl.BlockSpec((s5w, 2 * d), lambda b, i: (0, 0), **resident),
            pl.BlockSpec((hyw, d), lambda b, i: (0, 0), **resident),
            pl.BlockSpec((d, d), lambda b, i: (0, 0), **resident),
        ],
        out_specs=pl.BlockSpec((1, tm, d), lambda b, i: (b, i, 0)),
        out_shape=jax.ShapeDtypeStruct((bsz, n, d), F32),
        compiler_params=_cparams(("arbitrary", "arbitrary")),
        name="mixer",
    )(y_ssm[0], y_ssm[1], p_x, x0, y_conv, p_x, p_x, x,
      rows_tab.reshape(-1, nr, dh), cols_tab, s5_d, gate, w_glu, w_hy_out, w_out)


def _ffn_kernel(x_ref, g_ref, sh_ref, sc_ref, gate_ref, wa_ref, wb_ref, wo_ref, nf_ref,
                o_ref, h_ref, acc_ref):
    j = pl.program_id(2)

    @pl.when(j == 0)
    def _():
        h_ref[...] = _rms_mod(x_ref[0], g_ref[...], sh_ref[0], sc_ref[0]).astype(BF16)
        acc_ref[...] = jnp.zeros_like(acc_ref)

    h = h_ref[...]
    act = _silu(_dot(h, wa_ref[...])) * _dot(h, wb_ref[...])
    acc_ref[...] += _dot(act.astype(BF16), wo_ref[...])

    @pl.when(j == pl.num_programs(2) - 1)
    def _():
        xo = x_ref[0] + gate_ref[0] * acc_ref[...]
        o_ref[0] = xo * lax.rsqrt(jnp.mean(xo * xo, axis=-1, keepdims=True) + EPS) * nf_ref[...]


def _ffn(x, g, shift, scale, gate, w_in, w_out, norm_f, tm, tf):
    bsz, n, d = x.shape
    dff = w_out.shape[0]
    nj = dff // tf
    vec = pl.BlockSpec((1, 1, d), lambda b, i, j: (b, 0, 0))
    return pl.pallas_call(
        _ffn_kernel,
        grid=(bsz, n // tm, nj),
        in_specs=[pl.BlockSpec((1, tm, d), lambda b, i, j: (b, i, 0)),
                  pl.BlockSpec((1, d), lambda b, i, j: (0, 0)),
                  vec, vec, vec,
                  pl.BlockSpec((d, tf), lambda b, i, j: (0, j)),
                  pl.BlockSpec((d, tf), lambda b, i, j: (0, nj + j)),
                  pl.BlockSpec((tf, d), lambda b, i, j: (j, 0)),
                  pl.BlockSpec((1, d), lambda b, i, j: (0, 0))],
        out_specs=pl.BlockSpec((1, tm, d), lambda b, i, j: (b, i, 0)),
        out_shape=jax.ShapeDtypeStruct((bsz, n, d), F32),
        scratch_shapes=[pltpu.VMEM((tm, d), BF16), pltpu.VMEM((tm, d), F32)],
        compiler_params=_cparams(("parallel", "parallel", "arbitrary")),
        name="ffn",
    )(x, g, shift, scale, gate, w_in, w_in, w_out, norm_f)


def _pos_tables(n_rows, d):
    quarter = d // 4
    omega = 10000.0 ** (-jnp.arange(quarter, dtype=F32) / quarter)
    ar = jnp.arange(n_rows, dtype=F32)[:, None] * omega
    ac = jnp.arange(GRID_W, dtype=F32)[:, None] * omega
    return (jnp.concatenate([jnp.sin(ar), jnp.cos(ar)], axis=-1),
            jnp.concatenate([jnp.sin(ac), jnp.cos(ac)], axis=-1))


def _filter_features(n_lat, width):
    pos = jnp.arange(n_lat, dtype=F32)
    t = pos / float(max(n_lat - 1, 1))
    w = 2.0 * math.pi * pos / n_lat
    bands = jnp.linspace(1e-4, HY_BANDS - 1, HY_BANDS, dtype=F32)
    feats = jnp.concatenate([t[:, None], jnp.cos(w[:, None] * bands), -jnp.sin(w[:, None] * bands)],
                            axis=-1)
    return jnp.pad(feats, ((0, 0), (0, width - feats.shape[1])))


def kernel(x, c, ctx, c_ctx, w_ada, b_ada, norm_mix, w_in, s5_a_re, s5_a_im, s5_log_dt,
           s5_b_re, s5_b_im, s5_c_re, s5_c_im, s5_d, w_glu, hy_conv_w, hy_conv_b,
           hy_f1_w, hy_f1_b, hy_f1_freq, hy_f2_w, hy_f2_b, hy_f2_freq, hy_f3_w, hy_decay,
           hy_bias, w_hy_out, w_out, norm_ffn, w_ffn_in, w_ffn_out, norm_f):
    bsz, n_lat, d = x.shape
    depth = w_ada.shape[0]
    assert depth == 1, "the context stream is only advanced for the single-layer trunk"
    l = 0
    s5w = s5_d.shape[-1]
    hyw = hy_bias.shape[-1]
    n_ctx = ctx.shape[1]
    tm = 512
    tn = min(1024, d)
    tb = 256
    assert n_ctx % tb == 0 and n_lat % tm == 0 and s5w == S5_CHUNKS * LANES

    pos_tabs = _pos_tables(n_lat // GRID_W, d)

    cond_t = jnp.zeros((d, SUBLANES), F32).at[:, :bsz].set(c.T).at[:, bsz].set(c_ctx)
    ada = _ada(cond_t, w_ada[l], b_ada[l][None], bsz + 1, tn=min(1024, d))

    def vec(row0, rows, part):
        v = ada[row0:row0 + rows, part * d:(part + 1) * d]
        return jnp.broadcast_to(v, (bsz, d))[:, None, :]

    shift_mix, scale_mix, gate_mix = vec(0, bsz, 0), vec(0, bsz, 1), vec(0, bsz, 2)
    shift_ffn, scale_ffn, gate_ffn = vec(0, bsz, 3), vec(0, bsz, 4), vec(0, bsz, 5)
    cshift_mix, cscale_mix = vec(bsz, 1, 0), vec(bsz, 1, 1)

    w_in_b = w_in[l].astype(BF16)
    g_mix = norm_mix[l][None]
    u_c = _ctx_proj(ctx, g_mix, cshift_mix, cscale_mix, w_in_b, s5w, tb)
    p_x = _in_proj_latent(x, pos_tabs, g_mix, shift_mix, scale_mix, w_in_b, tm, tn)

    packed = [_s5_params(s5_a_re[l, k], s5_a_im[l, k], s5_log_dt[l, k], s5_b_re[l, k],
                         s5_b_im[l, k], s5_c_re[l, k], s5_c_im[l, k]) for k in range(2)]
    bd, cd, lam_r, lam_i = (jnp.stack(t) for t in zip(*packed))
    y_ssm = _s5(u_c, p_x, bd, cd, lam_r, lam_i, s5w, tb)

    fp = 64
    feats = _filter_features(n_lat, fp)
    w1 = jnp.pad(hy_f1_w[l], ((0, fp - hy_f1_w.shape[1]), (0, 0)))
    taps = _filter_taps(feats, w1, hy_f1_b[l][None], hy_f1_freq[l][None], hy_f2_w[l],
                        hy_f2_b[l][None], hy_f2_freq[l][None], hy_f3_w[l], hy_decay[l],
                        hy_bias[l][None])
    g1c, g1f, g2, f2, f2c = _dft_tables(2 * n_lat // DFT_N2)
    kf = _filter_spectrum(taps, g1f, f2)
    x0, u_hy = _short_conv(p_x, hy_conv_w[l], hy_conv_b[l][None], s5w, hyw, tm)
    y_conv = _long_conv(u_hy, kf, g1c, g2, f2, f2c)

    x1 = _mixer(y_ssm, p_x, x0, y_conv, x, pos_tabs, s5_d[l][None], gate_mix,
                w_glu[l].astype(BF16), w_hy_out[l].astype(BF16), w_out[l].astype(BF16),
                s5w, s5w + 3 * hyw, 256)

    return _ffn(x1, norm_ffn[l][None], shift_ffn, scale_ffn, gate_ffn,
                w_ffn_in[l].astype(BF16), w_ffn_out[l].astype(BF16), norm_f[None], tm, 512)
```

```python
import functools
import math

import jax
import jax.numpy as jnp
from jax import lax
from jax.experimental import pallas as pl
from jax.experimental.pallas import tpu as pltpu

F32 = jnp.float32
BF16 = jnp.bfloat16
HIGHEST = lax.Precision.HIGHEST

GRID_W = 64
N_ADA = 6
EPS = 1e-6
HY_BANDS = 16
LANES = 128
SUBLANES = 8
DFT_N2 = LANES
GROUP_PITCH = DFT_N2 + SUBLANES // 2
VMEM_LIMIT = 60 * 1024 * 1024


def _cparams(sem):
    return pltpu.CompilerParams(dimension_semantics=sem, vmem_limit_bytes=VMEM_LIMIT)


def _dot(a, b):
    return jnp.dot(a, b, preferred_element_type=F32)


def _dot_hi(a, b):
    return jnp.dot(a, b, preferred_element_type=F32, precision=HIGHEST)


def _silu(x):
    return x * jax.nn.sigmoid(x)


def _rms_mod(x, g, shift, scale):
    y = x * lax.rsqrt(jnp.mean(x * x, axis=-1, keepdims=True) + EPS)
    return (y * g) * (1.0 + scale) + shift


def _pos_block(rows, cols, tm):
    nr = tm // GRID_W
    dh = rows.shape[-1]
    rp = jnp.broadcast_to(rows[:, None, :], (nr, GRID_W, dh)).reshape(tm, dh)
    cp = jnp.broadcast_to(cols[None], (nr, GRID_W, dh)).reshape(tm, dh)
    return rp, cp


def _store_padded(ref, lead, val):
    pad = jnp.zeros((GROUP_PITCH - DFT_N2, val.shape[-1]), val.dtype)
    for g in range(val.shape[0] // DFT_N2):
        ref[lead, g * GROUP_PITCH:g * GROUP_PITCH + DFT_N2, :] = val[g * DFT_N2:(g + 1) * DFT_N2]
        ref[lead, g * GROUP_PITCH + DFT_N2:(g + 1) * GROUP_PITCH, :] = pad


def _ada_kernel(ct_ref, w_ref, b_ref, o_ref, *, n_vec):
    sc = _silu(ct_ref[...])
    w = w_ref[...]
    rows = [jnp.sum(w * sc[:, r:r + 1], axis=0, keepdims=True) for r in range(n_vec)]
    rows += [jnp.zeros_like(rows[0])] * (SUBLANES - n_vec)
    o_ref[...] = jnp.concatenate(rows, axis=0) + b_ref[...]


def _ada(cond_t, w, b, n_vec, tn=1024):
    d, n = w.shape
    return pl.pallas_call(
        functools.partial(_ada_kernel, n_vec=n_vec),
        grid=(n // tn,),
        in_specs=[pl.BlockSpec((d, SUBLANES), lambda j: (0, 0)),
                  pl.BlockSpec((d, tn), lambda j: (0, j)),
                  pl.BlockSpec((1, tn), lambda j: (0, j))],
        out_specs=pl.BlockSpec((SUBLANES, tn), lambda j: (0, j)),
        out_shape=jax.ShapeDtypeStruct((SUBLANES, n), F32),
        compiler_params=_cparams(("arbitrary",)),
        name="ada",
    )(cond_t, w, b)


def _ctx_kernel(x_ref, g_ref, sh_ref, sc_ref, w_ref, o_ref):
    h = _rms_mod(x_ref[0], g_ref[...], sh_ref[0], sc_ref[0]).astype(BF16)
    o_ref[0] = _dot(h, w_ref[...]).astype(o_ref.dtype)


def _ctx_proj(x, g, shift, scale, w, n_cols, tm):
    bsz, n, d = x.shape
    vec = pl.BlockSpec((1, 1, d), lambda b, i: (b, 0, 0))
    return pl.pallas_call(
        _ctx_kernel,
        grid=(bsz, n // tm),
        in_specs=[pl.BlockSpec((1, tm, d), lambda b, i: (b, i, 0)),
                  pl.BlockSpec((1, d), lambda b, i: (0, 0)), vec, vec,
                  pl.BlockSpec((d, n_cols), lambda b, i: (0, 0))],
        out_specs=pl.BlockSpec((1, tm, n_cols), lambda b, i: (b, i, 0)),
        out_shape=jax.ShapeDtypeStruct((bsz, n, n_cols), BF16),
        compiler_params=_cparams(("parallel", "parallel")),
        name="ctx_proj",
    )(x, g, shift, scale, w)


PRO_CHUNKS = 8


def _next_block(b, i, ni, nb):
    t = jnp.minimum(b * ni + i + 1, nb * ni - 1)
    return t // ni, t % ni


def _in_pipe_kernel(x0_ref, rows0_ref, sh0_ref, sc0_ref, xn_ref, rowsn_ref, shn_ref, scn_ref,
                    cols_ref, g_ref, w_ref, o_ref, h_ref, *, tm):
    b, i, j = pl.program_id(0), pl.program_id(1), pl.program_id(2)
    par = (b * pl.num_programs(1) + i) % 2
    cr = tm // PRO_CHUNKS
    dh = cols_ref.shape[-1]

    @pl.when(jnp.logical_and(jnp.logical_and(b == 0, i == 0), j == 0))
    def _():
        x = x0_ref[0] + jnp.concatenate(_pos_block(rows0_ref[...], cols_ref[...], tm), axis=-1)
        h_ref[0] = _rms_mod(x, g_ref[...], sh0_ref[0], sc0_ref[0]).astype(BF16)

    o_ref[0] = _dot(h_ref[par], w_ref[...]).astype(o_ref.dtype)

    c = jnp.minimum(j, PRO_CHUNKS - 1)
    rows = pl.ds(pl.multiple_of(c * cr, cr), cr)
    gpc = cr // GRID_W
    rp = jnp.concatenate([jnp.broadcast_to(rowsn_ref[pl.ds(c * gpc + r, 1), :], (GRID_W, dh))
                          for r in range(gpc)], axis=0)
    cp = jnp.concatenate([cols_ref[...]] * gpc, axis=0)
    xc = xn_ref[0, rows, :] + jnp.concatenate([rp, cp], axis=-1)
    h_ref[1 - par, rows, :] = _rms_mod(xc, g_ref[...], shn_ref[0], scn_ref[0]).astype(BF16)


def _in_proj_latent(x, pos_tabs, g, shift, scale, w, tm, tn):
    bsz, n, d = x.shape
    n_cols = w.shape[1]
    rows_tab, cols_tab = pos_tabs
    dh = d // 2
    ni = n // tm
    nr = tm // GRID_W
    assert (tm // PRO_CHUNKS) % GRID_W == 0 and n_cols // tn >= PRO_CHUNKS
    nxt = lambda b, i: _next_block(b, i, ni, bsz)
    once = dict(pipeline_mode=pl.Buffered(1))
    return pl.pallas_call(
        functools.partial(_in_pipe_kernel, tm=tm),
        grid=(bsz, ni, n_cols // tn),
        in_specs=[
            pl.BlockSpec((1, tm, d), lambda b, i, j: (0, 0, 0), **once),
            pl.BlockSpec((nr, dh), lambda b, i, j: (0, 0)),
            pl.BlockSpec((1, 1, d), lambda b, i, j: (0, 0, 0)),
            pl.BlockSpec((1, 1, d), lambda b, i, j: (0, 0, 0)),
            pl.BlockSpec((1, tm, d), lambda b, i, j: nxt(b, i) + (0,)),
            pl.BlockSpec((nr, dh), lambda b, i, j: (nxt(b, i)[1], 0)),
            pl.BlockSpec((1, 1, d), lambda b, i, j: (nxt(b, i)[0], 0, 0)),
            pl.BlockSpec((1, 1, d), lambda b, i, j: (nxt(b, i)[0], 0, 0)),
            pl.BlockSpec((GRID_W, dh), lambda b, i, j: (0, 0)),
            pl.BlockSpec((1, d), lambda b, i, j: (0, 0)),
            pl.BlockSpec((d, tn), lambda b, i, j: (0, j)),
        ],
        out_specs=pl.BlockSpec((1, tm, tn), lambda b, i, j: (b, i, j)),
        out_shape=jax.ShapeDtypeStruct((bsz, n, n_cols), BF16),
        scratch_shapes=[pltpu.VMEM((2, tm, d), BF16)],
        compiler_params=_cparams(("arbitrary", "arbitrary", "arbitrary")),
        name="in_proj",
    )(x, rows_tab, shift, scale, x, rows_tab, shift, scale, cols_tab, g, w)


S5_CHUNKS = 8


def _s5_rows(k, tb, pitch):
    pad = pitch - tb
    if (k * pitch) % SUBLANES == 0:
        return k * pitch, k * pitch + tb
    return k * pitch - pad, k * pitch + tb + pad


def _s5_fill(buf, lhs_of, bd_ref, k, tb, pitch):
    lhs = lhs_of(k)
    lo, hi = _s5_rows(k, tb, pitch)
    if lo != k * pitch:
        z = jnp.zeros((k * pitch - lo, LANES), F32)
        lhs = jnp.concatenate([z, lhs.astype(F32), z], axis=0).astype(BF16)
    bu = _dot(lhs, bd_ref[0, k])
    for m in range(buf.shape[0]):
        buf[m, lo:hi, :] = bu[:, m * LANES:(m + 1) * LANES]


def _s5_pass(d, bu_scan, st_scan, carry_ref, st_mm, bu_mm, lhs_of, y_ref, bd_ref, cd_ref,
             lam_r, lam_i, tb, pitch):
    nslab = bu_scan.shape[0]
    half = nslab // 2
    per = tb // S5_CHUNKS
    sr = [carry_ref[m] for m in range(half)]
    si = [carry_ref[half + m] for m in range(half)]

    def scan_steps(q0, q1):
        for q in range(q0, q1):
            idx = pl.ds(jnp.where(d == 0, q, tb - 1 - q), SUBLANES, stride=pitch)
            for m in range(half):
                nr = lam_r[m] * sr[m] - lam_i[m] * si[m] + bu_scan[m, idx, :]
                ni = lam_r[m] * si[m] + lam_i[m] * sr[m] + bu_scan[half + m, idx, :]
                st_scan[m, idx, :] = nr
                st_scan[half + m, idx, :] = ni
                sr[m], si[m] = nr, ni

    for k in range(S5_CHUNKS):
        lo, hi = _s5_rows(k, tb, pitch)
        sk = jnp.concatenate([st_mm[m, lo:hi, :] for m in range(nslab)], axis=-1)
        yk = _dot(sk.astype(BF16), cd_ref[0, k])
        y_ref[0, :, k * LANES:(k + 1) * LANES] = yk[k * pitch - lo:k * pitch - lo + tb]
        scan_steps(k * per, k * per + per // 2)
        _s5_fill(bu_mm, lhs_of, bd_ref, k, tb, pitch)
        scan_steps(k * per + per // 2, (k + 1) * per)
    for m in range(half):
        carry_ref[m] = sr[m]
        carry_ref[half + m] = si[m]


def _s5_kernel(uc0_ref, uca_ref, uxa_ref, ucb_ref, uxb_ref, bd_ref, cd_ref, lr_ref, li_ref,
               ya_ref, yb_ref, a_bu, a_st, b_bu, b_st, ca_ref, cb_ref, *, tb, pitch, nc, nblk):
    d = pl.program_id(0)
    i = pl.program_id(1)
    half = a_bu.shape[0] // 2
    lam_r = [lr_ref[0, m] for m in range(half)]
    lam_i = [li_ref[0, m] for m in range(half)]

    @pl.when(i == 0)
    def _():
        ca_ref[...] = jnp.zeros_like(ca_ref)
        cb_ref[...] = jnp.zeros_like(cb_ref)
        a_st[...] = jnp.zeros_like(a_st)
        b_st[...] = jnp.zeros_like(b_st)
        for k in range(S5_CHUNKS):
            _s5_fill(a_bu, lambda k: uc0_ref[0, :, k * LANES:(k + 1) * LANES], bd_ref, k, tb, pitch)

    def lhs(uc_ref, ux_ref, pos):
        def of(k):
            cols = slice(k * LANES, (k + 1) * LANES)
            return jnp.where(pos < nc, uc_ref[0, :, cols], ux_ref[0, :, cols])
        return of

    _s5_pass(d, a_bu, a_st, ca_ref, b_st, b_bu, lhs(ucb_ref, uxb_ref, i), yb_ref,
             bd_ref, cd_ref, lam_r, lam_i, tb, pitch)

    @pl.when(i < nblk)
    def _():
        _s5_pass(d, b_bu, b_st, cb_ref, a_st, a_bu, lhs(uca_ref, uxa_ref, i + 1), ya_ref,
                 bd_ref, cd_ref, lam_r, lam_i, tb, pitch)


def _s5(u_c, p_x, bd, cd, lam_r, lam_i, s5w, tb):
    bsz, lc, _ = u_c.shape
    assert bsz == 2, "the two batch elements are the two interleaved sequences"
    n = p_x.shape[1]
    nc, nb = lc // tb, n // tb
    nblk = nc + nb
    nslab = bd.shape[-1] // LANES
    pitch = tb + SUBLANES // 2

    def blk(d, pos, count):
        j = jnp.clip(pos, 0, count - 1)
        return jnp.where(d == 0, j, count - 1 - j)

    def u_spec(b, off, ctx):
        if ctx:
            return pl.BlockSpec((1, tb, s5w), lambda d, i: (b, blk(d, i + off, nc), 0))
        return pl.BlockSpec((1, tb, s5w), lambda d, i: (b, blk(d, i + off - nc, nb), 0))

    def y_spec(off):
        return pl.BlockSpec((1, tb, s5w), lambda d, i: (d, blk(d, i + off - nc, nb), 0))

    par = lambda d, i: (d, 0, 0, 0)
    buf = pltpu.VMEM((nslab, S5_CHUNKS * pitch, LANES), F32)
    state = pltpu.VMEM((nslab, SUBLANES, LANES), F32)
    return pl.pallas_call(
        functools.partial(_s5_kernel, tb=tb, pitch=pitch, nc=nc, nblk=nblk),
        grid=(2, nblk + 1),
        in_specs=[
            pl.BlockSpec((1, tb, s5w), lambda d, i: (0, blk(d, 0, nc), 0)),
            u_spec(0, 1, True), u_spec(0, 1, False),
            u_spec(1, 0, True), u_spec(1, 0, False),
            pl.BlockSpec((1, S5_CHUNKS, LANES, nslab * LANES), par, pipeline_mode=pl.Buffered(1)),
            pl.BlockSpec((1, S5_CHUNKS, nslab * LANES, LANES), par, pipeline_mode=pl.Buffered(1)),
            pl.BlockSpec((1, nslab // 2, SUBLANES, LANES), par),
            pl.BlockSpec((1, nslab // 2, SUBLANES, LANES), par),
        ],
        out_specs=[y_spec(0), y_spec(-1)],
        out_shape=[jax.ShapeDtypeStruct((2, n, s5w), F32)] * 2,
        scratch_shapes=[buf, buf, buf, buf, state, state],
        compiler_params=_cparams(("arbitrary", "arbitrary")),
        name="s5",
    )(u_c, u_c, p_x, u_c, p_x, bd, cd, lam_r, lam_i)


def _s5_params(a_re, a_im, log_dt, b_re, b_im, c_re, c_im):
    g, p, h = b_re.shape
    gl = g // S5_CHUNKS
    dt = jnp.exp(log_dt.astype(F32))[:, None]
    mag = jnp.exp(a_re.astype(F32) * dt)
    lr, li = mag * jnp.cos(a_im.astype(F32) * dt), mag * jnp.sin(a_im.astype(F32) * dt)
    den = a_re * a_re + a_im * a_im
    qr = ((lr - 1.0) * a_re + li * a_im) / den
    qi = (li * a_re - (lr - 1.0) * a_im) / den
    bbr = qr[:, :, None] * b_re - qi[:, :, None] * b_im
    bbi = qr[:, :, None] * b_im + qi[:, :, None] * b_re
    eye = jnp.eye(gl, dtype=F32)

    def pack_b(m):
        return jnp.einsum("kgph,gq->kghqp", m.reshape(S5_CHUNKS, gl, p, h), eye).reshape(
            S5_CHUNKS, gl * h, gl * p)

    def pack_c(m):
        return jnp.einsum("kghp,gq->kgpqh", m.reshape(S5_CHUNKS, gl, h, p), eye).reshape(
            S5_CHUNKS, gl * p, gl * h)

    bd = jnp.concatenate([pack_b(bbr), pack_b(bbi)], axis=-1).astype(BF16)
    cd = jnp.concatenate([pack_c(c_re.astype(F32)), -pack_c(c_im.astype(F32))], axis=1).astype(BF16)

    def slabs(v):
        return v.reshape(S5_CHUNKS, -1, LANES).transpose(1, 0, 2)

    return bd, cd, slabs(lr), slabs(li)


def _filt_kernel(f_ref, w1_ref, b1_ref, q1_ref, w2_ref, b2_ref, q2_ref, w3_ref, dec_ref,
                 bias_ref, o_ref, *, tq):
    f = f_ref[...]
    h = jnp.sin(q1_ref[...] * (_dot_hi(f, w1_ref[...]) + b1_ref[...]))
    h = jnp.sin(q2_ref[...] * (_dot_hi(h, w2_ref[...]) + b2_ref[...]))
    h = _dot_hi(h, w3_ref[...])
    c = bias_ref.shape[-1]
    t = f[:, 0:1]
    lag = pl.program_id(0) * tq + lax.broadcasted_iota(jnp.int32, (tq, 1), 0)
    fwd = h[:, :c] * jnp.exp(-t * jnp.abs(dec_ref[0:1, :]))
    bwd = h[:, c:] * jnp.exp(-t * jnp.abs(dec_ref[1:2, :]))
    fwd = jnp.where(lag == 0, fwd + bias_ref[...], fwd)
    bwd = jnp.where(lag == 0, 0.0, bwd)
    _store_padded(o_ref, 0, fwd)
    _store_padded(o_ref, 1, bwd)


def _filter_taps(feats, w1, b1, q1, w2, b2, q2, w3, decay, bias, tq=512):
    n_lat, fp = feats.shape
    hid = w2.shape[0]
    c = bias.shape[-1]
    rows = tq // DFT_N2 * GROUP_PITCH
    const = lambda r: (0, 0)
    return pl.pallas_call(
        functools.partial(_filt_kernel, tq=tq),
        grid=(n_lat // tq,),
        in_specs=[pl.BlockSpec((tq, fp), lambda r: (r, 0)),
                  pl.BlockSpec((fp, hid), const), pl.BlockSpec((1, hid), const),
                  pl.BlockSpec((1, hid), const),
                  pl.BlockSpec((hid, hid), const), pl.BlockSpec((1, hid), const),
                  pl.BlockSpec((1, hid), const),
                  pl.BlockSpec((hid, 2 * c), const),
                  pl.BlockSpec((2, c), const),
                  pl.BlockSpec((1, c), const)],
        out_specs=pl.BlockSpec((2, rows, c), lambda r: (0, r, 0)),
        out_shape=jax.ShapeDtypeStruct((2, n_lat // DFT_N2 * GROUP_PITCH, c), F32),
        compiler_params=_cparams(("parallel",)),
        name="hy_filter",
    )(feats, w1, b1, q1, w2, b2, q2, w3, decay, bias)


def _conv3(cur_ref, prev_ref, next_ref, w_ref, b_ref, i, last, tm):
    row = lax.broadcasted_iota(jnp.int32, (tm, 1), 0)
    cur = cur_ref[0].astype(F32)
    halo = prev_ref.shape[1]
    prev_row = jnp.where(i == 0, 0.0, prev_ref[0, halo - 1:halo, :].astype(F32))
    next_row = jnp.where(i == last, 0.0, next_ref[0, 0:1, :].astype(F32))
    up = jnp.where(row == 0, prev_row, pltpu.roll(cur, 1, 0))
    dn = jnp.where(row == tm - 1, next_row, pltpu.roll(cur, tm - 1, 0))
    w = w_ref[...]
    return up * w[0:1] + cur * w[1:2] + dn * w[2:3] + b_ref[...]


CONV_HALO = 16


def _conv3_specs(tm, c, n, col_block):
    hb = tm // CONV_HALO
    nhb = n // CONV_HALO
    return [pl.BlockSpec((1, tm, c), lambda b, i: (b, i, col_block)),
            pl.BlockSpec((1, CONV_HALO, c),
                         lambda b, i: (b, jnp.maximum(i * hb - 1, 0), col_block)),
            pl.BlockSpec((1, CONV_HALO, c),
                         lambda b, i: (b, jnp.minimum((i + 1) * hb, nhb - 1), col_block))]


def _sconv_kernel(x1_ref, x1p_ref, x1n_ref, v_ref, vp_ref, vn_ref, w1_ref, wv_ref, b1_ref, bv_ref,
                  u_ref, *, tm):
    i = pl.program_id(1)
    last = pl.num_programs(1) - 1
    x1 = _conv3(x1_ref, x1p_ref, x1n_ref, w1_ref, b1_ref, i, last, tm)
    v = _conv3(v_ref, vp_ref, vn_ref, wv_ref, bv_ref, i, last, tm)
    _store_padded(u_ref, 0, x1 * v)


def _short_conv(p_x, conv_w, conv_b, col0, c, tm):
    bsz, n, _ = p_x.shape
    cb0 = col0 // c
    part = lambda p, shape: pl.BlockSpec(shape, lambda b, i: (0, p))
    return pl.pallas_call(
        functools.partial(_sconv_kernel, tm=tm),
        grid=(bsz, n // tm),
        in_specs=(_conv3_specs(tm, c, n, cb0 + 1) + _conv3_specs(tm, c, n, cb0 + 2)
                  + [part(1, (3, c)), part(2, (3, c)), part(1, (1, c)), part(2, (1, c))]),
        out_specs=pl.BlockSpec((1, tm // DFT_N2 * GROUP_PITCH, c), lambda b, i: (b, i, 0)),
        out_shape=jax.ShapeDtypeStruct((bsz, n // DFT_N2 * GROUP_PITCH, c), F32),
        compiler_params=_cparams(("parallel", "parallel")),
        name="short_conv",
    )(p_x, p_x, p_x, p_x, p_x, p_x, conv_w, conv_w, conv_b, conv_b)


def _dft_tables(n1):
    n = n1 * DFT_N2
    k1 = jnp.arange(n1, dtype=jnp.int32)
    n2 = jnp.arange(DFT_N2, dtype=jnp.int32)

    def unit(ph, period):
        ang = (ph % period).astype(F32) * F32(2.0 * math.pi / period)
        return jnp.cos(ang), -jnp.sin(ang)

    ar, ai = unit(k1[:, None] * k1[None, :], n1)
    tr, ti = unit(n2[:, None] * k1[None, :], n)
    gr = ar[None] * tr[:, :, None] - ai[None] * ti[:, :, None]
    gi = ar[None] * ti[:, :, None] + ai[None] * tr[:, :, None]
    h = n1 // 2
    g1c = jnp.concatenate([jnp.concatenate([gr[:, :, :h], -gi[:, :, :h]], axis=2),
                           jnp.concatenate([gi[:, :, :h], gr[:, :, :h]], axis=2)], axis=1)
    m = jnp.arange(h, dtype=jnp.int32)

    def fold(g):
        up = jnp.take(g, n1 - 1 - m, axis=2)
        up0 = jnp.take(g[0], (n1 - m) % n1, axis=1) * (m != 0).astype(F32)
        return jnp.concatenate([g[:, :, :h], up.at[0].set(up0)], axis=2)

    g1f = jnp.concatenate([fold(gr), fold(gi)], axis=1)
    grt = jnp.swapaxes(gr, 1, 2)[:, :h, :] / n
    git = jnp.swapaxes(gi, 1, 2)[:, :h, :] / n
    g2 = jnp.concatenate([jnp.concatenate([grt, git], axis=2),
                          jnp.concatenate([-git, grt], axis=2)], axis=1)
    fr, fi = unit(n2[:, None] * n2[None, :], DFT_N2)
    f2 = jnp.concatenate([jnp.concatenate([fr, -fi], axis=1),
                          jnp.concatenate([fi, fr], axis=1)], axis=0)
    f2c = jnp.concatenate([jnp.concatenate([fr, fi], axis=1),
                           jnp.concatenate([-fi, fr], axis=1)], axis=0)
    return (g1c.astype(BF16), g1f.astype(BF16), g2.astype(BF16),
            f2.astype(BF16), f2c.astype(BF16))


def _stage1(gather, g_ref, a_ref, chunk, cn, n1, pitch):
    for q in range(cn):
        n2 = chunk * cn + q
        a = _dot(g_ref[q], gather(n2).astype(BF16))
        a_ref[0, pl.ds(n2, n1, stride=pitch), :] = a[:n1]
        a_ref[1, pl.ds(n2, n1, stride=pitch), :] = a[n1:]


def _a_rows(chunk, ck, q, pitch):
    return pl.ds(pl.multiple_of(chunk * (ck * pitch), SUBLANES) + q * pitch, DFT_N2)


def _load_a_chunk(a_ref, chunk, ck, pitch):
    tiles = []
    for q in range(ck):
        rows = _a_rows(chunk, ck, q, pitch)
        tiles.append(jnp.concatenate([a_ref[0, rows, :], a_ref[1, rows, :]], axis=0).astype(BF16))
    return jnp.concatenate(tiles, axis=1)


def _spec_kernel(t_ref, g_ref, f2_ref, o_ref, a_ref, *, n1, cn, ck, pitch, np1):
    s = pl.program_id(1)
    h = n1 // 2

    @pl.when(s < np1)
    def _():
        def gather(n2):
            back = (DFT_N2 - n2) % DFT_N2
            return jnp.concatenate([t_ref[0, pl.ds(n2, h, stride=GROUP_PITCH), :],
                                    t_ref[1, pl.ds(back, h, stride=GROUP_PITCH), :]], axis=0)
        _stage1(gather, g_ref, a_ref, s, cn, n1, pitch)

    @pl.when(s >= np1)
    def _():
        o_ref[0] = _dot(f2_ref[...], _load_a_chunk(a_ref, s - np1, ck, pitch)).astype(o_ref.dtype)


def _conv_kernel(u_ref, g1_ref, kf_ref, g2_ref, f2_ref, f2c_ref, y_ref, a_ref,
                 *, n1, cn, ck, pitch, np1, np2):
    s = pl.program_id(1)
    h = n1 // 2

    @pl.when(s == 0)
    def _():
        y_ref[...] = jnp.zeros_like(y_ref)

    @pl.when(s < np1)
    def _():
        def gather(n2):
            idx = pl.ds(n2, h, stride=GROUP_PITCH)
            return jnp.concatenate([u_ref[0, idx, :], u_ref[1, idx, :]], axis=0)
        _stage1(gather, g1_ref, a_ref, s, cn, n1, pitch)

    @pl.when(jnp.logical_and(s >= np1, s < np1 + np2))
    def _():
        chunk = s - np1
        x = _dot(f2_ref[...], _load_a_chunk(a_ref, chunk, ck, pitch))
        xr, xi = x[:DFT_N2], x[DFT_N2:]
        kr, ki = kf_ref[0, :DFT_N2, :].astype(F32), kf_ref[0, DFT_N2:, :].astype(F32)
        y = jnp.concatenate([xr * kr - xi * ki, xr * ki + xi * kr], axis=0)
        b = _dot(f2c_ref[...], y.astype(BF16))
        for q in range(ck):
            rows = _a_rows(chunk, ck, q, pitch)
            a_ref[0, rows, :] = b[:DFT_N2, q * LANES:(q + 1) * LANES]
            a_ref[1, rows, :] = b[DFT_N2:, q * LANES:(q + 1) * LANES]

    @pl.when(s >= np1 + np2)
    def _():
        for q in range(cn):
            n2 = (s - np1 - np2) * cn + q
            idx = pl.ds(n2, n1, stride=pitch)
            b = jnp.concatenate([a_ref[0, idx, :], a_ref[1, idx, :]], axis=0)
            y = _dot(g2_ref[q], b.astype(BF16))
            out = pl.ds(n2, h, stride=GROUP_PITCH)
            y_ref[0, out, :] = y[:h]
            y_ref[1, out, :] = y[h:]


def _dft_sizes(n1):
    cn = 32
    ck = min(16, n1)
    return cn, ck, DFT_N2 // cn, n1 // ck, DFT_N2 + SUBLANES


def _filter_spectrum(taps, g1f, f2):
    _, rows, c = taps.shape
    n1 = 2 * rows // GROUP_PITCH
    cn, ck, np1, np2, pitch = _dft_sizes(n1)
    return pl.pallas_call(
        functools.partial(_spec_kernel, n1=n1, cn=cn, ck=ck, pitch=pitch, np1=np1),
        grid=(c // LANES, np1 + np2),
        in_specs=[pl.BlockSpec((2, rows, LANES), lambda j, s: (0, 0, j),
                               pipeline_mode=pl.Buffered(1)),
                  pl.BlockSpec((cn, 2 * n1, n1), lambda j, s: (jnp.minimum(s, np1 - 1), 0, 0)),
                  pl.BlockSpec((2 * DFT_N2, 2 * DFT_N2), lambda j, s: (0, 0))],
        out_specs=pl.BlockSpec((1, 2 * DFT_N2, ck * LANES),
                               lambda j, s: (j, 0, jnp.maximum(s - np1, 0))),
        out_shape=jax.ShapeDtypeStruct((c // LANES, 2 * DFT_N2, n1 * LANES), BF16),
        scratch_shapes=[pltpu.VMEM((2, n1 * pitch, LANES), F32)],
        compiler_params=_cparams(("parallel", "arbitrary")),
        name="hy_spectrum",
    )(taps, g1f, f2)


def _long_conv(u, kf, g1c, g2, f2, f2c):
    bsz, rows, c = u.shape
    assert bsz == 2, "the two batch elements are packed as one complex signal"
    n1 = 2 * rows // GROUP_PITCH
    cn, ck, np1, np2, pitch = _dft_sizes(n1)
    once = dict(pipeline_mode=pl.Buffered(1))
    return pl.pallas_call(
        functools.partial(_conv_kernel, n1=n1, cn=cn, ck=ck, pitch=pitch, np1=np1, np2=np2),
        grid=(c // LANES, np1 + np2 + np1),
        in_specs=[
            pl.BlockSpec((2, rows, LANES), lambda j, s: (0, 0, j), **once),
            pl.BlockSpec((cn, 2 * n1, n1), lambda j, s: (jnp.minimum(s, np1 - 1), 0, 0)),
            pl.BlockSpec((1, 2 * DFT_N2, ck * LANES),
                         lambda j, s: (j, 0, jnp.clip(s - np1, 0, np2 - 1))),
            pl.BlockSpec((cn, n1, 2 * n1),
                         lambda j, s: (jnp.clip(s - np1 - np2, 0, np1 - 1), 0, 0)),
            pl.BlockSpec((2 * DFT_N2, 2 * DFT_N2), lambda j, s: (0, 0)),
            pl.BlockSpec((2 * DFT_N2, 2 * DFT_N2), lambda j, s: (0, 0)),
        ],
        out_specs=pl.BlockSpec((2, rows, LANES), lambda j, s: (0, 0, j), **once),
        out_shape=jax.ShapeDtypeStruct((2, rows, c), F32),
        scratch_shapes=[pltpu.VMEM((2, n1 * pitch, LANES), F32)],
        compiler_params=_cparams(("parallel", "arbitrary")),
        name="hy_conv",
    )(u, g1c, kf, g2, f2, f2c)


def _mixer_kernel(ya_ref, yb_ref, u_ref, h0_ref, h0p_ref, h0n_ref, cw_ref, cb_ref, yc_ref, gs_ref,
                  gh_ref, x_ref, rows_ref, cols_ref, d_ref, gate_ref, wglu_ref, why_ref, wout_ref,
                  o_ref, *, tm, cw):
    first = pl.program_id(0) == 0
    y = (jnp.where(first, ya_ref[0], yb_ref[0]) + jnp.where(first, ya_ref[1], yb_ref[1])
         + u_ref[0].astype(F32) * d_ref[...])
    h = jax.nn.gelu(y, approximate=True).astype(BF16)
    y_conv = jnp.concatenate(
        [yc_ref[0, g * GROUP_PITCH:g * GROUP_PITCH + DFT_N2, :] for g in range(tm // DFT_N2)],
        axis=0)
    x0 = _conv3(h0_ref, h0p_ref, h0n_ref, cw_ref, cb_ref, pl.program_id(1),
                pl.num_programs(1) - 1, tm)
    hx = (x0 * y_conv).astype(BF16)
    d = o_ref.shape[-1]
    merged = []
    for c in range(d // cw):
        cols = slice(c * cw, (c + 1) * cw)
        glu_v = _dot(h, wglu_ref[:, cols])
        glu_g = _dot(h, wglu_ref[:, d + c * cw:d + (c + 1) * cw])
        branch_hy = _dot(hx, why_ref[:, cols])
        m = (jax.nn.sigmoid(gs_ref[0, :, cols].astype(F32)) * (glu_v * jax.nn.sigmoid(glu_g))
             + jax.nn.sigmoid(gh_ref[0, :, cols].astype(F32)) * branch_hy)
        merged.append(m.astype(BF16))
    merged = jnp.concatenate(merged, axis=-1)
    dh = d // 2
    rp, cp = _pos_block(rows_ref[0], cols_ref[...], tm)
    for c in range(d // cw):
        cols = slice(c * cw, (c + 1) * cw)
        pos = rp[:, cols] if (c + 1) * cw <= dh else cp[:, c * cw - dh:(c + 1) * cw - dh]
        o_ref[0, :, cols] = (x_ref[0, :, cols] + pos
                             + gate_ref[0, :, cols] * _dot(merged, wout_ref[:, cols]))


def _mixer(y_ssm, p_x, conv_w, conv_b, y_conv, x, pos_tabs, s5_d, gate, w_glu, w_hy_out, w_out,
           s5w, gate_col0, tm):
    bsz, n, d = x.shape
    assert bsz == 2, "the S5 readouts arrive as one array per batch element"
    hyw = w_hy_out.shape[0]
    rows_tab, cols_tab = pos_tabs
    nr = tm // GRID_W
    dh = d // 2
    ni = n // tm
    g0 = gate_col0 // d
    cw = min(512, dh)
    resident = dict(pipeline_mode=pl.Buffered(1))
    tok = lambda width, col: pl.BlockSpec((1, tm, width), lambda b, i: (b, i, col))
    return pl.pallas_call(
        functools.partial(_mixer_kernel, tm=tm, cw=cw),
        grid=(bsz, ni),
        in_specs=[
            pl.BlockSpec((2, tm, s5w), lambda b, i: (0, jnp.where(b == 0, i, ni - 1), 0)),
            pl.BlockSpec((2, tm, s5w), lambda b, i: (0, jnp.where(b == 0, 0, i), 0)),
            tok(s5w, 0),
            *_conv3_specs(tm, hyw, n, s5w // hyw),
            pl.BlockSpec((3, hyw), lambda b, i: (0, 0)),
            pl.BlockSpec((1, hyw), lambda b, i: (0, 0)),
            pl.BlockSpec((1, tm // DFT_N2 * GROUP_PITCH, hyw), lambda b, i: (b, i, 0)),
            tok(d, g0), tok(d, g0 + 1), tok(d, 0),
            pl.BlockSpec((1, nr, dh), lambda b, i: (i, 0, 0)),
            pl.BlockSpec((GRID_W, dh), lambda b, i: (0, 0)),
            pl.BlockSpec((1, s5w), lambda b, i: (0, 0)),
            pl.BlockSpec((1, 1, d), lambda b, i: (b, 0, 0)),
            pl.BlockSpec((s5w, 2 * d), lambda b, i: (0, 0), **resident),
            pl.BlockSpec((hyw, d), lambda b, i: (0, 0), **resident),
            pl.BlockSpec((d, d), lambda b, i: (0, 0), **resident),
        ],
        out_specs=pl.BlockSpec((1, tm, d), lambda b, i: (b, i, 0)),
        out_shape=jax.ShapeDtypeStruct((bsz, n, d), F32),
        compiler_params=_cparams(("arbitrary", "arbitrary")),
        name="mixer",
    )(y_ssm[0], y_ssm[1], p_x, p_x, p_x, p_x, conv_w, conv_b, y_conv, p_x, p_x, x,
      rows_tab.reshape(-1, nr, dh), cols_tab, s5_d, gate, w_glu, w_hy_out, w_out)


def _ffn_kernel(x_ref, g_ref, sh_ref, sc_ref, gate_ref, wa_ref, wb_ref, wo_ref, nf_ref,
                o_ref, h_ref, acc_ref):
    j = pl.program_id(2)

    @pl.when(j == 0)
    def _():
        h_ref[...] = _rms_mod(x_ref[0], g_ref[...], sh_ref[0], sc_ref[0]).astype(BF16)
        acc_ref[...] = jnp.zeros_like(acc_ref)

    h = h_ref[...]
    act = _silu(_dot(h, wa_ref[...])) * _dot(h, wb_ref[...])
    acc_ref[...] += _dot(act.astype(BF16), wo_ref[...])

    @pl.when(j == pl.num_programs(2) - 1)
    def _():
        xo = x_ref[0] + gate_ref[0] * acc_ref[...]
        o_ref[0] = xo * lax.rsqrt(jnp.mean(xo * xo, axis=-1, keepdims=True) + EPS) * nf_ref[...]


def _ffn(x, g, shift, scale, gate, w_in, w_out, norm_f, tm, tf):
    bsz, n, d = x.shape
    dff = w_out.shape[0]
    nj = dff // tf
    vec = pl.BlockSpec((1, 1, d), lambda b, i, j: (b, 0, 0))
    return pl.pallas_call(
        _ffn_kernel,
        grid=(bsz, n // tm, nj),
        in_specs=[pl.BlockSpec((1, tm, d), lambda b, i, j: (b, i, 0)),
                  pl.BlockSpec((1, d), lambda b, i, j: (0, 0)),
                  vec, vec, vec,
                  pl.BlockSpec((d, tf), lambda b, i, j: (0, j)),
                  pl.BlockSpec((d, tf), lambda b, i, j: (0, nj + j)),
                  pl.BlockSpec((tf, d), lambda b, i, j: (j, 0)),
                  pl.BlockSpec((1, d), lambda b, i, j: (0, 0))],
        out_specs=pl.BlockSpec((1, tm, d), lambda b, i, j: (b, i, 0)),
        out_shape=jax.ShapeDtypeStruct((bsz, n, d), F32),
        scratch_shapes=[pltpu.VMEM((tm, d), BF16), pltpu.VMEM((tm, d), F32)],
        compiler_params=_cparams(("parallel", "parallel", "arbitrary")),
        name="ffn",
    )(x, g, shift, scale, gate, w_in, w_in, w_out, norm_f)


def _pos_tables(n_rows, d):
    quarter = d // 4
    omega = 10000.0 ** (-jnp.arange(quarter, dtype=F32) / quarter)
    ar = jnp.arange(n_rows, dtype=F32)[:, None] * omega
    ac = jnp.arange(GRID_W, dtype=F32)[:, None] * omega
    return (jnp.concatenate([jnp.sin(ar), jnp.cos(ar)], axis=-1),
            jnp.concatenate([jnp.sin(ac), jnp.cos(ac)], axis=-1))


def _filter_features(n_lat, width):
    pos = jnp.arange(n_lat, dtype=F32)
    t = pos / float(max(n_lat - 1, 1))
    w = 2.0 * math.pi * pos / n_lat
    bands = jnp.linspace(1e-4, HY_BANDS - 1, HY_BANDS, dtype=F32)
    feats = jnp.concatenate([t[:, None], jnp.cos(w[:, None] * bands), -jnp.sin(w[:, None] * bands)],
                            axis=-1)
    return jnp.pad(feats, ((0, 0), (0, width - feats.shape[1])))


def kernel(x, c, ctx, c_ctx, w_ada, b_ada, norm_mix, w_in, s5_a_re, s5_a_im, s5_log_dt,
           s5_b_re, s5_b_im, s5_c_re, s5_c_im, s5_d, w_glu, hy_conv_w, hy_conv_b,
           hy_f1_w, hy_f1_b, hy_f1_freq, hy_f2_w, hy_f2_b, hy_f2_freq, hy_f3_w, hy_decay,
           hy_bias, w_hy_out, w_out, norm_ffn, w_ffn_in, w_ffn_out, norm_f):
    bsz, n_lat, d = x.shape
    depth = w_ada.shape[0]
    assert depth == 1, "the context stream is only advanced for the single-layer trunk"
    l = 0
    s5w = s5_d.shape[-1]
    hyw = hy_bias.shape[-1]
    n_ctx = ctx.shape[1]
    tm = 512
    tn = min(1024, d)
    tb = 256
    assert n_ctx % tb == 0 and n_lat % tm == 0 and s5w == S5_CHUNKS * LANES

    pos_tabs = _pos_tables(n_lat // GRID_W, d)

    cond_t = jnp.zeros((d, SUBLANES), F32).at[:, :bsz].set(c.T).at[:, bsz].set(c_ctx)
    ada = _ada(cond_t, w_ada[l], b_ada[l][None], bsz + 1, tn=min(1024, d))

    def vec(row0, rows, part):
        v = ada[row0:row0 + rows, part * d:(part + 1) * d]
        return jnp.broadcast_to(v, (bsz, d))[:, None, :]

    shift_mix, scale_mix, gate_mix = vec(0, bsz, 0), vec(0, bsz, 1), vec(0, bsz, 2)
    shift_ffn, scale_ffn, gate_ffn = vec(0, bsz, 3), vec(0, bsz, 4), vec(0, bsz, 5)
    cshift_mix, cscale_mix = vec(bsz, 1, 0), vec(bsz, 1, 1)

    w_in_b = w_in[l].astype(BF16)
    g_mix = norm_mix[l][None]
    u_c = _ctx_proj(ctx, g_mix, cshift_mix, cscale_mix, w_in_b, s5w, tb)
    p_x = _in_proj_latent(x, pos_tabs, g_mix, shift_mix, scale_mix, w_in_b, min(1024, n_lat), tn)

    packed = [_s5_params(s5_a_re[l, k], s5_a_im[l, k], s5_log_dt[l, k], s5_b_re[l, k],
                         s5_b_im[l, k], s5_c_re[l, k], s5_c_im[l, k]) for k in range(2)]
    bd, cd, lam_r, lam_i = (jnp.stack(t) for t in zip(*packed))
    y_ssm = _s5(u_c, p_x, bd, cd, lam_r, lam_i, s5w, tb)

    fp = 64
    feats = _filter_features(n_lat, fp)
    w1 = jnp.pad(hy_f1_w[l], ((0, fp - hy_f1_w.shape[1]), (0, 0)))
    taps = _filter_taps(feats, w1, hy_f1_b[l][None], hy_f1_freq[l][None], hy_f2_w[l],
                        hy_f2_b[l][None], hy_f2_freq[l][None], hy_f3_w[l], hy_decay[l],
                        hy_bias[l][None])
    g1c, g1f, g2, f2, f2c = _dft_tables(2 * n_lat // DFT_N2)
    kf = _filter_spectrum(taps, g1f, f2)
    u_hy = _short_conv(p_x, hy_conv_w[l], hy_conv_b[l][None], s5w, hyw, tm)
    y_conv = _long_conv(u_hy, kf, g1c, g2, f2, f2c)

    x1 = _mixer(y_ssm, p_x, hy_conv_w[l], hy_conv_b[l][None], y_conv, x, pos_tabs, s5_d[l][None],
                gate_mix,
                w_glu[l].astype(BF16), w_hy_out[l].astype(BF16), w_out[l].astype(BF16),
                s5w, s5w + 3 * hyw, 256)

    return _ffn(x1, norm_ffn[l][None], shift_ffn, scale_ffn, gate_ffn,
                w_ffn_in[l].astype(BF16), w_ffn_out[l].astype(BF16), norm_f[None], tm, 512)
```

```python
import functools
import math

import jax
import jax.numpy as jnp
from jax import lax
from jax.experimental import pallas as pl
from jax.experimental.pallas import tpu as pltpu

F32 = jnp.float32
BF16 = jnp.bfloat16
HIGHEST = lax.Precision.HIGHEST

GRID_W = 64
N_ADA = 6
EPS = 1e-6
HY_BANDS = 16
LANES = 128
SUBLANES = 8
DFT_N2 = LANES
GROUP_PITCH = DFT_N2 + SUBLANES // 2
VMEM_LIMIT = 60 * 1024 * 1024


def _cparams(sem):
    return pltpu.CompilerParams(dimension_semantics=sem, vmem_limit_bytes=VMEM_LIMIT)


def _dot(a, b):
    return jnp.dot(a, b, preferred_element_type=F32)


def _dot_hi(a, b):
    return jnp.dot(a, b, preferred_element_type=F32, precision=HIGHEST)


def _silu(x):
    return x * jax.nn.sigmoid(x)


def _rms_mod(x, g, shift, scale):
    y = x * lax.rsqrt(jnp.mean(x * x, axis=-1, keepdims=True) + EPS)
    return (y * g) * (1.0 + scale) + shift


def _pos_block(rows, cols, tm):
    nr = tm // GRID_W
    dh = rows.shape[-1]
    rp = jnp.broadcast_to(rows[:, None, :], (nr, GRID_W, dh)).reshape(tm, dh)
    cp = jnp.broadcast_to(cols[None], (nr, GRID_W, dh)).reshape(tm, dh)
    return rp, cp


def _store_padded(ref, lead, val):
    pad = jnp.zeros((GROUP_PITCH - DFT_N2, val.shape[-1]), val.dtype)
    for g in range(val.shape[0] // DFT_N2):
        ref[lead, g * GROUP_PITCH:g * GROUP_PITCH + DFT_N2, :] = val[g * DFT_N2:(g + 1) * DFT_N2]
        ref[lead, g * GROUP_PITCH + DFT_N2:(g + 1) * GROUP_PITCH, :] = pad


def _ada_kernel(ct_ref, w_ref, b_ref, o_ref, *, n_vec):
    sc = _silu(ct_ref[...])
    w = w_ref[...]
    rows = [jnp.sum(w * sc[:, r:r + 1], axis=0, keepdims=True) for r in range(n_vec)]
    rows += [jnp.zeros_like(rows[0])] * (SUBLANES - n_vec)
    o_ref[...] = jnp.concatenate(rows, axis=0) + b_ref[...]


def _ada(cond_t, w, b, n_vec, tn=1024):
    d, n = w.shape
    return pl.pallas_call(
        functools.partial(_ada_kernel, n_vec=n_vec),
        grid=(n // tn,),
        in_specs=[pl.BlockSpec((d, SUBLANES), lambda j: (0, 0)),
                  pl.BlockSpec((d, tn), lambda j: (0, j)),
                  pl.BlockSpec((1, tn), lambda j: (0, j))],
        out_specs=pl.BlockSpec((SUBLANES, tn), lambda j: (0, j)),
        out_shape=jax.ShapeDtypeStruct((SUBLANES, n), F32),
        compiler_params=_cparams(("arbitrary",)),
        name="ada",
    )(cond_t, w, b)


def _ctx_kernel(x_ref, g_ref, sh_ref, sc_ref, w_ref, o_ref):
    h = _rms_mod(x_ref[0], g_ref[...], sh_ref[0], sc_ref[0]).astype(BF16)
    o_ref[0] = _dot(h, w_ref[...]).astype(o_ref.dtype)


def _ctx_proj(x, g, shift, scale, w, n_cols, tm):
    bsz, n, d = x.shape
    vec = pl.BlockSpec((1, 1, d), lambda b, i: (b, 0, 0))
    return pl.pallas_call(
        _ctx_kernel,
        grid=(bsz, n // tm),
        in_specs=[pl.BlockSpec((1, tm, d), lambda b, i: (b, i, 0)),
                  pl.BlockSpec((1, d), lambda b, i: (0, 0)), vec, vec,
                  pl.BlockSpec((d, n_cols), lambda b, i: (0, 0))],
        out_specs=pl.BlockSpec((1, tm, n_cols), lambda b, i: (b, i, 0)),
        out_shape=jax.ShapeDtypeStruct((bsz, n, n_cols), BF16),
        compiler_params=_cparams(("parallel", "parallel")),
        name="ctx_proj",
    )(x, g, shift, scale, w)


PRO_CHUNKS = 8


def _next_block(b, i, ni, nb):
    t = jnp.minimum(b * ni + i + 1, nb * ni - 1)
    return t // ni, t % ni


def _in_pipe_kernel(x0_ref, rows0_ref, sh0_ref, sc0_ref, xn_ref, rowsn_ref, shn_ref, scn_ref,
                    cols_ref, g_ref, w_ref, o_ref, h_ref, *, tm):
    b, i, j = pl.program_id(0), pl.program_id(1), pl.program_id(2)
    par = (b * pl.num_programs(1) + i) % 2
    cr = tm // PRO_CHUNKS
    dh = cols_ref.shape[-1]

    @pl.when(jnp.logical_and(jnp.logical_and(b == 0, i == 0), j == 0))
    def _():
        x = x0_ref[0] + jnp.concatenate(_pos_block(rows0_ref[...], cols_ref[...], tm), axis=-1)
        h_ref[0] = _rms_mod(x, g_ref[...], sh0_ref[0], sc0_ref[0]).astype(BF16)

    o_ref[0] = _dot(h_ref[par], w_ref[...]).astype(o_ref.dtype)

    c = jnp.minimum(j, PRO_CHUNKS - 1)
    rows = pl.ds(pl.multiple_of(c * cr, cr), cr)
    gpc = cr // GRID_W
    rp = jnp.concatenate([jnp.broadcast_to(rowsn_ref[pl.ds(c * gpc + r, 1), :], (GRID_W, dh))
                          for r in range(gpc)], axis=0)
    cp = jnp.concatenate([cols_ref[...]] * gpc, axis=0)
    xc = xn_ref[0, rows, :] + jnp.concatenate([rp, cp], axis=-1)
    h_ref[1 - par, rows, :] = _rms_mod(xc, g_ref[...], shn_ref[0], scn_ref[0]).astype(BF16)


def _in_proj_latent(x, pos_tabs, g, shift, scale, w, tm, tn):
    bsz, n, d = x.shape
    n_cols = w.shape[1]
    rows_tab, cols_tab = pos_tabs
    dh = d // 2
    ni = n // tm
    nr = tm // GRID_W
    assert (tm // PRO_CHUNKS) % GRID_W == 0 and n_cols // tn >= PRO_CHUNKS
    nxt = lambda b, i: _next_block(b, i, ni, bsz)
    once = dict(pipeline_mode=pl.Buffered(1))
    return pl.pallas_call(
        functools.partial(_in_pipe_kernel, tm=tm),
        grid=(bsz, ni, n_cols // tn),
        in_specs=[
            pl.BlockSpec((1, tm, d), lambda b, i, j: (0, 0, 0), **once),
            pl.BlockSpec((nr, dh), lambda b, i, j: (0, 0)),
            pl.BlockSpec((1, 1, d), lambda b, i, j: (0, 0, 0)),
            pl.BlockSpec((1, 1, d), lambda b, i, j: (0, 0, 0)),
            pl.BlockSpec((1, tm, d), lambda b, i, j: nxt(b, i) + (0,)),
            pl.BlockSpec((nr, dh), lambda b, i, j: (nxt(b, i)[1], 0)),
            pl.BlockSpec((1, 1, d), lambda b, i, j: (nxt(b, i)[0], 0, 0)),
            pl.BlockSpec((1, 1, d), lambda b, i, j: (nxt(b, i)[0], 0, 0)),
            pl.BlockSpec((GRID_W, dh), lambda b, i, j: (0, 0)),
            pl.BlockSpec((1, d), lambda b, i, j: (0, 0)),
            pl.BlockSpec((d, tn), lambda b, i, j: (0, j)),
        ],
        out_specs=pl.BlockSpec((1, tm, tn), lambda b, i, j: (b, i, j)),
        out_shape=jax.ShapeDtypeStruct((bsz, n, n_cols), BF16),
        scratch_shapes=[pltpu.VMEM((2, tm, d), BF16)],
        compiler_params=_cparams(("arbitrary", "arbitrary", "arbitrary")),
        name="in_proj",
    )(x, rows_tab, shift, scale, x, rows_tab, shift, scale, cols_tab, g, w)


S5_CHUNKS = 8


def _s5_rows(k, tb, pitch):
    pad = pitch - tb
    if (k * pitch) % SUBLANES == 0:
        return k * pitch, k * pitch + tb
    return k * pitch - pad, k * pitch + tb + pad


def _s5_fill(buf, lhs_of, bd_ref, k, tb, pitch):
    lhs = lhs_of(k)
    lo, hi = _s5_rows(k, tb, pitch)
    if lo != k * pitch:
        z = jnp.zeros((k * pitch - lo, LANES), F32)
        lhs = jnp.concatenate([z, lhs.astype(F32), z], axis=0).astype(BF16)
    bu = _dot(lhs, bd_ref[0, k])
    for m in range(buf.shape[0]):
        buf[m, lo:hi, :] = bu[:, m * LANES:(m + 1) * LANES]


def _s5_pass(d, bu_scan, st_scan, carry_ref, st_mm, bu_mm, lhs_of, y_ref, bd_ref, cd_ref,
             lam_r, lam_i, tb, pitch):
    nslab = bu_scan.shape[0]
    half = nslab // 2
    per = tb // S5_CHUNKS
    sr = [carry_ref[m] for m in range(half)]
    si = [carry_ref[half + m] for m in range(half)]

    def scan_steps(q0, q1):
        for q in range(q0, q1):
            idx = pl.ds(jnp.where(d == 0, q, tb - 1 - q), SUBLANES, stride=pitch)
            for m in range(half):
                nr = lam_r[m] * sr[m] - lam_i[m] * si[m] + bu_scan[m, idx, :]
                ni = lam_r[m] * si[m] + lam_i[m] * sr[m] + bu_scan[half + m, idx, :]
                st_scan[m, idx, :] = nr
                st_scan[half + m, idx, :] = ni
                sr[m], si[m] = nr, ni

    for k in range(S5_CHUNKS):
        lo, hi = _s5_rows(k, tb, pitch)
        sk = jnp.concatenate([st_mm[m, lo:hi, :] for m in range(nslab)], axis=-1)
        yk = _dot(sk.astype(BF16), cd_ref[0, k])
        y_ref[0, :, k * LANES:(k + 1) * LANES] = yk[k * pitch - lo:k * pitch - lo + tb]
        scan_steps(k * per, k * per + per // 2)
        _s5_fill(bu_mm, lhs_of, bd_ref, k, tb, pitch)
        scan_steps(k * per + per // 2, (k + 1) * per)
    for m in range(half):
        carry_ref[m] = sr[m]
        carry_ref[half + m] = si[m]


def _s5_kernel(uc0_ref, uca_ref, uxa_ref, ucb_ref, uxb_ref, bd_ref, cd_ref, lr_ref, li_ref,
               ya_ref, yb_ref, a_bu, a_st, b_bu, b_st, ca_ref, cb_ref, *, tb, pitch, nc, nblk):
    d = pl.program_id(0)
    i = pl.program_id(1)
    half = a_bu.shape[0] // 2
    lam_r = [lr_ref[0, m] for m in range(half)]
    lam_i = [li_ref[0, m] for m in range(half)]

    @pl.when(i == 0)
    def _():
        ca_ref[...] = jnp.zeros_like(ca_ref)
        cb_ref[...] = jnp.zeros_like(cb_ref)
        a_st[...] = jnp.zeros_like(a_st)
        b_st[...] = jnp.zeros_like(b_st)
        for k in range(S5_CHUNKS):
            _s5_fill(a_bu, lambda k: uc0_ref[0, :, k * LANES:(k + 1) * LANES], bd_ref, k, tb, pitch)

    def lhs(uc_ref, ux_ref, pos):
        def of(k):
            cols = slice(k * LANES, (k + 1) * LANES)
            return jnp.where(pos < nc, uc_ref[0, :, cols], ux_ref[0, :, cols])
        return of

    _s5_pass(d, a_bu, a_st, ca_ref, b_st, b_bu, lhs(ucb_ref, uxb_ref, i), yb_ref,
             bd_ref, cd_ref, lam_r, lam_i, tb, pitch)

    @pl.when(i < nblk)
    def _():
        _s5_pass(d, b_bu, b_st, cb_ref, a_st, a_bu, lhs(uca_ref, uxa_ref, i + 1), ya_ref,
                 bd_ref, cd_ref, lam_r, lam_i, tb, pitch)


def _s5(u_c, p_x, bd, cd, lam_r, lam_i, s5w, tb):
    bsz, lc, _ = u_c.shape
    assert bsz == 2, "the two batch elements are the two interleaved sequences"
    n = p_x.shape[1]
    nc, nb = lc // tb, n // tb
    nblk = nc + nb
    nslab = bd.shape[-1] // LANES
    pitch = tb + SUBLANES // 2

    def blk(d, pos, count):
        j = jnp.clip(pos, 0, count - 1)
        return jnp.where(d == 0, j, count - 1 - j)

    def u_spec(b, off, ctx):
        if ctx:
            return pl.BlockSpec((1, tb, s5w), lambda d, i: (b, blk(d, i + off, nc), 0))
        return pl.BlockSpec((1, tb, s5w), lambda d, i: (b, blk(d, i + off - nc, nb), 0))

    def y_spec(off):
        return pl.BlockSpec((1, tb, s5w), lambda d, i: (d, blk(d, i + off - nc, nb), 0))

    par = lambda d, i: (d, 0, 0, 0)
    buf = pltpu.VMEM((nslab, S5_CHUNKS * pitch, LANES), F32)
    state = pltpu.VMEM((nslab, SUBLANES, LANES), F32)
    return pl.pallas_call(
        functools.partial(_s5_kernel, tb=tb, pitch=pitch, nc=nc, nblk=nblk),
        grid=(2, nblk + 1),
        in_specs=[
            pl.BlockSpec((1, tb, s5w), lambda d, i: (0, blk(d, 0, nc), 0)),
            u_spec(0, 1, True), u_spec(0, 1, False),
            u_spec(1, 0, True), u_spec(1, 0, False),
            pl.BlockSpec((1, S5_CHUNKS, LANES, nslab * LANES), par, pipeline_mode=pl.Buffered(1)),
            pl.BlockSpec((1, S5_CHUNKS, nslab * LANES, LANES), par, pipeline_mode=pl.Buffered(1)),
            pl.BlockSpec((1, nslab // 2, SUBLANES, LANES), par),
            pl.BlockSpec((1, nslab // 2, SUBLANES, LANES), par),
        ],
        out_specs=[y_spec(0), y_spec(-1)],
        out_shape=[jax.ShapeDtypeStruct((2, n, s5w), F32)] * 2,
        scratch_shapes=[buf, buf, buf, buf, state, state],
        compiler_params=_cparams(("arbitrary", "arbitrary")),
        name="s5",
    )(u_c, u_c, p_x, u_c, p_x, bd, cd, lam_r, lam_i)


def _s5_params(a_re, a_im, log_dt, b_re, b_im, c_re, c_im):
    g, p, h = b_re.shape
    gl = g // S5_CHUNKS
    dt = jnp.exp(log_dt.astype(F32))[:, None]
    mag = jnp.exp(a_re.astype(F32) * dt)
    lr, li = mag * jnp.cos(a_im.astype(F32) * dt), mag * jnp.sin(a_im.astype(F32) * dt)
    den = a_re * a_re + a_im * a_im
    qr = ((lr - 1.0) * a_re + li * a_im) / den
    qi = (li * a_re - (lr - 1.0) * a_im) / den
    bbr = qr[:, :, None] * b_re - qi[:, :, None] * b_im
    bbi = qr[:, :, None] * b_im + qi[:, :, None] * b_re
    eye = jnp.eye(gl, dtype=F32)

    def pack_b(m):
        return jnp.einsum("kgph,gq->kghqp", m.reshape(S5_CHUNKS, gl, p, h), eye).reshape(
            S5_CHUNKS, gl * h, gl * p)

    def pack_c(m):
        return jnp.einsum("kghp,gq->kgpqh", m.reshape(S5_CHUNKS, gl, h, p), eye).reshape(
            S5_CHUNKS, gl * p, gl * h)

    bd = jnp.concatenate([pack_b(bbr), pack_b(bbi)], axis=-1).astype(BF16)
    cd = jnp.concatenate([pack_c(c_re.astype(F32)), -pack_c(c_im.astype(F32))], axis=1).astype(BF16)

    def slabs(v):
        return v.reshape(S5_CHUNKS, -1, LANES).transpose(1, 0, 2)

    return bd, cd, slabs(lr), slabs(li)


def _filt_kernel(f_ref, w1_ref, b1_ref, q1_ref, w2_ref, b2_ref, q2_ref, w3_ref, dec_ref,
                 bias_ref, o_ref, *, tq):
    f = f_ref[...]
    h = jnp.sin(q1_ref[...] * (_dot_hi(f, w1_ref[...]) + b1_ref[...]))
    h = jnp.sin(q2_ref[...] * (_dot_hi(h, w2_ref[...]) + b2_ref[...]))
    h = _dot_hi(h, w3_ref[...])
    c = bias_ref.shape[-1]
    t = f[:, 0:1]
    lag = pl.program_id(0) * tq + lax.broadcasted_iota(jnp.int32, (tq, 1), 0)
    fwd = h[:, :c] * jnp.exp(-t * jnp.abs(dec_ref[0:1, :]))
    bwd = h[:, c:] * jnp.exp(-t * jnp.abs(dec_ref[1:2, :]))
    fwd = jnp.where(lag == 0, fwd + bias_ref[...], fwd)
    bwd = jnp.where(lag == 0, 0.0, bwd)
    _store_padded(o_ref, 0, fwd)
    _store_padded(o_ref, 1, bwd)


def _filter_taps(feats, w1, b1, q1, w2, b2, q2, w3, decay, bias, tq=512):
    n_lat, fp = feats.shape
    hid = w2.shape[0]
    c = bias.shape[-1]
    rows = tq // DFT_N2 * GROUP_PITCH
    const = lambda r: (0, 0)
    return pl.pallas_call(
        functools.partial(_filt_kernel, tq=tq),
        grid=(n_lat // tq,),
        in_specs=[pl.BlockSpec((tq, fp), lambda r: (r, 0)),
                  pl.BlockSpec((fp, hid), const), pl.BlockSpec((1, hid), const),
                  pl.BlockSpec((1, hid), const),
                  pl.BlockSpec((hid, hid), const), pl.BlockSpec((1, hid), const),
                  pl.BlockSpec((1, hid), const),
                  pl.BlockSpec((hid, 2 * c), const),
                  pl.BlockSpec((2, c), const),
                  pl.BlockSpec((1, c), const)],
        out_specs=pl.BlockSpec((2, rows, c), lambda r: (0, r, 0)),
        out_shape=jax.ShapeDtypeStruct((2, n_lat // DFT_N2 * GROUP_PITCH, c), F32),
        compiler_params=_cparams(("parallel",)),
        name="hy_filter",
    )(feats, w1, b1, q1, w2, b2, q2, w3, decay, bias)


def _conv3(cur_ref, prev_ref, next_ref, w_ref, b_ref, i, last, tm):
    row = lax.broadcasted_iota(jnp.int32, (tm, 1), 0)
    cur = cur_ref[0].astype(F32)
    halo = prev_ref.shape[1]
    prev_row = jnp.where(i == 0, 0.0, prev_ref[0, halo - 1:halo, :].astype(F32))
    next_row = jnp.where(i == last, 0.0, next_ref[0, 0:1, :].astype(F32))
    up = jnp.where(row == 0, prev_row, pltpu.roll(cur, 1, 0))
    dn = jnp.where(row == tm - 1, next_row, pltpu.roll(cur, tm - 1, 0))
    w = w_ref[...]
    return up * w[0:1] + cur * w[1:2] + dn * w[2:3] + b_ref[...]


CONV_HALO = 16


def _conv3_specs(tm, c, n, col_block):
    hb = tm // CONV_HALO
    nhb = n // CONV_HALO
    return [pl.BlockSpec((1, tm, c), lambda b, i: (b, i, col_block)),
            pl.BlockSpec((1, CONV_HALO, c),
                         lambda b, i: (b, jnp.maximum(i * hb - 1, 0), col_block)),
            pl.BlockSpec((1, CONV_HALO, c),
                         lambda b, i: (b, jnp.minimum((i + 1) * hb, nhb - 1), col_block))]


def _sconv_kernel(x1_ref, x1p_ref, x1n_ref, v_ref, vp_ref, vn_ref, w1_ref, wv_ref, b1_ref, bv_ref,
                  u_ref, *, tm):
    i = pl.program_id(1)
    last = pl.num_programs(1) - 1
    x1 = _conv3(x1_ref, x1p_ref, x1n_ref, w1_ref, b1_ref, i, last, tm)
    v = _conv3(v_ref, vp_ref, vn_ref, wv_ref, bv_ref, i, last, tm)
    _store_padded(u_ref, 0, x1 * v)


def _short_conv(p_x, conv_w, conv_b, col0, c, tm):
    bsz, n, _ = p_x.shape
    cb0 = col0 // c
    part = lambda p, shape: pl.BlockSpec(shape, lambda b, i: (0, p))
    return pl.pallas_call(
        functools.partial(_sconv_kernel, tm=tm),
        grid=(bsz, n // tm),
        in_specs=(_conv3_specs(tm, c, n, cb0 + 1) + _conv3_specs(tm, c, n, cb0 + 2)
                  + [part(1, (3, c)), part(2, (3, c)), part(1, (1, c)), part(2, (1, c))]),
        out_specs=pl.BlockSpec((1, tm // DFT_N2 * GROUP_PITCH, c), lambda b, i: (b, i, 0)),
        out_shape=jax.ShapeDtypeStruct((bsz, n // DFT_N2 * GROUP_PITCH, c), F32),
        compiler_params=_cparams(("parallel", "parallel")),
        name="short_conv",
    )(p_x, p_x, p_x, p_x, p_x, p_x, conv_w, conv_w, conv_b, conv_b)


def _dft_tables(n1):
    n = n1 * DFT_N2
    k1 = jnp.arange(n1, dtype=jnp.int32)
    n2 = jnp.arange(DFT_N2, dtype=jnp.int32)

    def unit(ph, period):
        ang = (ph % period).astype(F32) * F32(2.0 * math.pi / period)
        return jnp.cos(ang), -jnp.sin(ang)

    ar, ai = unit(k1[:, None] * k1[None, :], n1)
    tr, ti = unit(n2[:, None] * k1[None, :], n)
    gr = ar[None] * tr[:, :, None] - ai[None] * ti[:, :, None]
    gi = ar[None] * ti[:, :, None] + ai[None] * tr[:, :, None]
    h = n1 // 2
    g1c = jnp.concatenate([jnp.concatenate([gr[:, :, :h], -gi[:, :, :h]], axis=2),
                           jnp.concatenate([gi[:, :, :h], gr[:, :, :h]], axis=2)], axis=1)
    m = jnp.arange(h, dtype=jnp.int32)

    def fold(g):
        up = jnp.take(g, n1 - 1 - m, axis=2)
        up0 = jnp.take(g[0], (n1 - m) % n1, axis=1) * (m != 0).astype(F32)
        return jnp.concatenate([g[:, :, :h], up.at[0].set(up0)], axis=2)

    g1f = jnp.concatenate([fold(gr), fold(gi)], axis=1)
    fr, fi = unit(n2[:, None] * n2[None, :], DFT_N2)
    f2 = jnp.concatenate([jnp.concatenate([fr, -fi], axis=1),
                          jnp.concatenate([fi, fr], axis=1)], axis=0)
    f2c = jnp.concatenate([jnp.concatenate([fr, fi], axis=1),
                           jnp.concatenate([-fi, fr], axis=1)], axis=0) / n
    return g1c.astype(BF16), g1f.astype(BF16), f2.astype(BF16), f2c.astype(BF16)


def _stage1(gather, g_ref, a_ref, chunk, cn, n1, pitch):
    for q in range(cn):
        n2 = chunk * cn + q
        a = _dot(g_ref[n2], gather(n2).astype(BF16))
        a_ref[0, pl.ds(n2, n1, stride=pitch), :] = a[:n1]
        a_ref[1, pl.ds(n2, n1, stride=pitch), :] = a[n1:]


def _a_rows(chunk, ck, q, pitch):
    return pl.ds(pl.multiple_of(chunk * (ck * pitch), SUBLANES) + q * pitch, DFT_N2)


def _load_a_chunk(a_ref, chunk, ck, pitch):
    tiles = []
    for q in range(ck):
        rows = _a_rows(chunk, ck, q, pitch)
        tiles.append(jnp.concatenate([a_ref[0, rows, :], a_ref[1, rows, :]], axis=0).astype(BF16))
    return jnp.concatenate(tiles, axis=1)


def _spec_kernel(t_ref, g_ref, f2_ref, o_ref, a_ref, *, n1, cn, ck, pitch, np1):
    s = pl.program_id(1)
    h = n1 // 2

    @pl.when(s < np1)
    def _():
        def gather(n2):
            back = (DFT_N2 - n2) % DFT_N2
            return jnp.concatenate([t_ref[0, pl.ds(n2, h, stride=GROUP_PITCH), :],
                                    t_ref[1, pl.ds(back, h, stride=GROUP_PITCH), :]], axis=0)
        _stage1(gather, g_ref, a_ref, s, cn, n1, pitch)

    @pl.when(s >= np1)
    def _():
        o_ref[0] = _dot(f2_ref[...], _load_a_chunk(a_ref, s - np1, ck, pitch)).astype(o_ref.dtype)


def _conv_kernel(u_ref, g1_ref, kf_ref, f2_ref, f2c_ref, y_ref, a_ref,
                 *, n1, cn, ck, pitch, np1, np2):
    s = pl.program_id(1)
    h = n1 // 2

    @pl.when(s == 0)
    def _():
        y_ref[...] = jnp.zeros_like(y_ref)

    @pl.when(s < np1)
    def _():
        def gather(n2):
            idx = pl.ds(n2, h, stride=GROUP_PITCH)
            return jnp.concatenate([u_ref[0, idx, :], u_ref[1, idx, :]], axis=0)
        _stage1(gather, g1_ref, a_ref, s, cn, n1, pitch)

    @pl.when(jnp.logical_and(s >= np1, s < np1 + np2))
    def _():
        chunk = s - np1
        x = _dot(f2_ref[...], _load_a_chunk(a_ref, chunk, ck, pitch))
        xr, xi = x[:DFT_N2], x[DFT_N2:]
        kr, ki = kf_ref[0, :DFT_N2, :].astype(F32), kf_ref[0, DFT_N2:, :].astype(F32)
        y = jnp.concatenate([xr * kr - xi * ki, xr * ki + xi * kr], axis=0)
        b = _dot(f2c_ref[...], y.astype(BF16))
        for q in range(ck):
            rows = _a_rows(chunk, ck, q, pitch)
            a_ref[0, rows, :] = b[:DFT_N2, q * LANES:(q + 1) * LANES]
            a_ref[1, rows, :] = b[DFT_N2:, q * LANES:(q + 1) * LANES]

    @pl.when(s >= np1 + np2)
    def _():
        for q in range(cn):
            n2 = (s - np1 - np2) * cn + q
            idx = pl.ds(n2, n1, stride=pitch)
            b = jnp.concatenate([a_ref[0, idx, :], a_ref[1, idx, :]], axis=0)
            y = lax.dot_general(g1_ref[n2], b.astype(BF16), (((0,), (0,)), ((), ())),
                                preferred_element_type=F32)
            out = pl.ds(n2, h, stride=GROUP_PITCH)
            y_ref[0, out, :] = y[:h]
            y_ref[1, out, :] = y[h:]


def _dft_sizes(n1):
    cn = 32
    ck = min(16, n1)
    return cn, ck, DFT_N2 // cn, n1 // ck, DFT_N2 + SUBLANES


def _filter_spectrum(taps, g1f, f2):
    _, rows, c = taps.shape
    n1 = 2 * rows // GROUP_PITCH
    cn, ck, np1, np2, pitch = _dft_sizes(n1)
    return pl.pallas_call(
        functools.partial(_spec_kernel, n1=n1, cn=cn, ck=ck, pitch=pitch, np1=np1),
        grid=(c // LANES, np1 + np2),
        in_specs=[pl.BlockSpec((2, rows, LANES), lambda j, s: (0, 0, j),
                               pipeline_mode=pl.Buffered(1)),
                  pl.BlockSpec((DFT_N2, 2 * n1, n1), lambda j, s: (0, 0, 0),
                               pipeline_mode=pl.Buffered(1)),
                  pl.BlockSpec((2 * DFT_N2, 2 * DFT_N2), lambda j, s: (0, 0))],
        out_specs=pl.BlockSpec((1, 2 * DFT_N2, ck * LANES),
                               lambda j, s: (j, 0, jnp.maximum(s - np1, 0))),
        out_shape=jax.ShapeDtypeStruct((c // LANES, 2 * DFT_N2, n1 * LANES), BF16),
        scratch_shapes=[pltpu.VMEM((2, n1 * pitch, LANES), F32)],
        compiler_params=_cparams(("parallel", "arbitrary")),
        name="hy_spectrum",
    )(taps, g1f, f2)


def _long_conv(u, kf, g1c, f2, f2c):
    bsz, rows, c = u.shape
    assert bsz == 2, "the two batch elements are packed as one complex signal"
    n1 = 2 * rows // GROUP_PITCH
    cn, ck, np1, np2, pitch = _dft_sizes(n1)
    once = dict(pipeline_mode=pl.Buffered(1))
    return pl.pallas_call(
        functools.partial(_conv_kernel, n1=n1, cn=cn, ck=ck, pitch=pitch, np1=np1, np2=np2),
        grid=(c // LANES, np1 + np2 + np1),
        in_specs=[
            pl.BlockSpec((2, rows, LANES), lambda j, s: (0, 0, j), **once),
            pl.BlockSpec((DFT_N2, 2 * n1, n1), lambda j, s: (0, 0, 0), **once),
            pl.BlockSpec((1, 2 * DFT_N2, ck * LANES),
                         lambda j, s: (j, 0, jnp.clip(s - np1, 0, np2 - 1))),
            pl.BlockSpec((2 * DFT_N2, 2 * DFT_N2), lambda j, s: (0, 0)),
            pl.BlockSpec((2 * DFT_N2, 2 * DFT_N2), lambda j, s: (0, 0)),
        ],
        out_specs=pl.BlockSpec((2, rows, LANES), lambda j, s: (0, 0, j), **once),
        out_shape=jax.ShapeDtypeStruct((2, rows, c), F32),
        scratch_shapes=[pltpu.VMEM((2, n1 * pitch, LANES), F32)],
        compiler_params=_cparams(("parallel", "arbitrary")),
        name="hy_conv",
    )(u, g1c, kf, f2, f2c)


def _mixer_kernel(ya_ref, yb_ref, u_ref, h0_ref, h0p_ref, h0n_ref, cw_ref, cb_ref, yc_ref, gs_ref,
                  gh_ref, x_ref, rows_ref, cols_ref, d_ref, gate_ref, wglu_ref, why_ref, wout_ref,
                  o_ref, *, tm, cw):
    first = pl.program_id(0) == 0
    y = (jnp.where(first, ya_ref[0], yb_ref[0]) + jnp.where(first, ya_ref[1], yb_ref[1])
         + u_ref[0].astype(F32) * d_ref[...])
    h = jax.nn.gelu(y, approximate=True).astype(BF16)
    y_conv = jnp.concatenate(
        [yc_ref[0, g * GROUP_PITCH:g * GROUP_PITCH + DFT_N2, :] for g in range(tm // DFT_N2)],
        axis=0)
    x0 = _conv3(h0_ref, h0p_ref, h0n_ref, cw_ref, cb_ref, pl.program_id(1),
                pl.num_programs(1) - 1, tm)
    hx = (x0 * y_conv).astype(BF16)
    d = o_ref.shape[-1]
    merged = []
    for c in range(d // cw):
        cols = slice(c * cw, (c + 1) * cw)
        glu_v = _dot(h, wglu_ref[:, cols])
        glu_g = _dot(h, wglu_ref[:, d + c * cw:d + (c + 1) * cw])
        branch_hy = _dot(hx, why_ref[:, cols])
        m = (jax.nn.sigmoid(gs_ref[0, :, cols].astype(F32)) * (glu_v * jax.nn.sigmoid(glu_g))
             + jax.nn.sigmoid(gh_ref[0, :, cols].astype(F32)) * branch_hy)
        merged.append(m.astype(BF16))
    merged = jnp.concatenate(merged, axis=-1)
    dh = d // 2
    rp, cp = _pos_block(rows_ref[0], cols_ref[...], tm)
    for c in range(d // cw):
        cols = slice(c * cw, (c + 1) * cw)
        pos = rp[:, cols] if (c + 1) * cw <= dh else cp[:, c * cw - dh:(c + 1) * cw - dh]
        o_ref[0, :, cols] = (x_ref[0, :, cols] + pos
                             + gate_ref[0, :, cols] * _dot(merged, wout_ref[:, cols]))


def _mixer(y_ssm, p_x, conv_w, conv_b, y_conv, x, pos_tabs, s5_d, gate, w_glu, w_hy_out, w_out,
           s5w, gate_col0, tm):
    bsz, n, d = x.shape
    assert bsz == 2, "the S5 readouts arrive as one array per batch element"
    hyw = w_hy_out.shape[0]
    rows_tab, cols_tab = pos_tabs
    nr = tm // GRID_W
    dh = d // 2
    ni = n // tm
    g0 = gate_col0 // d
    cw = min(512, dh)
    resident = dict(pipeline_mode=pl.Buffered(1))
    tok = lambda width, col: pl.BlockSpec((1, tm, width), lambda b, i: (b, i, col))
    return pl.pallas_call(
        functools.partial(_mixer_kernel, tm=tm, cw=cw),
        grid=(bsz, ni),
        in_specs=[
            pl.BlockSpec((2, tm, s5w), lambda b, i: (0, jnp.where(b == 0, i, ni - 1), 0)),
            pl.BlockSpec((2, tm, s5w), lambda b, i: (0, jnp.where(b == 0, 0, i), 0)),
            tok(s5w, 0),
            *_conv3_specs(tm, hyw, n, s5w // hyw),
            pl.BlockSpec((3, hyw), lambda b, i: (0, 0)),
            pl.BlockSpec((1, hyw), lambda b, i: (0, 0)),
            pl.BlockSpec((1, tm // DFT_N2 * GROUP_PITCH, hyw), lambda b, i: (b, i, 0)),
            tok(d, g0), tok(d, g0 + 1), tok(d, 0),
            pl.BlockSpec((1, nr, dh), lambda b, i: (i, 0, 0)),
            pl.BlockSpec((GRID_W, dh), lambda b, i: (0, 0)),
            pl.BlockSpec((1, s5w), lambda b, i: (0, 0)),
            pl.BlockSpec((1, 1, d), lambda b, i: (b, 0, 0)),
            pl.BlockSpec((s5w, 2 * d), lambda b, i: (0, 0), **resident),
            pl.BlockSpec((hyw, d), lambda b, i: (0, 0), **resident),
            pl.BlockSpec((d, d), lambda b, i: (0, 0), **resident),
        ],
        out_specs=pl.BlockSpec((1, tm, d), lambda b, i: (b, i, 0)),
        out_shape=jax.ShapeDtypeStruct((bsz, n, d), F32),
        compiler_params=_cparams(("arbitrary", "arbitrary")),
        name="mixer",
    )(y_ssm[0], y_ssm[1], p_x, p_x, p_x, p_x, conv_w, conv_b, y_conv, p_x, p_x, x,
      rows_tab.reshape(-1, nr, dh), cols_tab, s5_d, gate, w_glu, w_hy_out, w_out)


def _ffn_kernel(x_ref, g_ref, sh_ref, sc_ref, gate_ref, wa_ref, wb_ref, wo_ref, nf_ref,
                o_ref, h_ref, acc_ref):
    j = pl.program_id(2)

    @pl.when(j == 0)
    def _():
        h_ref[...] = _rms_mod(x_ref[0], g_ref[...], sh_ref[0], sc_ref[0]).astype(BF16)
        acc_ref[...] = jnp.zeros_like(acc_ref)

    h = h_ref[...]
    act = _silu(_dot(h, wa_ref[...])) * _dot(h, wb_ref[...])
    acc_ref[...] += _dot(act.astype(BF16), wo_ref[...])

    @pl.when(j == pl.num_programs(2) - 1)
    def _():
        xo = x_ref[0] + gate_ref[0] * acc_ref[...]
        o_ref[0] = xo * lax.rsqrt(jnp.mean(xo * xo, axis=-1, keepdims=True) + EPS) * nf_ref[...]


def _ffn(x, g, shift, scale, gate, w_in, w_out, norm_f, tm, tf):
    bsz, n, d = x.shape
    dff = w_out.shape[0]
    nj = dff // tf
    vec = pl.BlockSpec((1, 1, d), lambda b, i, j: (b, 0, 0))
    return pl.pallas_call(
        _ffn_kernel,
        grid=(bsz, n // tm, nj),
        in_specs=[pl.BlockSpec((1, tm, d), lambda b, i, j: (b, i, 0)),
                  pl.BlockSpec((1, d), lambda b, i, j: (0, 0)),
                  vec, vec, vec,
                  pl.BlockSpec((d, tf), lambda b, i, j: (0, j)),
                  pl.BlockSpec((d, tf), lambda b, i, j: (0, nj + j)),
                  pl.BlockSpec((tf, d), lambda b, i, j: (j, 0)),
                  pl.BlockSpec((1, d), lambda b, i, j: (0, 0))],
        out_specs=pl.BlockSpec((1, tm, d), lambda b, i, j: (b, i, 0)),
        out_shape=jax.ShapeDtypeStruct((bsz, n, d), F32),
        scratch_shapes=[pltpu.VMEM((tm, d), BF16), pltpu.VMEM((tm, d), F32)],
        compiler_params=_cparams(("parallel", "parallel", "arbitrary")),
        name="ffn",
    )(x, g, shift, scale, gate, w_in, w_in, w_out, norm_f)


def _pos_tables(n_rows, d):
    quarter = d // 4
    omega = 10000.0 ** (-jnp.arange(quarter, dtype=F32) / quarter)
    ar = jnp.arange(n_rows, dtype=F32)[:, None] * omega
    ac = jnp.arange(GRID_W, dtype=F32)[:, None] * omega
    return (jnp.concatenate([jnp.sin(ar), jnp.cos(ar)], axis=-1),
            jnp.concatenate([jnp.sin(ac), jnp.cos(ac)], axis=-1))


def _filter_features(n_lat, width):
    pos = jnp.arange(n_lat, dtype=F32)
    t = pos / float(max(n_lat - 1, 1))
    w = 2.0 * math.pi * pos / n_lat
    bands = jnp.linspace(1e-4, HY_BANDS - 1, HY_BANDS, dtype=F32)
    feats = jnp.concatenate([t[:, None], jnp.cos(w[:, None] * bands), -jnp.sin(w[:, None] * bands)],
                            axis=-1)
    return jnp.pad(feats, ((0, 0), (0, width - feats.shape[1])))


def kernel(x, c, ctx, c_ctx, w_ada, b_ada, norm_mix, w_in, s5_a_re, s5_a_im, s5_log_dt,
           s5_b_re, s5_b_im, s5_c_re, s5_c_im, s5_d, w_glu, hy_conv_w, hy_conv_b,
           hy_f1_w, hy_f1_b, hy_f1_freq, hy_f2_w, hy_f2_b, hy_f2_freq, hy_f3_w, hy_decay,
           hy_bias, w_hy_out, w_out, norm_ffn, w_ffn_in, w_ffn_out, norm_f):
    bsz, n_lat, d = x.shape
    depth = w_ada.shape[0]
    assert depth == 1, "the context stream is only advanced for the single-layer trunk"
    l = 0
    s5w = s5_d.shape[-1]
    hyw = hy_bias.shape[-1]
    n_ctx = ctx.shape[1]
    tm = 512
    tn = min(1024, d)
    tb = 256
    assert n_ctx % tb == 0 and n_lat % tm == 0 and s5w == S5_CHUNKS * LANES

    pos_tabs = _pos_tables(n_lat // GRID_W, d)

    cond_t = jnp.zeros((d, SUBLANES), F32).at[:, :bsz].set(c.T).at[:, bsz].set(c_ctx)
    ada = _ada(cond_t, w_ada[l], b_ada[l][None], bsz + 1, tn=min(1024, d))

    def vec(row0, rows, part):
        v = ada[row0:row0 + rows, part * d:(part + 1) * d]
        return jnp.broadcast_to(v, (bsz, d))[:, None, :]

    shift_mix, scale_mix, gate_mix = vec(0, bsz, 0), vec(0, bsz, 1), vec(0, bsz, 2)
    shift_ffn, scale_ffn, gate_ffn = vec(0, bsz, 3), vec(0, bsz, 4), vec(0, bsz, 5)
    cshift_mix, cscale_mix = vec(bsz, 1, 0), vec(bsz, 1, 1)

    w_in_b = w_in[l].astype(BF16)
    g_mix = norm_mix[l][None]
    u_c = _ctx_proj(ctx, g_mix, cshift_mix, cscale_mix, w_in_b, s5w, tb)
    p_x = _in_proj_latent(x, pos_tabs, g_mix, shift_mix, scale_mix, w_in_b, min(1024, n_lat), tn)

    packed = [_s5_params(s5_a_re[l, k], s5_a_im[l, k], s5_log_dt[l, k], s5_b_re[l, k],
                         s5_b_im[l, k], s5_c_re[l, k], s5_c_im[l, k]) for k in range(2)]
    bd, cd, lam_r, lam_i = (jnp.stack(t) for t in zip(*packed))
    y_ssm = _s5(u_c, p_x, bd, cd, lam_r, lam_i, s5w, tb)

    fp = 64
    feats = _filter_features(n_lat, fp)
    w1 = jnp.pad(hy_f1_w[l], ((0, fp - hy_f1_w.shape[1]), (0, 0)))
    taps = _filter_taps(feats, w1, hy_f1_b[l][None], hy_f1_freq[l][None], hy_f2_w[l],
                        hy_f2_b[l][None], hy_f2_freq[l][None], hy_f3_w[l], hy_decay[l],
                        hy_bias[l][None])
    g1c, g1f, f2, f2c = _dft_tables(2 * n_lat // DFT_N2)
    kf = _filter_spectrum(taps, g1f, f2)
    u_hy = _short_conv(p_x, hy_conv_w[l], hy_conv_b[l][None], s5w, hyw, tm)
    y_conv = _long_conv(u_hy, kf, g1c, f2, f2c)

    x1 = _mixer(y_ssm, p_x, hy_conv_w[l], hy_conv_b[l][None], y_conv, x, pos_tabs, s5_d[l][None],
                gate_mix,
                w_glu[l].astype(BF16), w_hy_out[l].astype(BF16), w_out[l].astype(BF16),
                s5w, s5w + 3 * hyw, 256)

    return _ffn(x1, norm_ffn[l][None], shift_ffn, scale_ffn, gate_ffn,
                w_ffn_in[l].astype(BF16), w_ffn_out[l].astype(BF16), norm_f[None], tm, 512)
```

```python
import functools
import math

import jax
import jax.numpy as jnp
from jax import lax
from jax.experimental import pallas as pl
from jax.experimental.pallas import tpu as pltpu

F32 = jnp.float32
BF16 = jnp.bfloat16
HIGHEST = lax.Precision.HIGHEST

GRID_W = 64
N_ADA = 6
EPS = 1e-6
HY_BANDS = 16
LANES = 128
SUBLANES = 8
DFT_N2 = LANES
GROUP_PITCH = DFT_N2 + SUBLANES // 2
VMEM_LIMIT = 60 * 1024 * 1024


def _cparams(sem):
    return pltpu.CompilerParams(dimension_semantics=sem, vmem_limit_bytes=VMEM_LIMIT)


def _dot(a, b):
    return jnp.dot(a, b, preferred_element_type=F32)


def _dot_hi(a, b):
    return jnp.dot(a, b, preferred_element_type=F32, precision=HIGHEST)


def _silu(x):
    return x * jax.nn.sigmoid(x)


def _rms_mod(x, g, shift, scale):
    y = x * lax.rsqrt(jnp.mean(x * x, axis=-1, keepdims=True) + EPS)
    return (y * g) * (1.0 + scale) + shift


def _pos_block(rows, cols, tm):
    nr = tm // GRID_W
    dh = rows.shape[-1]
    rp = jnp.broadcast_to(rows[:, None, :], (nr, GRID_W, dh)).reshape(tm, dh)
    cp = jnp.broadcast_to(cols[None], (nr, GRID_W, dh)).reshape(tm, dh)
    return rp, cp


def _store_padded(ref, lead, val):
    pad = jnp.zeros((GROUP_PITCH - DFT_N2, val.shape[-1]), val.dtype)
    for g in range(val.shape[0] // DFT_N2):
        ref[lead, g * GROUP_PITCH:g * GROUP_PITCH + DFT_N2, :] = val[g * DFT_N2:(g + 1) * DFT_N2]
        ref[lead, g * GROUP_PITCH + DFT_N2:(g + 1) * GROUP_PITCH, :] = pad


def _ada_kernel(ct_ref, w_ref, b_ref, o_ref, *, n_vec):
    sc = _silu(ct_ref[...])
    w = w_ref[...]
    rows = [jnp.sum(w * sc[:, r:r + 1], axis=0, keepdims=True) for r in range(n_vec)]
    rows += [jnp.zeros_like(rows[0])] * (SUBLANES - n_vec)
    o_ref[...] = jnp.concatenate(rows, axis=0) + b_ref[...]


def _ada(cond_t, w, b, n_vec, tn=1024):
    d, n = w.shape
    return pl.pallas_call(
        functools.partial(_ada_kernel, n_vec=n_vec),
        grid=(n // tn,),
        in_specs=[pl.BlockSpec((d, SUBLANES), lambda j: (0, 0)),
                  pl.BlockSpec((d, tn), lambda j: (0, j)),
                  pl.BlockSpec((1, tn), lambda j: (0, j))],
        out_specs=pl.BlockSpec((SUBLANES, tn), lambda j: (0, j)),
        out_shape=jax.ShapeDtypeStruct((SUBLANES, n), F32),
        compiler_params=_cparams(("arbitrary",)),
        name="ada",
    )(cond_t, w, b)


def _ctx_kernel(x_ref, g_ref, sh_ref, sc_ref, w_ref, o_ref):
    h = _rms_mod(x_ref[0], g_ref[...], sh_ref[0], sc_ref[0]).astype(BF16)
    o_ref[0] = _dot(h, w_ref[...]).astype(o_ref.dtype)


def _ctx_proj(x, g, shift, scale, w, n_cols, tm):
    bsz, n, d = x.shape
    vec = pl.BlockSpec((1, 1, d), lambda b, i: (b, 0, 0))
    return pl.pallas_call(
        _ctx_kernel,
        grid=(bsz, n // tm),
        in_specs=[pl.BlockSpec((1, tm, d), lambda b, i: (b, i, 0)),
                  pl.BlockSpec((1, d), lambda b, i: (0, 0)), vec, vec,
                  pl.BlockSpec((d, n_cols), lambda b, i: (0, 0))],
        out_specs=pl.BlockSpec((1, tm, n_cols), lambda b, i: (b, i, 0)),
        out_shape=jax.ShapeDtypeStruct((bsz, n, n_cols), BF16),
        compiler_params=_cparams(("parallel", "parallel")),
        name="ctx_proj",
    )(x, g, shift, scale, w)


PRO_CHUNKS = 8


def _next_block(b, i, ni, nb):
    t = jnp.minimum(b * ni + i + 1, nb * ni - 1)
    return t // ni, t % ni


def _in_pipe_kernel(x0_ref, rows0_ref, sh0_ref, sc0_ref, xn_ref, rowsn_ref, shn_ref, scn_ref,
                    cols_ref, g_ref, w_ref, o_ref, h_ref, *, tm):
    b, i, j = pl.program_id(0), pl.program_id(1), pl.program_id(2)
    par = (b * pl.num_programs(1) + i) % 2
    cr = tm // PRO_CHUNKS
    dh = cols_ref.shape[-1]

    @pl.when(jnp.logical_and(jnp.logical_and(b == 0, i == 0), j == 0))
    def _():
        x = x0_ref[0] + jnp.concatenate(_pos_block(rows0_ref[...], cols_ref[...], tm), axis=-1)
        h_ref[0] = _rms_mod(x, g_ref[...], sh0_ref[0], sc0_ref[0]).astype(BF16)

    o_ref[0] = _dot(h_ref[par], w_ref[...]).astype(o_ref.dtype)

    c = jnp.minimum(j, PRO_CHUNKS - 1)
    rows = pl.ds(pl.multiple_of(c * cr, cr), cr)
    gpc = cr // GRID_W
    rp = jnp.concatenate([jnp.broadcast_to(rowsn_ref[pl.ds(c * gpc + r, 1), :], (GRID_W, dh))
                          for r in range(gpc)], axis=0)
    cp = jnp.concatenate([cols_ref[...]] * gpc, axis=0)
    xc = xn_ref[0, rows, :] + jnp.concatenate([rp, cp], axis=-1)
    h_ref[1 - par, rows, :] = _rms_mod(xc, g_ref[...], shn_ref[0], scn_ref[0]).astype(BF16)


def _in_proj_latent(x, pos_tabs, g, shift, scale, w, tm, tn):
    bsz, n, d = x.shape
    n_cols = w.shape[1]
    rows_tab, cols_tab = pos_tabs
    dh = d // 2
    ni = n // tm
    nr = tm // GRID_W
    assert (tm // PRO_CHUNKS) % GRID_W == 0 and n_cols // tn >= PRO_CHUNKS
    nxt = lambda b, i: _next_block(b, i, ni, bsz)
    once = dict(pipeline_mode=pl.Buffered(1))
    return pl.pallas_call(
        functools.partial(_in_pipe_kernel, tm=tm),
        grid=(bsz, ni, n_cols // tn),
        in_specs=[
            pl.BlockSpec((1, tm, d), lambda b, i, j: (0, 0, 0), **once),
            pl.BlockSpec((nr, dh), lambda b, i, j: (0, 0)),
            pl.BlockSpec((1, 1, d), lambda b, i, j: (0, 0, 0)),
            pl.BlockSpec((1, 1, d), lambda b, i, j: (0, 0, 0)),
            pl.BlockSpec((1, tm, d), lambda b, i, j: nxt(b, i) + (0,)),
            pl.BlockSpec((nr, dh), lambda b, i, j: (nxt(b, i)[1], 0)),
            pl.BlockSpec((1, 1, d), lambda b, i, j: (nxt(b, i)[0], 0, 0)),
            pl.BlockSpec((1, 1, d), lambda b, i, j: (nxt(b, i)[0], 0, 0)),
            pl.BlockSpec((GRID_W, dh), lambda b, i, j: (0, 0)),
            pl.BlockSpec((1, d), lambda b, i, j: (0, 0)),
            pl.BlockSpec((d, tn), lambda b, i, j: (0, j)),
        ],
        out_specs=pl.BlockSpec((1, tm, tn), lambda b, i, j: (b, i, j)),
        out_shape=jax.ShapeDtypeStruct((bsz, n, n_cols), BF16),
        scratch_shapes=[pltpu.VMEM((2, tm, d), BF16)],
        compiler_params=_cparams(("arbitrary", "arbitrary", "arbitrary")),
        name="in_proj",
    )(x, rows_tab, shift, scale, x, rows_tab, shift, scale, cols_tab, g, w)


S5_CHUNKS = 8


def _s5_rows(k, tb, pitch):
    pad = pitch - tb
    if (k * pitch) % SUBLANES == 0:
        return k * pitch, k * pitch + tb
    return k * pitch - pad, k * pitch + tb + pad


def _s5_fill(buf, lhs_of, bd_ref, k, tb, pitch):
    lhs = lhs_of(k)
    lo, hi = _s5_rows(k, tb, pitch)
    if lo != k * pitch:
        z = jnp.zeros((k * pitch - lo, LANES), F32)
        lhs = jnp.concatenate([z, lhs.astype(F32), z], axis=0).astype(BF16)
    bu = _dot(lhs, bd_ref[0, k])
    for m in range(buf.shape[0]):
        buf[m, lo:hi, :] = bu[:, m * LANES:(m + 1) * LANES]


def _s5_pass(d, bu_scan, st_scan, carry_ref, st_mm, bu_mm, lhs_of, y_ref, bd_ref, cd_ref,
             lam_r, lam_i, tb, pitch):
    nslab = bu_scan.shape[0]
    half = nslab // 2
    per = tb // S5_CHUNKS
    sr = [carry_ref[m] for m in range(half)]
    si = [carry_ref[half + m] for m in range(half)]

    def scan_steps(q0, q1):
        for q in range(q0, q1):
            idx = pl.ds(jnp.where(d == 0, q, tb - 1 - q), SUBLANES, stride=pitch)
            for m in range(half):
                nr = lam_r[m] * sr[m] - lam_i[m] * si[m] + bu_scan[m, idx, :]
                ni = lam_r[m] * si[m] + lam_i[m] * sr[m] + bu_scan[half + m, idx, :]
                st_scan[m, idx, :] = nr
                st_scan[half + m, idx, :] = ni
                sr[m], si[m] = nr, ni

    for k in range(S5_CHUNKS):
        lo, hi = _s5_rows(k, tb, pitch)
        sk = jnp.concatenate([st_mm[m, lo:hi, :] for m in range(nslab)], axis=-1)
        yk = _dot(sk.astype(BF16), cd_ref[0, k])
        y_ref[0, :, k * LANES:(k + 1) * LANES] = yk[k * pitch - lo:k * pitch - lo + tb]
        scan_steps(k * per, k * per + per // 2)
        _s5_fill(bu_mm, lhs_of, bd_ref, k, tb, pitch)
        scan_steps(k * per + per // 2, (k + 1) * per)
    for m in range(half):
        carry_ref[m] = sr[m]
        carry_ref[half + m] = si[m]


def _s5_kernel(uc0_ref, uca_ref, uxa_ref, ucb_ref, uxb_ref, bd_ref, cd_ref, lr_ref, li_ref,
               ya_ref, yb_ref, a_bu, a_st, b_bu, b_st, ca_ref, cb_ref, *, tb, pitch, nc, nblk):
    d = pl.program_id(0)
    i = pl.program_id(1)
    half = a_bu.shape[0] // 2
    lam_r = [lr_ref[0, m] for m in range(half)]
    lam_i = [li_ref[0, m] for m in range(half)]

    @pl.when(i == 0)
    def _():
        ca_ref[...] = jnp.zeros_like(ca_ref)
        cb_ref[...] = jnp.zeros_like(cb_ref)
        a_st[...] = jnp.zeros_like(a_st)
        b_st[...] = jnp.zeros_like(b_st)
        for k in range(S5_CHUNKS):
            _s5_fill(a_bu, lambda k: uc0_ref[0, :, k * LANES:(k + 1) * LANES], bd_ref, k, tb, pitch)

    def lhs(uc_ref, ux_ref, pos):
        def of(k):
            cols = slice(k * LANES, (k + 1) * LANES)
            return jnp.where(pos < nc, uc_ref[0, :, cols], ux_ref[0, :, cols])
        return of

    _s5_pass(d, a_bu, a_st, ca_ref, b_st, b_bu, lhs(ucb_ref, uxb_ref, i), yb_ref,
             bd_ref, cd_ref, lam_r, lam_i, tb, pitch)

    @pl.when(i < nblk)
    def _():
        _s5_pass(d, b_bu, b_st, cb_ref, a_st, a_bu, lhs(uca_ref, uxa_ref, i + 1), ya_ref,
                 bd_ref, cd_ref, lam_r, lam_i, tb, pitch)


def _s5(u_c, p_x, bd, cd, lam_r, lam_i, s5w, tb):
    bsz, lc, _ = u_c.shape
    assert bsz == 2, "the two batch elements are the two interleaved sequences"
    n = p_x.shape[1]
    nc, nb = lc // tb, n // tb
    nblk = nc + nb
    nslab = bd.shape[-1] // LANES
    pitch = tb + SUBLANES // 2

    def blk(d, pos, count):
        j = jnp.clip(pos, 0, count - 1)
        return jnp.where(d == 0, j, count - 1 - j)

    def u_spec(b, off, ctx):
        if ctx:
            return pl.BlockSpec((1, tb, s5w), lambda d, i: (b, blk(d, i + off, nc), 0))
        return pl.BlockSpec((1, tb, s5w), lambda d, i: (b, blk(d, i + off - nc, nb), 0))

    def y_spec(off):
        return pl.BlockSpec((1, tb, s5w), lambda d, i: (d, blk(d, i + off - nc, nb), 0))

    par = lambda d, i: (d, 0, 0, 0)
    buf = pltpu.VMEM((nslab, S5_CHUNKS * pitch, LANES), F32)
    state = pltpu.VMEM((nslab, SUBLANES, LANES), F32)
    return pl.pallas_call(
        functools.partial(_s5_kernel, tb=tb, pitch=pitch, nc=nc, nblk=nblk),
        grid=(2, nblk + 1),
        in_specs=[
            pl.BlockSpec((1, tb, s5w), lambda d, i: (0, blk(d, 0, nc), 0)),
            u_spec(0, 1, True), u_spec(0, 1, False),
            u_spec(1, 0, True), u_spec(1, 0, False),
            pl.BlockSpec((1, S5_CHUNKS, LANES, nslab * LANES), par, pipeline_mode=pl.Buffered(1)),
            pl.BlockSpec((1, S5_CHUNKS, nslab * LANES, LANES), par, pipeline_mode=pl.Buffered(1)),
            pl.BlockSpec((1, nslab // 2, SUBLANES, LANES), par),
            pl.BlockSpec((1, nslab // 2, SUBLANES, LANES), par),
        ],
        out_specs=[y_spec(0), y_spec(-1)],
        out_shape=[jax.ShapeDtypeStruct((2, n, s5w), F32)] * 2,
        scratch_shapes=[buf, buf, buf, buf, state, state],
        compiler_params=_cparams(("arbitrary", "arbitrary")),
        name="s5",
    )(u_c, u_c, p_x, u_c, p_x, bd, cd, lam_r, lam_i)


def _s5_params(a_re, a_im, log_dt, b_re, b_im, c_re, c_im):
    g, p, h = b_re.shape
    gl = g // S5_CHUNKS
    dt = jnp.exp(log_dt.astype(F32))[:, None]
    mag = jnp.exp(a_re.astype(F32) * dt)
    lr, li = mag * jnp.cos(a_im.astype(F32) * dt), mag * jnp.sin(a_im.astype(F32) * dt)
    den = a_re * a_re + a_im * a_im
    qr = ((lr - 1.0) * a_re + li * a_im) / den
    qi = (li * a_re - (lr - 1.0) * a_im) / den
    bbr = qr[:, :, None] * b_re - qi[:, :, None] * b_im
    bbi = qr[:, :, None] * b_im + qi[:, :, None] * b_re
    eye = jnp.eye(gl, dtype=F32)

    def pack_b(m):
        return jnp.einsum("kgph,gq->kghqp", m.reshape(S5_CHUNKS, gl, p, h), eye).reshape(
            S5_CHUNKS, gl * h, gl * p)

    def pack_c(m):
        return jnp.einsum("kghp,gq->kgpqh", m.reshape(S5_CHUNKS, gl, h, p), eye).reshape(
            S5_CHUNKS, gl * p, gl * h)

    bd = jnp.concatenate([pack_b(bbr), pack_b(bbi)], axis=-1).astype(BF16)
    cd = jnp.concatenate([pack_c(c_re.astype(F32)), -pack_c(c_im.astype(F32))], axis=1).astype(BF16)

    def slabs(v):
        return v.reshape(S5_CHUNKS, -1, LANES).transpose(1, 0, 2)

    return bd, cd, slabs(lr), slabs(li)


def _filt_kernel(f_ref, w1_ref, b1_ref, q1_ref, w2_ref, b2_ref, q2_ref, w3_ref, dec_ref,
                 bias_ref, o_ref, *, tq):
    f = f_ref[...]
    h = jnp.sin(q1_ref[...] * (_dot_hi(f, w1_ref[...]) + b1_ref[...]))
    h = jnp.sin(q2_ref[...] * (_dot_hi(h, w2_ref[...]) + b2_ref[...]))
    h = _dot_hi(h, w3_ref[...])
    c = bias_ref.shape[-1]
    t = f[:, 0:1]
    lag = pl.program_id(0) * tq + lax.broadcasted_iota(jnp.int32, (tq, 1), 0)
    fwd = h[:, :c] * jnp.exp(-t * jnp.abs(dec_ref[0:1, :]))
    bwd = h[:, c:] * jnp.exp(-t * jnp.abs(dec_ref[1:2, :]))
    fwd = jnp.where(lag == 0, fwd + bias_ref[...], fwd)
    bwd = jnp.where(lag == 0, 0.0, bwd)
    _store_padded(o_ref, 0, fwd)
    _store_padded(o_ref, 1, bwd)


def _filter_taps(feats, w1, b1, q1, w2, b2, q2, w3, decay, bias, tq=512):
    n_lat, fp = feats.shape
    hid = w2.shape[0]
    c = bias.shape[-1]
    rows = tq // DFT_N2 * GROUP_PITCH
    const = lambda r: (0, 0)
    return pl.pallas_call(
        functools.partial(_filt_kernel, tq=tq),
        grid=(n_lat // tq,),
        in_specs=[pl.BlockSpec((tq, fp), lambda r: (r, 0)),
                  pl.BlockSpec((fp, hid), const), pl.BlockSpec((1, hid), const),
                  pl.BlockSpec((1, hid), const),
                  pl.BlockSpec((hid, hid), const), pl.BlockSpec((1, hid), const),
                  pl.BlockSpec((1, hid), const),
                  pl.BlockSpec((hid, 2 * c), const),
                  pl.BlockSpec((2, c), const),
                  pl.BlockSpec((1, c), const)],
        out_specs=pl.BlockSpec((2, rows, c), lambda r: (0, r, 0)),
        out_shape=jax.ShapeDtypeStruct((2, n_lat // DFT_N2 * GROUP_PITCH, c), F32),
        compiler_params=_cparams(("parallel",)),
        name="hy_filter",
    )(feats, w1, b1, q1, w2, b2, q2, w3, decay, bias)


def _conv3(cur_ref, prev_ref, next_ref, w_ref, b_ref, i, last, tm):
    row = lax.broadcasted_iota(jnp.int32, (tm, 1), 0)
    cur = cur_ref[0].astype(F32)
    halo = prev_ref.shape[1]
    prev_row = jnp.where(i == 0, 0.0, prev_ref[0, halo - 1:halo, :].astype(F32))
    next_row = jnp.where(i == last, 0.0, next_ref[0, 0:1, :].astype(F32))
    up = jnp.where(row == 0, prev_row, pltpu.roll(cur, 1, 0))
    dn = jnp.where(row == tm - 1, next_row, pltpu.roll(cur, tm - 1, 0))
    w = w_ref[...]
    return up * w[0:1] + cur * w[1:2] + dn * w[2:3] + b_ref[...]


CONV_HALO = 16


def _conv3_specs(tm, c, n, col_block):
    hb = tm // CONV_HALO
    nhb = n // CONV_HALO
    return [pl.BlockSpec((1, tm, c), lambda b, i: (b, i, col_block)),
            pl.BlockSpec((1, CONV_HALO, c),
                         lambda b, i: (b, jnp.maximum(i * hb - 1, 0), col_block)),
            pl.BlockSpec((1, CONV_HALO, c),
                         lambda b, i: (b, jnp.minimum((i + 1) * hb, nhb - 1), col_block))]


def _sconv_kernel(x1_ref, x1p_ref, x1n_ref, v_ref, vp_ref, vn_ref, w1_ref, wv_ref, b1_ref, bv_ref,
                  u_ref, *, tm):
    i = pl.program_id(1)
    last = pl.num_programs(1) - 1
    x1 = _conv3(x1_ref, x1p_ref, x1n_ref, w1_ref, b1_ref, i, last, tm)
    v = _conv3(v_ref, vp_ref, vn_ref, wv_ref, bv_ref, i, last, tm)
    _store_padded(u_ref, 0, x1 * v)


def _short_conv(p_x, conv_w, conv_b, col0, c, tm):
    bsz, n, _ = p_x.shape
    cb0 = col0 // c
    part = lambda p, shape: pl.BlockSpec(shape, lambda b, i: (0, p))
    return pl.pallas_call(
        functools.partial(_sconv_kernel, tm=tm),
        grid=(bsz, n // tm),
        in_specs=(_conv3_specs(tm, c, n, cb0 + 1) + _conv3_specs(tm, c, n, cb0 + 2)
                  + [part(1, (3, c)), part(2, (3, c)), part(1, (1, c)), part(2, (1, c))]),
        out_specs=pl.BlockSpec((1, tm // DFT_N2 * GROUP_PITCH, c), lambda b, i: (b, i, 0)),
        out_shape=jax.ShapeDtypeStruct((bsz, n // DFT_N2 * GROUP_PITCH, c), F32),
        compiler_params=_cparams(("parallel", "parallel")),
        name="short_conv",
    )(p_x, p_x, p_x, p_x, p_x, p_x, conv_w, conv_w, conv_b, conv_b)


def _dft_tables(n1):
    n = n1 * DFT_N2
    k1 = jnp.arange(n1, dtype=jnp.int32)
    n2 = jnp.arange(DFT_N2, dtype=jnp.int32)

    def unit(ph, period):
        ang = (ph % period).astype(F32) * F32(2.0 * math.pi / period)
        return jnp.cos(ang), -jnp.sin(ang)

    ar, ai = unit(k1[:, None] * k1[None, :], n1)
    tr, ti = unit(n2[:, None] * k1[None, :], n)
    gr = ar[None] * tr[:, :, None] - ai[None] * ti[:, :, None]
    gi = ar[None] * ti[:, :, None] + ai[None] * tr[:, :, None]
    h = n1 // 2
    g1c = jnp.concatenate([jnp.concatenate([gr[:, :, :h], -gi[:, :, :h]], axis=2),
                           jnp.concatenate([gi[:, :, :h], gr[:, :, :h]], axis=2)], axis=1)
    m = jnp.arange(h, dtype=jnp.int32)

    def fold(g):
        up = jnp.take(g, n1 - 1 - m, axis=2)
        up0 = jnp.take(g[0], (n1 - m) % n1, axis=1) * (m != 0).astype(F32)
        return jnp.concatenate([g[:, :, :h], up.at[0].set(up0)], axis=2)

    g1f = jnp.concatenate([fold(gr), fold(gi)], axis=1)
    fr, fi = unit(n2[:, None] * n2[None, :], DFT_N2)
    f2 = jnp.concatenate([jnp.concatenate([fr, -fi], axis=1),
                          jnp.concatenate([fi, fr], axis=1)], axis=0)
    f2c = jnp.concatenate([jnp.concatenate([fr, fi], axis=1),
                           jnp.concatenate([-fi, fr], axis=1)], axis=0) / n
    return g1c.astype(BF16), g1f.astype(BF16), f2.astype(BF16), f2c.astype(BF16)


def _stage1(gather, g_of, a_ref, chunk, cn, n1, pitch):
    for q in range(cn):
        n2 = chunk * cn + q
        a = _dot(g_of(q, n2), gather(n2).astype(BF16))
        a_ref[0, pl.ds(n2, n1, stride=pitch), :] = a[:n1]
        a_ref[1, pl.ds(n2, n1, stride=pitch), :] = a[n1:]


def _a_rows(chunk, ck, q, pitch):
    return pl.ds(pl.multiple_of(chunk * (ck * pitch), SUBLANES) + q * pitch, DFT_N2)


def _load_a_chunk(a_ref, chunk, ck, pitch):
    tiles = []
    for q in range(ck):
        rows = _a_rows(chunk, ck, q, pitch)
        tiles.append(jnp.concatenate([a_ref[0, rows, :], a_ref[1, rows, :]], axis=0).astype(BF16))
    return jnp.concatenate(tiles, axis=1)


def _gather_groups(piece_refs, lead, row, groups):
    per = groups // len(piece_refs)
    return [r[lead, pl.ds(row, per, stride=GROUP_PITCH), :] for r in piece_refs]


def _spec_kernel(*refs, n1, cn, ck, pitch, np1):
    t_refs = refs[:TIME_PIECES]
    g_ref, f2_ref, o_ref, a_ref = refs[TIME_PIECES:]
    s = pl.program_id(1)
    h = n1 // 2

    @pl.when(s < np1)
    def _():
        def gather(n2):
            back = (DFT_N2 - n2) % DFT_N2
            return jnp.concatenate(_gather_groups(t_refs, 0, n2, h)
                                   + _gather_groups(t_refs, 1, back, h), axis=0)
        _stage1(gather, lambda q, n2: g_ref[n2], a_ref, s, cn, n1, pitch)

    @pl.when(s >= np1)
    def _():
        o_ref[0] = _dot(f2_ref[...], _load_a_chunk(a_ref, s - np1, ck, pitch)).astype(o_ref.dtype)


def _conv_kernel(*refs, n1, cn, ck, pitch, np1, np2):
    u_refs = refs[:TIME_PIECES]
    g1_ref, kf_ref, f2_ref, f2c_ref, y_ref, a_ref = refs[TIME_PIECES:]
    s = pl.program_id(1)
    h = n1 // 2

    @pl.when(s == 0)
    def _():
        y_ref[...] = jnp.zeros_like(y_ref)

    @pl.when(s < np1)
    def _():
        def gather(n2):
            return jnp.concatenate(_gather_groups(u_refs, 0, n2, h)
                                   + _gather_groups(u_refs, 1, n2, h), axis=0)
        _stage1(gather, lambda q, n2: g1_ref[q], a_ref, s, cn, n1, pitch)

    @pl.when(jnp.logical_and(s >= np1, s < np1 + np2))
    def _():
        chunk = s - np1
        x = _dot(f2_ref[...], _load_a_chunk(a_ref, chunk, ck, pitch))
        xr, xi = x[:DFT_N2], x[DFT_N2:]
        kr, ki = kf_ref[0, :DFT_N2, :].astype(F32), kf_ref[0, DFT_N2:, :].astype(F32)
        y = jnp.concatenate([xr * kr - xi * ki, xr * ki + xi * kr], axis=0)
        b = _dot(f2c_ref[...], y.astype(BF16))
        for q in range(ck):
            rows = _a_rows(chunk, ck, q, pitch)
            a_ref[0, rows, :] = b[:DFT_N2, q * LANES:(q + 1) * LANES]
            a_ref[1, rows, :] = b[DFT_N2:, q * LANES:(q + 1) * LANES]

    @pl.when(s >= np1 + np2)
    def _():
        for q in range(cn):
            n2 = (s - np1 - np2) * cn + q
            idx = pl.ds(n2, n1, stride=pitch)
            b = jnp.concatenate([a_ref[0, idx, :], a_ref[1, idx, :]], axis=0)
            y = lax.dot_general(g1_ref[q], b.astype(BF16), (((0,), (0,)), ((), ())),
                                preferred_element_type=F32)
            out = pl.ds(n2, h, stride=GROUP_PITCH)
            y_ref[0, out, :] = y[:h]
            y_ref[1, out, :] = y[h:]


TIME_PIECES = 4


def _piece_specs(rows, ncb, np1):
    per = rows // TIME_PIECES

    def spec(p):
        start = np1 + 2 + 3 * p
        return pl.BlockSpec((2, per, LANES), lambda j, s: (
            0, p, jnp.minimum(j + (s >= start).astype(jnp.int32), ncb - 1)))

    return [spec(p) for p in range(TIME_PIECES)]


def _dft_sizes(n1):
    cn = 32
    ck = min(8, n1)
    return cn, ck, DFT_N2 // cn, n1 // ck, DFT_N2 + SUBLANES


def _filter_spectrum(taps, g1f, f2):
    _, rows, c = taps.shape
    n1 = 2 * rows // GROUP_PITCH
    cn, ck, np1, np2, pitch = _dft_sizes(n1)
    return pl.pallas_call(
        functools.partial(_spec_kernel, n1=n1, cn=cn, ck=ck, pitch=pitch, np1=np1),
        grid=(c // LANES, np1 + np2),
        in_specs=_piece_specs(rows, c // LANES, np1) + [
                  pl.BlockSpec((DFT_N2, 2 * n1, n1), lambda j, s: (0, 0, 0),
                               pipeline_mode=pl.Buffered(1)),
                  pl.BlockSpec((2 * DFT_N2, 2 * DFT_N2), lambda j, s: (0, 0))],
        out_specs=pl.BlockSpec((1, 2 * DFT_N2, ck * LANES),
                               lambda j, s: (j, 0, jnp.maximum(s - np1, 0))),
        out_shape=jax.ShapeDtypeStruct((c // LANES, 2 * DFT_N2, n1 * LANES), BF16),
        scratch_shapes=[pltpu.VMEM((2, n1 * pitch, LANES), F32)],
        compiler_params=_cparams(("parallel", "arbitrary")),
        name="hy_spectrum",
    )(*([taps] * TIME_PIECES), g1f, f2)


def _long_conv(u, kf, g1c, f2, f2c):
    bsz, rows, c = u.shape
    assert bsz == 2, "the two batch elements are packed as one complex signal"
    n1 = 2 * rows // GROUP_PITCH
    cn, ck, np1, np2, pitch = _dft_sizes(n1)
    once = dict(pipeline_mode=pl.Buffered(1))
    return pl.pallas_call(
        functools.partial(_conv_kernel, n1=n1, cn=cn, ck=ck, pitch=pitch, np1=np1, np2=np2),
        grid=(c // LANES, np1 + np2 + np1),
        in_specs=_piece_specs(rows, c // LANES, np1) + [
            pl.BlockSpec((cn, 2 * n1, n1), lambda j, s: (
                jnp.where(s < np1, s, jnp.clip(s - np1 - np2, 0, np1 - 1)), 0, 0)),
            pl.BlockSpec((1, 2 * DFT_N2, ck * LANES),
                         lambda j, s: (j, 0, jnp.clip(s - np1, 0, np2 - 1))),
            pl.BlockSpec((2 * DFT_N2, 2 * DFT_N2), lambda j, s: (0, 0)),
            pl.BlockSpec((2 * DFT_N2, 2 * DFT_N2), lambda j, s: (0, 0)),
        ],
        out_specs=pl.BlockSpec((2, rows, LANES), lambda j, s: (0, 0, j), **once),
        out_shape=jax.ShapeDtypeStruct((2, rows, c), F32),
        scratch_shapes=[pltpu.VMEM((2, n1 * pitch, LANES), F32)],
        compiler_params=_cparams(("parallel", "arbitrary")),
        name="hy_conv",
    )(*([u] * TIME_PIECES), g1c, kf, f2, f2c)


def _mixer_kernel(ya_ref, yb_ref, u_ref, h0_ref, h0p_ref, h0n_ref, cw_ref, cb_ref, yc_ref, gs_ref,
                  gh_ref, x_ref, rows_ref, cols_ref, d_ref, gate_ref, wglu_ref, why_ref, wout_ref,
                  o_ref, *, tm, cw):
    first = pl.program_id(0) == 0
    y = (jnp.where(first, ya_ref[0], yb_ref[0]) + jnp.where(first, ya_ref[1], yb_ref[1])
         + u_ref[0].astype(F32) * d_ref[...])
    h = jax.nn.gelu(y, approximate=True).astype(BF16)
    y_conv = jnp.concatenate(
        [yc_ref[0, g * GROUP_PITCH:g * GROUP_PITCH + DFT_N2, :] for g in range(tm // DFT_N2)],
        axis=0)
    x0 = _conv3(h0_ref, h0p_ref, h0n_ref, cw_ref, cb_ref, pl.program_id(1),
                pl.num_programs(1) - 1, tm)
    hx = (x0 * y_conv).astype(BF16)
    d = o_ref.shape[-1]
    merged = []
    for c in range(d // cw):
        cols = slice(c * cw, (c + 1) * cw)
        glu_v = _dot(h, wglu_ref[:, cols])
        glu_g = _dot(h, wglu_ref[:, d + c * cw:d + (c + 1) * cw])
        branch_hy = _dot(hx, why_ref[:, cols])
        m = (jax.nn.sigmoid(gs_ref[0, :, cols].astype(F32)) * (glu_v * jax.nn.sigmoid(glu_g))
             + jax.nn.sigmoid(gh_ref[0, :, cols].astype(F32)) * branch_hy)
        merged.append(m.astype(BF16))
    merged = jnp.concatenate(merged, axis=-1)
    dh = d // 2
    rp, cp = _pos_block(rows_ref[0], cols_ref[...], tm)
    for c in range(d // cw):
        cols = slice(c * cw, (c + 1) * cw)
        pos = rp[:, cols] if (c + 1) * cw <= dh else cp[:, c * cw - dh:(c + 1) * cw - dh]
        o_ref[0, :, cols] = (x_ref[0, :, cols] + pos
                             + gate_ref[0, :, cols] * _dot(merged, wout_ref[:, cols]))


def _mixer(y_ssm, p_x, conv_w, conv_b, y_conv, x, pos_tabs, s5_d, gate, w_glu, w_hy_out, w_out,
           s5w, gate_col0, tm):
    bsz, n, d = x.shape
    assert bsz == 2, "the S5 readouts arrive as one array per batch element"
    hyw = w_hy_out.shape[0]
    rows_tab, cols_tab = pos_tabs
    nr = tm // GRID_W
    dh = d // 2
    ni = n // tm
    g0 = gate_col0 // d
    cw = min(512, dh)
    resident = dict(pipeline_mode=pl.Buffered(1))
    tok = lambda width, col: pl.BlockSpec((1, tm, width), lambda b, i: (b, i, col))
    return pl.pallas_call(
        functools.partial(_mixer_kernel, tm=tm, cw=cw),
        grid=(bsz, ni),
        in_specs=[
            pl.BlockSpec((2, tm, s5w), lambda b, i: (0, jnp.where(b == 0, i, ni - 1), 0)),
            pl.BlockSpec((2, tm, s5w), lambda b, i: (0, jnp.where(b == 0, 0, i), 0)),
            tok(s5w, 0),
            *_conv3_specs(tm, hyw, n, s5w // hyw),
            pl.BlockSpec((3, hyw), lambda b, i: (0, 0)),
            pl.BlockSpec((1, hyw), lambda b, i: (0, 0)),
            pl.BlockSpec((1, tm // DFT_N2 * GROUP_PITCH, hyw), lambda b, i: (b, i, 0)),
            tok(d, g0), tok(d, g0 + 1), tok(d, 0),
            pl.BlockSpec((1, nr, dh), lambda b, i: (i, 0, 0)),
            pl.BlockSpec((GRID_W, dh), lambda b, i: (0, 0)),
            pl.BlockSpec((1, s5w), lambda b, i: (0, 0)),
            pl.BlockSpec((1, 1, d), lambda b, i: (b, 0, 0)),
            pl.BlockSpec((s5w, 2 * d), lambda b, i: (0, 0), **resident),
            pl.BlockSpec((hyw, d), lambda b, i: (0, 0), **resident),
            pl.BlockSpec((d, d), lambda b, i: (0, 0), **resident),
        ],
        out_specs=pl.BlockSpec((1, tm, d), lambda b, i: (b, i, 0)),
        out_shape=jax.ShapeDtypeStruct((bsz, n, d), F32),
        compiler_params=_cparams(("arbitrary", "arbitrary")),
        name="mixer",
    )(y_ssm[0], y_ssm[1], p_x, p_x, p_x, p_x, conv_w, conv_b, y_conv, p_x, p_x, x,
      rows_tab.reshape(-1, nr, dh), cols_tab, s5_d, gate, w_glu, w_hy_out, w_out)


def _ffn_kernel(x_ref, g_ref, sh_ref, sc_ref, gate_ref, wa_ref, wb_ref, wo_ref, nf_ref,
                o_ref, h_ref, acc_ref):
    j = pl.program_id(2)

    @pl.when(j == 0)
    def _():
        h_ref[...] = _rms_mod(x_ref[0], g_ref[...], sh_ref[0], sc_ref[0]).astype(BF16)
        acc_ref[...] = jnp.zeros_like(acc_ref)

    h = h_ref[...]
    act = _silu(_dot(h, wa_ref[...])) * _dot(h, wb_ref[...])
    acc_ref[...] += _dot(act.astype(BF16), wo_ref[...])

    @pl.when(j == pl.num_programs(2) - 1)
    def _():
        xo = x_ref[0] + gate_ref[0] * acc_ref[...]
        o_ref[0] = xo * lax.rsqrt(jnp.mean(xo * xo, axis=-1, keepdims=True) + EPS) * nf_ref[...]


def _ffn(x, g, shift, scale, gate, w_in, w_out, norm_f, tm, tf):
    bsz, n, d = x.shape
    dff = w_out.shape[0]
    nj = dff // tf
    vec = pl.BlockSpec((1, 1, d), lambda b, i, j: (b, 0, 0))
    return pl.pallas_call(
        _ffn_kernel,
        grid=(bsz, n // tm, nj),
        in_specs=[pl.BlockSpec((1, tm, d), lambda b, i, j: (b, i, 0)),
                  pl.BlockSpec((1, d), lambda b, i, j: (0, 0)),
                  vec, vec, vec,
                  pl.BlockSpec((d, tf), lambda b, i, j: (0, j)),
                  pl.BlockSpec((d, tf), lambda b, i, j: (0, nj + j)),
                  pl.BlockSpec((tf, d), lambda b, i, j: (j, 0)),
                  pl.BlockSpec((1, d), lambda b, i, j: (0, 0))],
        out_specs=pl.BlockSpec((1, tm, d), lambda b, i, j: (b, i, 0)),
        out_shape=jax.ShapeDtypeStruct((bsz, n, d), F32),
        scratch_shapes=[pltpu.VMEM((tm, d), BF16), pltpu.VMEM((tm, d), F32)],
        compiler_params=_cparams(("parallel", "parallel", "arbitrary")),
        name="ffn",
    )(x, g, shift, scale, gate, w_in, w_in, w_out, norm_f)


def _pos_tables(n_rows, d):
    quarter = d // 4
    omega = 10000.0 ** (-jnp.arange(quarter, dtype=F32) / quarter)
    ar = jnp.arange(n_rows, dtype=F32)[:, None] * omega
    ac = jnp.arange(GRID_W, dtype=F32)[:, None] * omega
    return (jnp.concatenate([jnp.sin(ar), jnp.cos(ar)], axis=-1),
            jnp.concatenate([jnp.sin(ac), jnp.cos(ac)], axis=-1))


def _filter_features(n_lat, width):
    pos = jnp.arange(n_lat, dtype=F32)
    t = pos / float(max(n_lat - 1, 1))
    w = 2.0 * math.pi * pos / n_lat
    bands = jnp.linspace(1e-4, HY_BANDS - 1, HY_BANDS, dtype=F32)
    feats = jnp.concatenate([t[:, None], jnp.cos(w[:, None] * bands), -jnp.sin(w[:, None] * bands)],
                            axis=-1)
    return jnp.pad(feats, ((0, 0), (0, width - feats.shape[1])))


def kernel(x, c, ctx, c_ctx, w_ada, b_ada, norm_mix, w_in, s5_a_re, s5_a_im, s5_log_dt,
           s5_b_re, s5_b_im, s5_c_re, s5_c_im, s5_d, w_glu, hy_conv_w, hy_conv_b,
           hy_f1_w, hy_f1_b, hy_f1_freq, hy_f2_w, hy_f2_b, hy_f2_freq, hy_f3_w, hy_decay,
           hy_bias, w_hy_out, w_out, norm_ffn, w_ffn_in, w_ffn_out, norm_f):
    bsz, n_lat, d = x.shape
    depth = w_ada.shape[0]
    assert depth == 1, "the context stream is only advanced for the single-layer trunk"
    l = 0
    s5w = s5_d.shape[-1]
    hyw = hy_bias.shape[-1]
    n_ctx = ctx.shape[1]
    tm = 512
    tn = min(1024, d)
    tb = 256
    assert n_ctx % tb == 0 and n_lat % tm == 0 and s5w == S5_CHUNKS * LANES

    pos_tabs = _pos_tables(n_lat // GRID_W, d)

    cond_t = jnp.zeros((d, SUBLANES), F32).at[:, :bsz].set(c.T).at[:, bsz].set(c_ctx)
    ada = _ada(cond_t, w_ada[l], b_ada[l][None], bsz + 1, tn=min(1024, d))

    def vec(row0, rows, part):
        v = ada[row0:row0 + rows, part * d:(part + 1) * d]
        return jnp.broadcast_to(v, (bsz, d))[:, None, :]

    shift_mix, scale_mix, gate_mix = vec(0, bsz, 0), vec(0, bsz, 1), vec(0, bsz, 2)
    shift_ffn, scale_ffn, gate_ffn = vec(0, bsz, 3), vec(0, bsz, 4), vec(0, bsz, 5)
    cshift_mix, cscale_mix = vec(bsz, 1, 0), vec(bsz, 1, 1)

    w_in_b = w_in[l].astype(BF16)
    g_mix = norm_mix[l][None]
    u_c = _ctx_proj(ctx, g_mix, cshift_mix, cscale_mix, w_in_b, s5w, tb)
    p_x = _in_proj_latent(x, pos_tabs, g_mix, shift_mix, scale_mix, w_in_b, min(1024, n_lat), tn)

    packed = [_s5_params(s5_a_re[l, k], s5_a_im[l, k], s5_log_dt[l, k], s5_b_re[l, k],
                         s5_b_im[l, k], s5_c_re[l, k], s5_c_im[l, k]) for k in range(2)]
    bd, cd, lam_r, lam_i = (jnp.stack(t) for t in zip(*packed))
    y_ssm = _s5(u_c, p_x, bd, cd, lam_r, lam_i, s5w, tb)

    fp = 64
    feats = _filter_features(n_lat, fp)
    w1 = jnp.pad(hy_f1_w[l], ((0, fp - hy_f1_w.shape[1]), (0, 0)))
    taps = _filter_taps(feats, w1, hy_f1_b[l][None], hy_f1_freq[l][None], hy_f2_w[l],
                        hy_f2_b[l][None], hy_f2_freq[l][None], hy_f3_w[l], hy_decay[l],
                        hy_bias[l][None])
    g1c, g1f, f2, f2c = _dft_tables(2 * n_lat // DFT_N2)
    kf = _filter_spectrum(taps, g1f, f2)
    u_hy = _short_conv(p_x, hy_conv_w[l], hy_conv_b[l][None], s5w, hyw, tm)
    y_conv = _long_conv(u_hy, kf, g1c, f2, f2c)

    x1 = _mixer(y_ssm, p_x, hy_conv_w[l], hy_conv_b[l][None], y_conv, x, pos_tabs, s5_d[l][None],
                gate_mix,
                w_glu[l].astype(BF16), w_hy_out[l].astype(BF16), w_out[l].astype(BF16),
                s5w, s5w + 3 * hyw, 256)

    return _ffn(x1, norm_ffn[l][None], shift_ffn, scale_ffn, gate_ffn,
                w_ffn_in[l].astype(BF16), w_ffn_out[l].astype(BF16), norm_f[None], tm, 512)
```

```python
import functools
import math

import jax
import jax.numpy as jnp
from jax import lax
from jax.experimental import pallas as pl
from jax.experimental.pallas import tpu as pltpu

F32 = jnp.float32
BF16 = jnp.bfloat16
HIGHEST = lax.Precision.HIGHEST

GRID_W = 64
N_ADA = 6
EPS = 1e-6
HY_BANDS = 16
LANES = 128
SUBLANES = 8
DFT_N2 = LANES
GROUP_PITCH = DFT_N2 + SUBLANES // 2
VMEM_LIMIT = 60 * 1024 * 1024


def _cparams(sem):
    return pltpu.CompilerParams(dimension_semantics=sem, vmem_limit_bytes=VMEM_LIMIT)


def _dot(a, b):
    return jnp.dot(a, b, preferred_element_type=F32)


def _dot_hi(a, b):
    return jnp.dot(a, b, preferred_element_type=F32, precision=HIGHEST)


def _silu(x):
    return x * jax.nn.sigmoid(x)


def _rms_mod(x, g, shift, scale):
    y = x * lax.rsqrt(jnp.mean(x * x, axis=-1, keepdims=True) + EPS)
    return (y * g) * (1.0 + scale) + shift


def _pos_block(rows, cols, tm):
    nr = tm // GRID_W
    dh = rows.shape[-1]
    rp = jnp.broadcast_to(rows[:, None, :], (nr, GRID_W, dh)).reshape(tm, dh)
    cp = jnp.broadcast_to(cols[None], (nr, GRID_W, dh)).reshape(tm, dh)
    return rp, cp


def _store_padded(ref, lead, val):
    pad = jnp.zeros((GROUP_PITCH - DFT_N2, val.shape[-1]), val.dtype)
    for g in range(val.shape[0] // DFT_N2):
        ref[lead, g * GROUP_PITCH:g * GROUP_PITCH + DFT_N2, :] = val[g * DFT_N2:(g + 1) * DFT_N2]
        ref[lead, g * GROUP_PITCH + DFT_N2:(g + 1) * GROUP_PITCH, :] = pad


def _ada_kernel(ct_ref, w_ref, b_ref, o_ref, *, n_vec):
    sc = _silu(ct_ref[...])
    w = w_ref[...]
    rows = [jnp.sum(w * sc[:, r:r + 1], axis=0, keepdims=True) for r in range(n_vec)]
    rows += [jnp.zeros_like(rows[0])] * (SUBLANES - n_vec)
    o_ref[...] = jnp.concatenate(rows, axis=0) + b_ref[...]


def _ada(cond_t, w, b, n_vec, tn=1024):
    d, n = w.shape
    return pl.pallas_call(
        functools.partial(_ada_kernel, n_vec=n_vec),
        grid=(n // tn,),
        in_specs=[pl.BlockSpec((d, SUBLANES), lambda j: (0, 0)),
                  pl.BlockSpec((d, tn), lambda j: (0, j)),
                  pl.BlockSpec((1, tn), lambda j: (0, j))],
        out_specs=pl.BlockSpec((SUBLANES, tn), lambda j: (0, j)),
        out_shape=jax.ShapeDtypeStruct((SUBLANES, n), F32),
        compiler_params=_cparams(("arbitrary",)),
        name="ada",
    )(cond_t, w, b)


def _ctx_kernel(x_ref, g_ref, sh_ref, sc_ref, w_ref, o_ref):
    h = _rms_mod(x_ref[0], g_ref[...], sh_ref[0], sc_ref[0]).astype(BF16)
    o_ref[0] = _dot(h, w_ref[...]).astype(o_ref.dtype)


def _ctx_proj(x, g, shift, scale, w, n_cols, tm):
    bsz, n, d = x.shape
    vec = pl.BlockSpec((1, 1, d), lambda b, i: (b, 0, 0))
    return pl.pallas_call(
        _ctx_kernel,
        grid=(bsz, n // tm),
        in_specs=[pl.BlockSpec((1, tm, d), lambda b, i: (b, i, 0)),
                  pl.BlockSpec((1, d), lambda b, i: (0, 0)), vec, vec,
                  pl.BlockSpec((d, n_cols), lambda b, i: (0, 0))],
        out_specs=pl.BlockSpec((1, tm, n_cols), lambda b, i: (b, i, 0)),
        out_shape=jax.ShapeDtypeStruct((bsz, n, n_cols), BF16),
        compiler_params=_cparams(("parallel", "parallel")),
        name="ctx_proj",
    )(x, g, shift, scale, w)


PRO_CHUNKS = 8


def _next_block(b, i, ni, nb):
    t = jnp.minimum(b * ni + i + 1, nb * ni - 1)
    return t // ni, t % ni


def _in_pipe_kernel(x0_ref, rows0_ref, sh0_ref, sc0_ref, xn_ref, rowsn_ref, shn_ref, scn_ref,
                    cols_ref, g_ref, w_ref, o_ref, h_ref, *, tm):
    b, i, j = pl.program_id(0), pl.program_id(1), pl.program_id(2)
    par = (b * pl.num_programs(1) + i) % 2
    cr = tm // PRO_CHUNKS
    dh = cols_ref.shape[-1]

    @pl.when(jnp.logical_and(jnp.logical_and(b == 0, i == 0), j == 0))
    def _():
        x = x0_ref[0] + jnp.concatenate(_pos_block(rows0_ref[...], cols_ref[...], tm), axis=-1)
        h_ref[0] = _rms_mod(x, g_ref[...], sh0_ref[0], sc0_ref[0]).astype(BF16)

    o_ref[0] = _dot(h_ref[par], w_ref[...]).astype(o_ref.dtype)

    c = jnp.minimum(j, PRO_CHUNKS - 1)
    rows = pl.ds(pl.multiple_of(c * cr, cr), cr)
    gpc = cr // GRID_W
    rp = jnp.concatenate([jnp.broadcast_to(rowsn_ref[pl.ds(c * gpc + r, 1), :], (GRID_W, dh))
                          for r in range(gpc)], axis=0)
    cp = jnp.concatenate([cols_ref[...]] * gpc, axis=0)
    xc = xn_ref[0, rows, :] + jnp.concatenate([rp, cp], axis=-1)
    h_ref[1 - par, rows, :] = _rms_mod(xc, g_ref[...], shn_ref[0], scn_ref[0]).astype(BF16)


def _in_proj_latent(x, pos_tabs, g, shift, scale, w, tm, tn):
    bsz, n, d = x.shape
    n_cols = w.shape[1]
    rows_tab, cols_tab = pos_tabs
    dh = d // 2
    ni = n // tm
    nr = tm // GRID_W
    assert (tm // PRO_CHUNKS) % GRID_W == 0 and n_cols // tn >= PRO_CHUNKS
    nxt = lambda b, i: _next_block(b, i, ni, bsz)
    once = dict(pipeline_mode=pl.Buffered(1))
    return pl.pallas_call(
        functools.partial(_in_pipe_kernel, tm=tm),
        grid=(bsz, ni, n_cols // tn),
        in_specs=[
            pl.BlockSpec((1, tm, d), lambda b, i, j: (0, 0, 0), **once),
            pl.BlockSpec((nr, dh), lambda b, i, j: (0, 0)),
            pl.BlockSpec((1, 1, d), lambda b, i, j: (0, 0, 0)),
            pl.BlockSpec((1, 1, d), lambda b, i, j: (0, 0, 0)),
            pl.BlockSpec((1, tm, d), lambda b, i, j: nxt(b, i) + (0,)),
            pl.BlockSpec((nr, dh), lambda b, i, j: (nxt(b, i)[1], 0)),
            pl.BlockSpec((1, 1, d), lambda b, i, j: (nxt(b, i)[0], 0, 0)),
            pl.BlockSpec((1, 1, d), lambda b, i, j: (nxt(b, i)[0], 0, 0)),
            pl.BlockSpec((GRID_W, dh), lambda b, i, j: (0, 0)),
            pl.BlockSpec((1, d), lambda b, i, j: (0, 0)),
            pl.BlockSpec((d, tn), lambda b, i, j: (0, j)),
        ],
        out_specs=pl.BlockSpec((1, tm, tn), lambda b, i, j: (b, i, j)),
        out_shape=jax.ShapeDtypeStruct((bsz, n, n_cols), BF16),
        scratch_shapes=[pltpu.VMEM((2, tm, d), BF16)],
        compiler_params=_cparams(("arbitrary", "arbitrary", "arbitrary")),
        name="in_proj",
    )(x, rows_tab, shift, scale, x, rows_tab, shift, scale, cols_tab, g, w)


S5_CHUNKS = 8


def _s5_rows(k, tb, pitch):
    pad = pitch - tb
    if (k * pitch) % SUBLANES == 0:
        return k * pitch, k * pitch + tb
    return k * pitch - pad, k * pitch + tb + pad


def _s5_fill(buf, lhs_of, bd_ref, k, tb, pitch):
    lhs = lhs_of(k)
    lo, hi = _s5_rows(k, tb, pitch)
    if lo != k * pitch:
        z = jnp.zeros((k * pitch - lo, LANES), F32)
        lhs = jnp.concatenate([z, lhs.astype(F32), z], axis=0).astype(BF16)
    bu = _dot(lhs, bd_ref[0, k])
    for m in range(buf.shape[0]):
        buf[m, lo:hi, :] = bu[:, m * LANES:(m + 1) * LANES]


def _s5_pass(d, bu_scan, st_scan, carry_ref, st_mm, bu_mm, lhs_of, y_ref, bd_ref, cd_ref,
             lam_r, lam_i, tb, pitch):
    nslab = bu_scan.shape[0]
    half = nslab // 2
    per = tb // S5_CHUNKS
    sr = [carry_ref[m] for m in range(half)]
    si = [carry_ref[half + m] for m in range(half)]

    def scan_steps(q0, q1):
        for q in range(q0, q1):
            idx = pl.ds(jnp.where(d == 0, q, tb - 1 - q), SUBLANES, stride=pitch)
            for m in range(half):
                nr = lam_r[m] * sr[m] - lam_i[m] * si[m] + bu_scan[m, idx, :]
                ni = lam_r[m] * si[m] + lam_i[m] * sr[m] + bu_scan[half + m, idx, :]
                st_scan[m, idx, :] = nr
                st_scan[half + m, idx, :] = ni
                sr[m], si[m] = nr, ni

    for k in range(S5_CHUNKS):
        lo, hi = _s5_rows(k, tb, pitch)
        sk = jnp.concatenate([st_mm[m, lo:hi, :] for m in range(nslab)], axis=-1)
        yk = _dot(sk.astype(BF16), cd_ref[0, k])
        y_ref[0, :, k * LANES:(k + 1) * LANES] = yk[k * pitch - lo:k * pitch - lo + tb]
        scan_steps(k * per, k * per + per // 2)
        _s5_fill(bu_mm, lhs_of, bd_ref, k, tb, pitch)
        scan_steps(k * per + per // 2, (k + 1) * per)
    for m in range(half):
        carry_ref[m] = sr[m]
        carry_ref[half + m] = si[m]


def _s5_kernel(uc0_ref, uca_ref, uxa_ref, ucb_ref, uxb_ref, bd_ref, cd_ref, lr_ref, li_ref,
               ya_ref, yb_ref, a_bu, a_st, b_bu, b_st, ca_ref, cb_ref, *, tb, pitch, nc, nblk):
    d = pl.program_id(0)
    i = pl.program_id(1)
    half = a_bu.shape[0] // 2
    lam_r = [lr_ref[0, m] for m in range(half)]
    lam_i = [li_ref[0, m] for m in range(half)]

    @pl.when(i == 0)
    def _():
        ca_ref[...] = jnp.zeros_like(ca_ref)
        cb_ref[...] = jnp.zeros_like(cb_ref)
        a_st[...] = jnp.zeros_like(a_st)
        b_st[...] = jnp.zeros_like(b_st)
        for k in range(S5_CHUNKS):
            _s5_fill(a_bu, lambda k: uc0_ref[0, :, k * LANES:(k + 1) * LANES], bd_ref, k, tb, pitch)

    def lhs(uc_ref, ux_ref, pos):
        def of(k):
            cols = slice(k * LANES, (k + 1) * LANES)
            return jnp.where(pos < nc, uc_ref[0, :, cols], ux_ref[0, :, cols])
        return of

    _s5_pass(d, a_bu, a_st, ca_ref, b_st, b_bu, lhs(ucb_ref, uxb_ref, i), yb_ref,
             bd_ref, cd_ref, lam_r, lam_i, tb, pitch)

    @pl.when(i < nblk)
    def _():
        _s5_pass(d, b_bu, b_st, cb_ref, a_st, a_bu, lhs(uca_ref, uxa_ref, i + 1), ya_ref,
                 bd_ref, cd_ref, lam_r, lam_i, tb, pitch)


def _s5(u_c, p_x, bd, cd, lam_r, lam_i, s5w, tb):
    bsz, lc, _ = u_c.shape
    assert bsz == 2, "the two batch elements are the two interleaved sequences"
    n = p_x.shape[1]
    nc, nb = lc // tb, n // tb
    nblk = nc + nb
    nslab = bd.shape[-1] // LANES
    pitch = tb + SUBLANES // 2

    def blk(d, pos, count):
        j = jnp.clip(pos, 0, count - 1)
        return jnp.where(d == 0, j, count - 1 - j)

    def u_spec(b, off, ctx):
        if ctx:
            return pl.BlockSpec((1, tb, s5w), lambda d, i: (b, blk(d, i + off, nc), 0))
        return pl.BlockSpec((1, tb, s5w), lambda d, i: (b, blk(d, i + off - nc, nb), 0))

    def y_spec(off):
        return pl.BlockSpec((1, tb, s5w), lambda d, i: (d, blk(d, i + off - nc, nb), 0))

    par = lambda d, i: (d, 0, 0, 0)
    buf = pltpu.VMEM((nslab, S5_CHUNKS * pitch, LANES), F32)
    state = pltpu.VMEM((nslab, SUBLANES, LANES), F32)
    return pl.pallas_call(
        functools.partial(_s5_kernel, tb=tb, pitch=pitch, nc=nc, nblk=nblk),
        grid=(2, nblk + 1),
        in_specs=[
            pl.BlockSpec((1, tb, s5w), lambda d, i: (0, blk(d, 0, nc), 0)),
            u_spec(0, 1, True), u_spec(0, 1, False),
            u_spec(1, 0, True), u_spec(1, 0, False),
            pl.BlockSpec((1, S5_CHUNKS, LANES, nslab * LANES), par, pipeline_mode=pl.Buffered(1)),
            pl.BlockSpec((1, S5_CHUNKS, nslab * LANES, LANES), par, pipeline_mode=pl.Buffered(1)),
            pl.BlockSpec((1, nslab // 2, SUBLANES, LANES), par),
            pl.BlockSpec((1, nslab // 2, SUBLANES, LANES), par),
        ],
        out_specs=[y_spec(0), y_spec(-1)],
        out_shape=[jax.ShapeDtypeStruct((2, n, s5w), F32)] * 2,
        scratch_shapes=[buf, buf, buf, buf, state, state],
        compiler_params=_cparams(("arbitrary", "arbitrary")),
        name="s5",
    )(u_c, u_c, p_x, u_c, p_x, bd, cd, lam_r, lam_i)


def _s5_params(a_re, a_im, log_dt, b_re, b_im, c_re, c_im):
    g, p, h = b_re.shape
    gl = g // S5_CHUNKS
    dt = jnp.exp(log_dt.astype(F32))[:, None]
    mag = jnp.exp(a_re.astype(F32) * dt)
    lr, li = mag * jnp.cos(a_im.astype(F32) * dt), mag * jnp.sin(a_im.astype(F32) * dt)
    den = a_re * a_re + a_im * a_im
    qr = ((lr - 1.0) * a_re + li * a_im) / den
    qi = (li * a_re - (lr - 1.0) * a_im) / den
    bbr = qr[:, :, None] * b_re - qi[:, :, None] * b_im
    bbi = qr[:, :, None] * b_im + qi[:, :, None] * b_re
    eye = jnp.eye(gl, dtype=F32)

    def pack_b(m):
        return jnp.einsum("kgph,gq->kghqp", m.reshape(S5_CHUNKS, gl, p, h), eye).reshape(
            S5_CHUNKS, gl * h, gl * p)

    def pack_c(m):
        return jnp.einsum("kghp,gq->kgpqh", m.reshape(S5_CHUNKS, gl, h, p), eye).reshape(
            S5_CHUNKS, gl * p, gl * h)

    bd = jnp.concatenate([pack_b(bbr), pack_b(bbi)], axis=-1).astype(BF16)
    cd = jnp.concatenate([pack_c(c_re.astype(F32)), -pack_c(c_im.astype(F32))], axis=1).astype(BF16)

    def slabs(v):
        return v.reshape(S5_CHUNKS, -1, LANES).transpose(1, 0, 2)

    return bd, cd, slabs(lr), slabs(li)


def _filt_kernel(f_ref, w1_ref, b1_ref, q1_ref, w2_ref, b2_ref, q2_ref, w3_ref, dec_ref,
                 bias_ref, o_ref, *, tq):
    f = f_ref[...]
    h = jnp.sin(q1_ref[...] * (_dot_hi(f, w1_ref[...]) + b1_ref[...]))
    h = jnp.sin(q2_ref[...] * (_dot_hi(h, w2_ref[...]) + b2_ref[...]))
    h = _dot_hi(h, w3_ref[...])
    c = bias_ref.shape[-1]
    t = f[:, 0:1]
    lag = pl.program_id(0) * tq + lax.broadcasted_iota(jnp.int32, (tq, 1), 0)
    fwd = h[:, :c] * jnp.exp(-t * jnp.abs(dec_ref[0:1, :]))
    bwd = h[:, c:] * jnp.exp(-t * jnp.abs(dec_ref[1:2, :]))
    fwd = jnp.where(lag == 0, fwd + bias_ref[...], fwd)
    bwd = jnp.where(lag == 0, 0.0, bwd)
    _store_padded(o_ref, 0, fwd)
    _store_padded(o_ref, 1, bwd)


def _filter_taps(feats, w1, b1, q1, w2, b2, q2, w3, decay, bias, tq=512):
    n_lat, fp = feats.shape
    hid = w2.shape[0]
    c = bias.shape[-1]
    rows = tq // DFT_N2 * GROUP_PITCH
    const = lambda r: (0, 0)
    return pl.pallas_call(
        functools.partial(_filt_kernel, tq=tq),
        grid=(n_lat // tq,),
        in_specs=[pl.BlockSpec((tq, fp), lambda r: (r, 0)),
                  pl.BlockSpec((fp, hid), const), pl.BlockSpec((1, hid), const),
                  pl.BlockSpec((1, hid), const),
                  pl.BlockSpec((hid, hid), const), pl.BlockSpec((1, hid), const),
                  pl.BlockSpec((1, hid), const),
                  pl.BlockSpec((hid, 2 * c), const),
                  pl.BlockSpec((2, c), const),
                  pl.BlockSpec((1, c), const)],
        out_specs=pl.BlockSpec((2, rows, c), lambda r: (0, r, 0)),
        out_shape=jax.ShapeDtypeStruct((2, n_lat // DFT_N2 * GROUP_PITCH, c), F32),
        compiler_params=_cparams(("parallel",)),
        name="hy_filter",
    )(feats, w1, b1, q1, w2, b2, q2, w3, decay, bias)


def _conv3(cur_ref, prev_ref, next_ref, w_ref, b_ref, i, last, tm):
    row = lax.broadcasted_iota(jnp.int32, (tm, 1), 0)
    cur = cur_ref[0].astype(F32)
    halo = prev_ref.shape[1]
    prev_row = jnp.where(i == 0, 0.0, prev_ref[0, halo - 1:halo, :].astype(F32))
    next_row = jnp.where(i == last, 0.0, next_ref[0, 0:1, :].astype(F32))
    up = jnp.where(row == 0, prev_row, pltpu.roll(cur, 1, 0))
    dn = jnp.where(row == tm - 1, next_row, pltpu.roll(cur, tm - 1, 0))
    w = w_ref[...]
    return up * w[0:1] + cur * w[1:2] + dn * w[2:3] + b_ref[...]


CONV_HALO = 16


def _conv3_specs(tm, c, n, col_block):
    hb = tm // CONV_HALO
    nhb = n // CONV_HALO
    return [pl.BlockSpec((1, tm, c), lambda b, i: (b, i, col_block)),
            pl.BlockSpec((1, CONV_HALO, c),
                         lambda b, i: (b, jnp.maximum(i * hb - 1, 0), col_block)),
            pl.BlockSpec((1, CONV_HALO, c),
                         lambda b, i: (b, jnp.minimum((i + 1) * hb, nhb - 1), col_block))]


def _sconv_kernel(x1_ref, x1p_ref, x1n_ref, v_ref, vp_ref, vn_ref, w1_ref, wv_ref, b1_ref, bv_ref,
                  u_ref, *, tm):
    i = pl.program_id(1)
    last = pl.num_programs(1) - 1
    x1 = _conv3(x1_ref, x1p_ref, x1n_ref, w1_ref, b1_ref, i, last, tm)
    v = _conv3(v_ref, vp_ref, vn_ref, wv_ref, bv_ref, i, last, tm)
    _store_padded(u_ref, 0, x1 * v)


def _short_conv(p_x, conv_w, conv_b, col0, c, tm):
    bsz, n, _ = p_x.shape
    cb0 = col0 // c
    part = lambda p, shape: pl.BlockSpec(shape, lambda b, i: (0, p))
    return pl.pallas_call(
        functools.partial(_sconv_kernel, tm=tm),
        grid=(bsz, n // tm),
        in_specs=(_conv3_specs(tm, c, n, cb0 + 1) + _conv3_specs(tm, c, n, cb0 + 2)
                  + [part(1, (3, c)), part(2, (3, c)), part(1, (1, c)), part(2, (1, c))]),
        out_specs=pl.BlockSpec((1, tm // DFT_N2 * GROUP_PITCH, c), lambda b, i: (b, i, 0)),
        out_shape=jax.ShapeDtypeStruct((bsz, n // DFT_N2 * GROUP_PITCH, c), F32),
        compiler_params=_cparams(("parallel", "parallel")),
        name="short_conv",
    )(p_x, p_x, p_x, p_x, p_x, p_x, conv_w, conv_w, conv_b, conv_b)


def _dft_tables(n1):
    n = n1 * DFT_N2
    k1 = jnp.arange(n1, dtype=jnp.int32)
    n2 = jnp.arange(DFT_N2, dtype=jnp.int32)

    def unit(ph, period):
        ang = (ph % period).astype(F32) * F32(2.0 * math.pi / period)
        return jnp.cos(ang), -jnp.sin(ang)

    ar, ai = unit(k1[:, None] * k1[None, :], n1)
    tr, ti = unit(n2[:, None] * k1[None, :], n)
    gr = ar[None] * tr[:, :, None] - ai[None] * ti[:, :, None]
    gi = ar[None] * ti[:, :, None] + ai[None] * tr[:, :, None]
    h = n1 // 2
    g1c = jnp.concatenate([jnp.concatenate([gr[:, :, :h], -gi[:, :, :h]], axis=2),
                           jnp.concatenate([gi[:, :, :h], gr[:, :, :h]], axis=2)], axis=1)
    m = jnp.arange(h, dtype=jnp.int32)

    def fold(g):
        up = jnp.take(g, n1 - 1 - m, axis=2)
        up0 = jnp.take(g[0], (n1 - m) % n1, axis=1) * (m != 0).astype(F32)
        return jnp.concatenate([g[:, :, :h], up.at[0].set(up0)], axis=2)

    g1f = jnp.concatenate([fold(gr), fold(gi)], axis=1)
    fr, fi = unit(n2[:, None] * n2[None, :], DFT_N2)
    f2 = jnp.concatenate([jnp.concatenate([fr, -fi], axis=1),
                          jnp.concatenate([fi, fr], axis=1)], axis=0)
    f2c = jnp.concatenate([jnp.concatenate([fr, fi], axis=1),
                           jnp.concatenate([-fi, fr], axis=1)], axis=0) / n
    return g1c.astype(BF16), g1f.astype(BF16), f2.astype(BF16), f2c.astype(BF16)


def _stage1(gather, g_ref, a_ref, cn, n1, pitch):
    def trip(chunk, carry):
        for q in range(cn):
            n2 = chunk * cn + q
            a = _dot(g_ref[n2], gather(n2).astype(BF16))
            a_ref[0, pl.ds(n2, n1, stride=pitch), :] = a[:n1]
            a_ref[1, pl.ds(n2, n1, stride=pitch), :] = a[n1:]
        return carry
    lax.fori_loop(0, DFT_N2 // cn, trip, 0)


def _a_rows(chunk, ck, q, pitch):
    return pl.ds(pl.multiple_of(chunk * (ck * pitch), SUBLANES) + q * pitch, DFT_N2)


def _load_a_chunk(a_ref, chunk, ck, pitch):
    tiles = []
    for q in range(ck):
        rows = _a_rows(chunk, ck, q, pitch)
        tiles.append(jnp.concatenate([a_ref[0, rows, :], a_ref[1, rows, :]], axis=0).astype(BF16))
    return jnp.concatenate(tiles, axis=1)


def _spec_kernel(t_ref, g_ref, f2_ref, o_ref, a_ref, *, n1, cn, ck, pitch):
    s = pl.program_id(1)
    h = n1 // 2

    @pl.when(s == 0)
    def _():
        def gather(n2):
            back = (DFT_N2 - n2) % DFT_N2
            return jnp.concatenate([t_ref[0, pl.ds(n2, h, stride=GROUP_PITCH), :],
                                    t_ref[1, pl.ds(back, h, stride=GROUP_PITCH), :]], axis=0)
        _stage1(gather, g_ref, a_ref, cn, n1, pitch)

    @pl.when(s >= 1)
    def _():
        o_ref[0] = _dot(f2_ref[...], _load_a_chunk(a_ref, s - 1, ck, pitch)).astype(o_ref.dtype)


def _conv_kernel(u_ref, g1_ref, kf_ref, f2_ref, f2c_ref, y_ref, a_ref,
                 *, n1, cn, ck, pitch, np2):
    s = pl.program_id(1)
    h = n1 // 2

    @pl.when(s == 0)
    def _():
        y_ref[...] = jnp.zeros_like(y_ref)

        def gather(n2):
            idx = pl.ds(n2, h, stride=GROUP_PITCH)
            return jnp.concatenate([u_ref[0, idx, :], u_ref[1, idx, :]], axis=0)
        _stage1(gather, g1_ref, a_ref, cn, n1, pitch)

    @pl.when(jnp.logical_and(s >= 1, s <= np2))
    def _():
        chunk = s - 1
        x = _dot(f2_ref[...], _load_a_chunk(a_ref, chunk, ck, pitch))
        xr, xi = x[:DFT_N2], x[DFT_N2:]
        kr, ki = kf_ref[0, :DFT_N2, :].astype(F32), kf_ref[0, DFT_N2:, :].astype(F32)
        y = jnp.concatenate([xr * kr - xi * ki, xr * ki + xi * kr], axis=0)
        b = _dot(f2c_ref[...], y.astype(BF16))
        for q in range(ck):
            rows = _a_rows(chunk, ck, q, pitch)
            a_ref[0, rows, :] = b[:DFT_N2, q * LANES:(q + 1) * LANES]
            a_ref[1, rows, :] = b[DFT_N2:, q * LANES:(q + 1) * LANES]

    @pl.when(s > np2)
    def _():
        def trip(chunk, carry):
            for q in range(cn):
                n2 = chunk * cn + q
                idx = pl.ds(n2, n1, stride=pitch)
                b = jnp.concatenate([a_ref[0, idx, :], a_ref[1, idx, :]], axis=0)
                y = lax.dot_general(g1_ref[n2], b.astype(BF16), (((0,), (0,)), ((), ())),
                                    preferred_element_type=F32)
                out = pl.ds(n2, h, stride=GROUP_PITCH)
                y_ref[0, out, :] = y[:h]
                y_ref[1, out, :] = y[h:]
            return carry
        lax.fori_loop(0, DFT_N2 // cn, trip, 0)


DFT_UNROLL = 32
A_PITCH = DFT_N2 + SUBLANES


def _filter_spectrum(taps, g1f, f2):
    _, rows, c = taps.shape
    n1 = 2 * rows // GROUP_PITCH
    ck = min(32, n1)
    np2 = n1 // ck
    pitch = A_PITCH
    return pl.pallas_call(
        functools.partial(_spec_kernel, n1=n1, cn=DFT_UNROLL, ck=ck, pitch=pitch),
        grid=(c // LANES, 1 + np2),
        in_specs=[pl.BlockSpec((2, rows, LANES), lambda j, s: (0, 0, j),
                               pipeline_mode=pl.Buffered(1)),
                  pl.BlockSpec((DFT_N2, 2 * n1, n1), lambda j, s: (0, 0, 0),
                               pipeline_mode=pl.Buffered(1)),
                  pl.BlockSpec((2 * DFT_N2, 2 * DFT_N2), lambda j, s: (0, 0))],
        out_specs=pl.BlockSpec((1, 2 * DFT_N2, ck * LANES),
                               lambda j, s: (j, 0, jnp.maximum(s - 1, 0))),
        out_shape=jax.ShapeDtypeStruct((c // LANES, 2 * DFT_N2, n1 * LANES), BF16),
        scratch_shapes=[pltpu.VMEM((2, n1 * pitch, LANES), F32)],
        compiler_params=_cparams(("parallel", "arbitrary")),
        name="hy_spectrum",
    )(taps, g1f, f2)


def _long_conv(u, kf, g1c, f2, f2c):
    bsz, rows, c = u.shape
    assert bsz == 2, "the two batch elements are packed as one complex signal"
    n1 = 2 * rows // GROUP_PITCH
    ck = min(16, n1)
    np2 = n1 // ck
    pitch = A_PITCH
    once = dict(pipeline_mode=pl.Buffered(1))
    return pl.pallas_call(
        functools.partial(_conv_kernel, n1=n1, cn=DFT_UNROLL, ck=ck, pitch=pitch, np2=np2),
        grid=(c // LANES, np2 + 2),
        in_specs=[
            pl.BlockSpec((2, rows, LANES), lambda j, s: (0, 0, j), **once),
            pl.BlockSpec((DFT_N2, 2 * n1, n1), lambda j, s: (0, 0, 0), **once),
            pl.BlockSpec((1, 2 * DFT_N2, ck * LANES),
                         lambda j, s: (j, 0, jnp.clip(s - 1, 0, np2 - 1))),
            pl.BlockSpec((2 * DFT_N2, 2 * DFT_N2), lambda j, s: (0, 0)),
            pl.BlockSpec((2 * DFT_N2, 2 * DFT_N2), lambda j, s: (0, 0)),
        ],
        out_specs=pl.BlockSpec((2, rows, LANES), lambda j, s: (0, 0, j), **once),
        out_shape=jax.ShapeDtypeStruct((2, rows, c), F32),
        scratch_shapes=[pltpu.VMEM((2, n1 * pitch, LANES), F32)],
        compiler_params=_cparams(("parallel", "arbitrary")),
        name="hy_conv",
    )(u, g1c, kf, f2, f2c)


def _mixer_kernel(ya_ref, yb_ref, u_ref, h0_ref, h0p_ref, h0n_ref, cw_ref, cb_ref, yc_ref, gs_ref,
                  gh_ref, x_ref, rows_ref, cols_ref, d_ref, gate_ref, wglu_ref, why_ref, wout_ref,
                  o_ref, *, tm, cw):
    first = pl.program_id(0) == 0
    y = (jnp.where(first, ya_ref[0], yb_ref[0]) + jnp.where(first, ya_ref[1], yb_ref[1])
         + u_ref[0].astype(F32) * d_ref[...])
    h = jax.nn.gelu(y, approximate=True).astype(BF16)
    y_conv = jnp.concatenate(
        [yc_ref[0, g * GROUP_PITCH:g * GROUP_PITCH + DFT_N2, :] for g in range(tm // DFT_N2)],
        axis=0)
    x0 = _conv3(h0_ref, h0p_ref, h0n_ref, cw_ref, cb_ref, pl.program_id(1),
                pl.num_programs(1) - 1, tm)
    hx = (x0 * y_conv).astype(BF16)
    d = o_ref.shape[-1]
    merged = []
    for c in range(d // cw):
        cols = slice(c * cw, (c + 1) * cw)
        glu_v = _dot(h, wglu_ref[:, cols])
        glu_g = _dot(h, wglu_ref[:, d + c * cw:d + (c + 1) * cw])
        branch_hy = _dot(hx, why_ref[:, cols])
        m = (jax.nn.sigmoid(gs_ref[0, :, cols].astype(F32)) * (glu_v * jax.nn.sigmoid(glu_g))
             + jax.nn.sigmoid(gh_ref[0, :, cols].astype(F32)) * branch_hy)
        merged.append(m.astype(BF16))
    merged = jnp.concatenate(merged, axis=-1)
    dh = d // 2
    rp, cp = _pos_block(rows_ref[0], cols_ref[...], tm)
    for c in range(d // cw):
        cols = slice(c * cw, (c + 1) * cw)
        pos = rp[:, cols] if (c + 1) * cw <= dh else cp[:, c * cw - dh:(c + 1) * cw - dh]
        o_ref[0, :, cols] = (x_ref[0, :, cols] + pos
                             + gate_ref[0, :, cols] * _dot(merged, wout_ref[:, cols]))


def _mixer(y_ssm, p_x, conv_w, conv_b, y_conv, x, pos_tabs, s5_d, gate, w_glu, w_hy_out, w_out,
           s5w, gate_col0, tm):
    bsz, n, d = x.shape
    assert bsz == 2, "the S5 readouts arrive as one array per batch element"
    hyw = w_hy_out.shape[0]
    rows_tab, cols_tab = pos_tabs
    nr = tm // GRID_W
    dh = d // 2
    ni = n // tm
    g0 = gate_col0 // d
    cw = min(512, dh)
    resident = dict(pipeline_mode=pl.Buffered(1))
    tok = lambda width, col: pl.BlockSpec((1, tm, width), lambda b, i: (b, i, col))
    return pl.pallas_call(
        functools.partial(_mixer_kernel, tm=tm, cw=cw),
        grid=(bsz, ni),
        in_specs=[
            pl.BlockSpec((2, tm, s5w), lambda b, i: (0, jnp.where(b == 0, i, ni - 1), 0)),
            pl.BlockSpec((2, tm, s5w), lambda b, i: (0, jnp.where(b == 0, 0, i), 0)),
            tok(s5w, 0),
            *_conv3_specs(tm, hyw, n, s5w // hyw),
            pl.BlockSpec((3, hyw), lambda b, i: (0, 0)),
            pl.BlockSpec((1, hyw), lambda b, i: (0, 0)),
            pl.BlockSpec((1, tm // DFT_N2 * GROUP_PITCH, hyw), lambda b, i: (b, i, 0)),
            tok(d, g0), tok(d, g0 + 1), tok(d, 0),
            pl.BlockSpec((1, nr, dh), lambda b, i: (i, 0, 0)),
            pl.BlockSpec((GRID_W, dh), lambda b, i: (0, 0)),
            pl.BlockSpec((1, s5w), lambda b, i: (0, 0)),
            pl.BlockSpec((1, 1, d), lambda b, i: (b, 0, 0)),
            pl.BlockSpec((s5w, 2 * d), lambda b, i: (0, 0), **resident),
            pl.BlockSpec((hyw, d), lambda b, i: (0, 0), **resident),
            pl.BlockSpec((d, d), lambda b, i: (0, 0), **resident),
        ],
        out_specs=pl.BlockSpec((1, tm, d), lambda b, i: (b, i, 0)),
        out_shape=jax.ShapeDtypeStruct((bsz, n, d), F32),
        compiler_params=_cparams(("arbitrary", "arbitrary")),
        name="mixer",
    )(y_ssm[0], y_ssm[1], p_x, p_x, p_x, p_x, conv_w, conv_b, y_conv, p_x, p_x, x,
      rows_tab.reshape(-1, nr, dh), cols_tab, s5_d, gate, w_glu, w_hy_out, w_out)


def _ffn_kernel(x_ref, g_ref, sh_ref, sc_ref, gate_ref, wa_ref, wb_ref, wo_ref, nf_ref,
                o_ref, h_ref, acc_ref):
    j = pl.program_id(2)

    @pl.when(j == 0)
    def _():
        h_ref[...] = _rms_mod(x_ref[0], g_ref[...], sh_ref[0], sc_ref[0]).astype(BF16)
        acc_ref[...] = jnp.zeros_like(acc_ref)

    h = h_ref[...]
    act = _silu(_dot(h, wa_ref[...])) * _dot(h, wb_ref[...])
    acc_ref[...] += _dot(act.astype(BF16), wo_ref[...])

    @pl.when(j == pl.num_programs(2) - 1)
    def _():
        xo = x_ref[0] + gate_ref[0] * acc_ref[...]
        o_ref[0] = xo * lax.rsqrt(jnp.mean(xo * xo, axis=-1, keepdims=True) + EPS) * nf_ref[...]


def _ffn(x, g, shift, scale, gate, w_in, w_out, norm_f, tm, tf):
    bsz, n, d = x.shape
    dff = w_out.shape[0]
    nj = dff // tf
    vec = pl.BlockSpec((1, 1, d), lambda b, i, j: (b, 0, 0))
    return pl.pallas_call(
        _ffn_kernel,
        grid=(bsz, n // tm, nj),
        in_specs=[pl.BlockSpec((1, tm, d), lambda b, i, j: (b, i, 0)),
                  pl.BlockSpec((1, d), lambda b, i, j: (0, 0)),
                  vec, vec, vec,
                  pl.BlockSpec((d, tf), lambda b, i, j: (0, j)),
                  pl.BlockSpec((d, tf), lambda b, i, j: (0, nj + j)),
                  pl.BlockSpec((tf, d), lambda b, i, j: (j, 0)),
                  pl.BlockSpec((1, d), lambda b, i, j: (0, 0))],
        out_specs=pl.BlockSpec((1, tm, d), lambda b, i, j: (b, i, 0)),
        out_shape=jax.ShapeDtypeStruct((bsz, n, d), F32),
        scratch_shapes=[pltpu.VMEM((tm, d), BF16), pltpu.VMEM((tm, d), F32)],
        compiler_params=_cparams(("parallel", "parallel", "arbitrary")),
        name="ffn",
    )(x, g, shift, scale, gate, w_in, w_in, w_out, norm_f)


def _pos_tables(n_rows, d):
    quarter = d // 4
    omega = 10000.0 ** (-jnp.arange(quarter, dtype=F32) / quarter)
    ar = jnp.arange(n_rows, dtype=F32)[:, None] * omega
    ac = jnp.arange(GRID_W, dtype=F32)[:, None] * omega
    return (jnp.concatenate([jnp.sin(ar), jnp.cos(ar)], axis=-1),
            jnp.concatenate([jnp.sin(ac), jnp.cos(ac)], axis=-1))


def _filter_features(n_lat, width):
    pos = jnp.arange(n_lat, dtype=F32)
    t = pos / float(max(n_lat - 1, 1))
    w = 2.0 * math.pi * pos / n_lat
    bands = jnp.linspace(1e-4, HY_BANDS - 1, HY_BANDS, dtype=F32)
    feats = jnp.concatenate([t[:, None], jnp.cos(w[:, None] * bands), -jnp.sin(w[:, None] * bands)],
                            axis=-1)
    return jnp.pad(feats, ((0, 0), (0, width - feats.shape[1])))


def kernel(x, c, ctx, c_ctx, w_ada, b_ada, norm_mix, w_in, s5_a_re, s5_a_im, s5_log_dt,
           s5_b_re, s5_b_im, s5_c_re, s5_c_im, s5_d, w_glu, hy_conv_w, hy_conv_b,
           hy_f1_w, hy_f1_b, hy_f1_freq, hy_f2_w, hy_f2_b, hy_f2_freq, hy_f3_w, hy_decay,
           hy_bias, w_hy_out, w_out, norm_ffn, w_ffn_in, w_ffn_out, norm_f):
    bsz, n_lat, d = x.shape
    depth = w_ada.shape[0]
    assert depth == 1, "the context stream is only advanced for the single-layer trunk"
    l = 0
    s5w = s5_d.shape[-1]
    hyw = hy_bias.shape[-1]
    n_ctx = ctx.shape[1]
    tm = 512
    tn = min(1024, d)
    tb = 256
    assert n_ctx % tb == 0 and n_lat % tm == 0 and s5w == S5_CHUNKS * LANES

    pos_tabs = _pos_tables(n_lat // GRID_W, d)

    cond_t = jnp.zeros((d, SUBLANES), F32).at[:, :bsz].set(c.T).at[:, bsz].set(c_ctx)
    ada = _ada(cond_t, w_ada[l], b_ada[l][None], bsz + 1, tn=min(1024, d))

    def vec(row0, rows, part):
        v = ada[row0:row0 + rows, part * d:(part + 1) * d]
        return jnp.broadcast_to(v, (bsz, d))[:, None, :]

    shift_mix, scale_mix, gate_mix = vec(0, bsz, 0), vec(0, bsz, 1), vec(0, bsz, 2)
    shift_ffn, scale_ffn, gate_ffn = vec(0, bsz, 3), vec(0, bsz, 4), vec(0, bsz, 5)
    cshift_mix, cscale_mix = vec(bsz, 1, 0), vec(bsz, 1, 1)

    w_in_b = w_in[l].astype(BF16)
    g_mix = norm_mix[l][None]
    u_c = _ctx_proj(ctx, g_mix, cshift_mix, cscale_mix, w_in_b, s5w, tb)
    p_x = _in_proj_latent(x, pos_tabs, g_mix, shift_mix, scale_mix, w_in_b, min(1024, n_lat), tn)

    packed = [_s5_params(s5_a_re[l, k], s5_a_im[l, k], s5_log_dt[l, k], s5_b_re[l, k],
                         s5_b_im[l, k], s5_c_re[l, k], s5_c_im[l, k]) for k in range(2)]
    bd, cd, lam_r, lam_i = (jnp.stack(t) for t in zip(*packed))
    y_ssm = _s5(u_c, p_x, bd, cd, lam_r, lam_i, s5w, tb)

    fp = 64
    feats = _filter_features(n_lat, fp)
    w1 = jnp.pad(hy_f1_w[l], ((0, fp - hy_f1_w.shape[1]), (0, 0)))
    taps = _filter_taps(feats, w1, hy_f1_b[l][None], hy_f1_freq[l][None], hy_f2_w[l],
                        hy_f2_b[l][None], hy_f2_freq[l][None], hy_f3_w[l], hy_decay[l],
                        hy_bias[l][None])
    g1c, g1f, f2, f2c = _dft_tables(2 * n_lat // DFT_N2)
    kf = _filter_spectrum(taps, g1f, f2)
    u_hy = _short_conv(p_x, hy_conv_w[l], hy_conv_b[l][None], s5w, hyw, tm)
    y_conv = _long_conv(u_hy, kf, g1c, f2, f2c)

    x1 = _mixer(y_ssm, p_x, hy_conv_w[l], hy_conv_b[l][None], y_conv, x, pos_tabs, s5_d[l][None],
                gate_mix,
                w_glu[l].astype(BF16), w_hy_out[l].astype(BF16), w_out[l].astype(BF16),
                s5w, s5w + 3 * hyw, 256)

    return _ffn(x1, norm_ffn[l][None], shift_ffn, scale_ffn, gate_ffn,
                w_ffn_in[l].astype(BF16), w_ffn_out[l].astype(BF16), norm_f[None], tm, 512)
```

```python
import functools
import math

import jax
import jax.numpy as jnp
from jax import lax
from jax.experimental import pallas as pl
from jax.experimental.pallas import tpu as pltpu

F32 = jnp.float32
BF16 = jnp.bfloat16
HIGHEST = lax.Precision.HIGHEST

GRID_W = 64
N_ADA = 6
EPS = 1e-6
HY_BANDS = 16
LANES = 128
SUBLANES = 8
DFT_N2 = LANES
GROUP_PITCH = DFT_N2 + SUBLANES // 2
VMEM_LIMIT = 60 * 1024 * 1024


def _cparams(sem):
    return pltpu.CompilerParams(dimension_semantics=sem, vmem_limit_bytes=VMEM_LIMIT)


def _dot(a, b):
    return jnp.dot(a, b, preferred_element_type=F32)


def _dot_hi(a, b):
    return jnp.dot(a, b, preferred_element_type=F32, precision=HIGHEST)


def _dot_3pass(a, b):
    a_hi, b_hi = a.astype(BF16), b.astype(BF16)
    a_lo = (a - a_hi.astype(F32)).astype(BF16)
    b_lo = (b - b_hi.astype(F32)).astype(BF16)
    return _dot(a_hi, b_hi) + (_dot(a_hi, b_lo) + _dot(a_lo, b_hi))


def _silu(x):
    return x * jax.nn.sigmoid(x)


def _rms_mod(x, g, shift, scale):
    y = x * lax.rsqrt(jnp.mean(x * x, axis=-1, keepdims=True) + EPS)
    return (y * g) * (1.0 + scale) + shift


def _pos_block(rows, cols, tm):
    nr = tm // GRID_W
    dh = rows.shape[-1]
    rp = jnp.broadcast_to(rows[:, None, :], (nr, GRID_W, dh)).reshape(tm, dh)
    cp = jnp.broadcast_to(cols[None], (nr, GRID_W, dh)).reshape(tm, dh)
    return rp, cp


def _store_padded(ref, lead, val):
    pad = jnp.zeros((GROUP_PITCH - DFT_N2, val.shape[-1]), val.dtype)
    for g in range(val.shape[0] // DFT_N2):
        ref[lead, g * GROUP_PITCH:g * GROUP_PITCH + DFT_N2, :] = val[g * DFT_N2:(g + 1) * DFT_N2]
        ref[lead, g * GROUP_PITCH + DFT_N2:(g + 1) * GROUP_PITCH, :] = pad


def _ada_kernel(ct_ref, w_ref, b_ref, o_ref, *, n_vec):
    sc = _silu(ct_ref[...])
    w = w_ref[...]
    rows = [jnp.sum(w * sc[:, r:r + 1], axis=0, keepdims=True) for r in range(n_vec)]
    rows += [jnp.zeros_like(rows[0])] * (SUBLANES - n_vec)
    o_ref[...] = jnp.concatenate(rows, axis=0) + b_ref[...]


def _ada(cond_t, w, b, n_vec, tn=1024):
    d, n = w.shape
    return pl.pallas_call(
        functools.partial(_ada_kernel, n_vec=n_vec),
        grid=(n // tn,),
        in_specs=[pl.BlockSpec((d, SUBLANES), lambda j: (0, 0)),
                  pl.BlockSpec((d, tn), lambda j: (0, j)),
                  pl.BlockSpec((1, tn), lambda j: (0, j))],
        out_specs=pl.BlockSpec((SUBLANES, tn), lambda j: (0, j)),
        out_shape=jax.ShapeDtypeStruct((SUBLANES, n), F32),
        compiler_params=_cparams(("arbitrary",)),
        name="ada",
    )(cond_t, w, b)


def _ctx_kernel(x_ref, g_ref, sh_ref, sc_ref, w_ref, o_ref):
    h = _rms_mod(x_ref[0], g_ref[...], sh_ref[0], sc_ref[0]).astype(BF16)
    o_ref[0] = _dot(h, w_ref[...]).astype(o_ref.dtype)


def _ctx_proj(x, g, shift, scale, w, n_cols, tm):
    bsz, n, d = x.shape
    vec = pl.BlockSpec((1, 1, d), lambda b, i: (b, 0, 0))
    return pl.pallas_call(
        _ctx_kernel,
        grid=(bsz, n // tm),
        in_specs=[pl.BlockSpec((1, tm, d), lambda b, i: (b, i, 0)),
                  pl.BlockSpec((1, d), lambda b, i: (0, 0)), vec, vec,
                  pl.BlockSpec((d, n_cols), lambda b, i: (0, 0))],
        out_specs=pl.BlockSpec((1, tm, n_cols), lambda b, i: (b, i, 0)),
        out_shape=jax.ShapeDtypeStruct((bsz, n, n_cols), BF16),
        compiler_params=_cparams(("parallel", "parallel")),
        name="ctx_proj",
    )(x, g, shift, scale, w)


PRO_CHUNKS = 8


def _next_block(b, i, ni, nb):
    t = jnp.minimum(b * ni + i + 1, nb * ni - 1)
    return t // ni, t % ni


def _in_pipe_kernel(x0_ref, rows0_ref, sh0_ref, sc0_ref, xn_ref, rowsn_ref, shn_ref, scn_ref,
                    cols_ref, g_ref, w_ref, o_ref, h_ref, *, tm):
    b, i, j = pl.program_id(0), pl.program_id(1), pl.program_id(2)
    par = (b * pl.num_programs(1) + i) % 2
    cr = tm // PRO_CHUNKS
    dh = cols_ref.shape[-1]

    @pl.when(jnp.logical_and(jnp.logical_and(b == 0, i == 0), j == 0))
    def _():
        x = x0_ref[0] + jnp.concatenate(_pos_block(rows0_ref[...], cols_ref[...], tm), axis=-1)
        h_ref[0] = _rms_mod(x, g_ref[...], sh0_ref[0], sc0_ref[0]).astype(BF16)

    o_ref[0] = _dot(h_ref[par], w_ref[...]).astype(o_ref.dtype)

    c = jnp.minimum(j, PRO_CHUNKS - 1)
    rows = pl.ds(pl.multiple_of(c * cr, cr), cr)
    gpc = cr // GRID_W
    rp = jnp.concatenate([jnp.broadcast_to(rowsn_ref[pl.ds(c * gpc + r, 1), :], (GRID_W, dh))
                          for r in range(gpc)], axis=0)
    cp = jnp.concatenate([cols_ref[...]] * gpc, axis=0)
    xc = xn_ref[0, rows, :] + jnp.concatenate([rp, cp], axis=-1)
    h_ref[1 - par, rows, :] = _rms_mod(xc, g_ref[...], shn_ref[0], scn_ref[0]).astype(BF16)


def _in_proj_latent(x, pos_tabs, g, shift, scale, w, tm, tn):
    bsz, n, d = x.shape
    n_cols = w.shape[1]
    rows_tab, cols_tab = pos_tabs
    dh = d // 2
    ni = n // tm
    nr = tm // GRID_W
    assert (tm // PRO_CHUNKS) % GRID_W == 0 and n_cols // tn >= PRO_CHUNKS
    nxt = lambda b, i: _next_block(b, i, ni, bsz)
    once = dict(pipeline_mode=pl.Buffered(1))
    return pl.pallas_call(
        functools.partial(_in_pipe_kernel, tm=tm),
        grid=(bsz, ni, n_cols // tn),
        in_specs=[
            pl.BlockSpec((1, tm, d), lambda b, i, j: (0, 0, 0), **once),
            pl.BlockSpec((nr, dh), lambda b, i, j: (0, 0)),
            pl.BlockSpec((1, 1, d), lambda b, i, j: (0, 0, 0)),
            pl.BlockSpec((1, 1, d), lambda b, i, j: (0, 0, 0)),
            pl.BlockSpec((1, tm, d), lambda b, i, j: nxt(b, i) + (0,)),
            pl.BlockSpec((nr, dh), lambda b, i, j: (nxt(b, i)[1], 0)),
            pl.BlockSpec((1, 1, d), lambda b, i, j: (nxt(b, i)[0], 0, 0)),
            pl.BlockSpec((1, 1, d), lambda b, i, j: (nxt(b, i)[0], 0, 0)),
            pl.BlockSpec((GRID_W, dh), lambda b, i, j: (0, 0)),
            pl.BlockSpec((1, d), lambda b, i, j: (0, 0)),
            pl.BlockSpec((d, tn), lambda b, i, j: (0, j)),
        ],
        out_specs=pl.BlockSpec((1, tm, tn), lambda b, i, j: (b, i, j)),
        out_shape=jax.ShapeDtypeStruct((bsz, n, n_cols), BF16),
        scratch_shapes=[pltpu.VMEM((2, tm, d), BF16)],
        compiler_params=_cparams(("arbitrary", "arbitrary", "arbitrary")),
        name="in_proj",
    )(x, rows_tab, shift, scale, x, rows_tab, shift, scale, cols_tab, g, w)


S5_CHUNKS = 8


def _s5_rows(k, tb, pitch):
    pad = pitch - tb
    if (k * pitch) % SUBLANES == 0:
        return k * pitch, k * pitch + tb
    return k * pitch - pad, k * pitch + tb + pad


def _s5_fill(buf, lhs_of, bd_ref, k, tb, pitch):
    lhs = lhs_of(k)
    lo, hi = _s5_rows(k, tb, pitch)
    if lo != k * pitch:
        z = jnp.zeros((k * pitch - lo, LANES), F32)
        lhs = jnp.concatenate([z, lhs.astype(F32), z], axis=0).astype(BF16)
    bu = _dot(lhs, bd_ref[0, k])
    for m in range(buf.shape[0]):
        buf[m, lo:hi, :] = bu[:, m * LANES:(m + 1) * LANES]


def _s5_pass(d, bu_scan, st_scan, carry_ref, st_mm, bu_mm, lhs_of, y_ref, bd_ref, cd_ref,
             lam_r, lam_i, tb, pitch):
    nslab = bu_scan.shape[0]
    half = nslab // 2
    per = tb // S5_CHUNKS
    sr = [carry_ref[m] for m in range(half)]
    si = [carry_ref[half + m] for m in range(half)]

    def scan_steps(q0, q1):
        for q in range(q0, q1):
            idx = pl.ds(jnp.where(d == 0, q, tb - 1 - q), SUBLANES, stride=pitch)
            for m in range(half):
                nr = lam_r[m] * sr[m] - lam_i[m] * si[m] + bu_scan[m, idx, :]
                ni = lam_r[m] * si[m] + lam_i[m] * sr[m] + bu_scan[half + m, idx, :]
                st_scan[m, idx, :] = nr
                st_scan[half + m, idx, :] = ni
                sr[m], si[m] = nr, ni

    for k in range(S5_CHUNKS):
        lo, hi = _s5_rows(k, tb, pitch)
        sk = jnp.concatenate([st_mm[m, lo:hi, :] for m in range(nslab)], axis=-1)
        yk = _dot(sk.astype(BF16), cd_ref[0, k])
        y_ref[0, :, k * LANES:(k + 1) * LANES] = yk[k * pitch - lo:k * pitch - lo + tb].astype(
            y_ref.dtype)
        scan_steps(k * per, k * per + per // 2)
        _s5_fill(bu_mm, lhs_of, bd_ref, k, tb, pitch)
        scan_steps(k * per + per // 2, (k + 1) * per)
    for m in range(half):
        carry_ref[m] = sr[m]
        carry_ref[half + m] = si[m]


def _s5_kernel(uc0_ref, uca_ref, uxa_ref, ucb_ref, uxb_ref, bd_ref, cd_ref, lr_ref, li_ref,
               ya_ref, yb_ref, a_bu, a_st, b_bu, b_st, ca_ref, cb_ref, *, tb, pitch, nc, nblk):
    d = pl.program_id(0)
    i = pl.program_id(1)
    half = a_bu.shape[0] // 2
    lam_r = [lr_ref[0, m] for m in range(half)]
    lam_i = [li_ref[0, m] for m in range(half)]

    @pl.when(i == 0)
    def _():
        ca_ref[...] = jnp.zeros_like(ca_ref)
        cb_ref[...] = jnp.zeros_like(cb_ref)
        a_st[...] = jnp.zeros_like(a_st)
        b_st[...] = jnp.zeros_like(b_st)
        for k in range(S5_CHUNKS):
            _s5_fill(a_bu, lambda k: uc0_ref[0, :, k * LANES:(k + 1) * LANES], bd_ref, k, tb, pitch)

    def lhs(uc_ref, ux_ref, pos):
        def of(k):
            cols = slice(k * LANES, (k + 1) * LANES)
            return jnp.where(pos < nc, uc_ref[0, :, cols], ux_ref[0, :, cols])
        return of

    _s5_pass(d, a_bu, a_st, ca_ref, b_st, b_bu, lhs(ucb_ref, uxb_ref, i), yb_ref,
             bd_ref, cd_ref, lam_r, lam_i, tb, pitch)

    @pl.when(i < nblk)
    def _():
        _s5_pass(d, b_bu, b_st, cb_ref, a_st, a_bu, lhs(uca_ref, uxa_ref, i + 1), ya_ref,
                 bd_ref, cd_ref, lam_r, lam_i, tb, pitch)


def _s5(u_c, p_x, bd, cd, lam_r, lam_i, s5w, tb):
    bsz, lc, _ = u_c.shape
    assert bsz == 2, "the two batch elements are the two interleaved sequences"
    n = p_x.shape[1]
    nc, nb = lc // tb, n // tb
    nblk = nc + nb
    nslab = bd.shape[-1] // LANES
    pitch = tb + SUBLANES // 2

    def blk(d, pos, count):
        j = jnp.clip(pos, 0, count - 1)
        return jnp.where(d == 0, j, count - 1 - j)

    def u_spec(b, off, ctx):
        if ctx:
            return pl.BlockSpec((1, tb, s5w), lambda d, i: (b, blk(d, i + off, nc), 0))
        return pl.BlockSpec((1, tb, s5w), lambda d, i: (b, blk(d, i + off - nc, nb), 0))

    def y_spec(off):
        return pl.BlockSpec((1, tb, s5w), lambda d, i: (d, blk(d, i + off - nc, nb), 0))

    par = lambda d, i: (d, 0, 0, 0)
    buf = pltpu.VMEM((nslab, S5_CHUNKS * pitch, LANES), F32)
    state = pltpu.VMEM((nslab, SUBLANES, LANES), F32)
    return pl.pallas_call(
        functools.partial(_s5_kernel, tb=tb, pitch=pitch, nc=nc, nblk=nblk),
        grid=(2, nblk + 1),
        in_specs=[
            pl.BlockSpec((1, tb, s5w), lambda d, i: (0, blk(d, 0, nc), 0)),
            u_spec(0, 1, True), u_spec(0, 1, False),
            u_spec(1, 0, True), u_spec(1, 0, False),
            pl.BlockSpec((1, S5_CHUNKS, LANES, nslab * LANES), par, pipeline_mode=pl.Buffered(1)),
            pl.BlockSpec((1, S5_CHUNKS, nslab * LANES, LANES), par, pipeline_mode=pl.Buffered(1)),
            pl.BlockSpec((1, nslab // 2, SUBLANES, LANES), par),
            pl.BlockSpec((1, nslab // 2, SUBLANES, LANES), par),
        ],
        out_specs=[y_spec(0), y_spec(-1)],
        out_shape=[jax.ShapeDtypeStruct((2, n, s5w), BF16)] * 2,
        scratch_shapes=[buf, buf, buf, buf, state, state],
        compiler_params=_cparams(("arbitrary", "arbitrary")),
        name="s5",
    )(u_c, u_c, p_x, u_c, p_x, bd, cd, lam_r, lam_i)


def _s5_params(a_re, a_im, log_dt, b_re, b_im, c_re, c_im):
    g, p, h = b_re.shape
    gl = g // S5_CHUNKS
    dt = jnp.exp(log_dt.astype(F32))[:, None]
    mag = jnp.exp(a_re.astype(F32) * dt)
    lr, li = mag * jnp.cos(a_im.astype(F32) * dt), mag * jnp.sin(a_im.astype(F32) * dt)
    den = a_re * a_re + a_im * a_im
    qr = ((lr - 1.0) * a_re + li * a_im) / den
    qi = (li * a_re - (lr - 1.0) * a_im) / den
    bbr = qr[:, :, None] * b_re - qi[:, :, None] * b_im
    bbi = qr[:, :, None] * b_im + qi[:, :, None] * b_re
    eye = jnp.eye(gl, dtype=F32)

    def pack_b(m):
        return jnp.einsum("kgph,gq->kghqp", m.reshape(S5_CHUNKS, gl, p, h), eye).reshape(
            S5_CHUNKS, gl * h, gl * p)

    def pack_c(m):
        return jnp.einsum("kghp,gq->kgpqh", m.reshape(S5_CHUNKS, gl, h, p), eye).reshape(
            S5_CHUNKS, gl * p, gl * h)

    bd = jnp.concatenate([pack_b(bbr), pack_b(bbi)], axis=-1).astype(BF16)
    cd = jnp.concatenate([pack_c(c_re.astype(F32)), -pack_c(c_im.astype(F32))], axis=1).astype(BF16)

    def slabs(v):
        return v.reshape(S5_CHUNKS, -1, LANES).transpose(1, 0, 2)

    return bd, cd, slabs(lr), slabs(li)


def _filt_kernel(f_ref, w1_ref, b1_ref, q1_ref, w2_ref, b2_ref, q2_ref, w3_ref, dec_ref,
                 bias_ref, o_ref, *, tq):
    f = f_ref[...]
    h = jnp.sin(q1_ref[...] * (_dot_hi(f, w1_ref[...]) + b1_ref[...]))
    h = jnp.sin(q2_ref[...] * (_dot_hi(h, w2_ref[...]) + b2_ref[...]))
    h = _dot_3pass(h, w3_ref[...])
    c = bias_ref.shape[-1]
    t = f[:, 0:1]
    lag = pl.program_id(0) * tq + lax.broadcasted_iota(jnp.int32, (tq, 1), 0)
    fwd = h[:, :c] * jnp.exp(-t * jnp.abs(dec_ref[0:1, :]))
    bwd = h[:, c:] * jnp.exp(-t * jnp.abs(dec_ref[1:2, :]))
    fwd = jnp.where(lag == 0, fwd + bias_ref[...], fwd)
    bwd = jnp.where(lag == 0, 0.0, bwd)
    _store_padded(o_ref, 0, fwd)
    _store_padded(o_ref, 1, bwd)


def _filter_taps(feats, w1, b1, q1, w2, b2, q2, w3, decay, bias, tq=512):
    n_lat, fp = feats.shape
    hid = w2.shape[0]
    c = bias.shape[-1]
    rows = tq // DFT_N2 * GROUP_PITCH
    const = lambda r: (0, 0)
    return pl.pallas_call(
        functools.partial(_filt_kernel, tq=tq),
        grid=(n_lat // tq,),
        in_specs=[pl.BlockSpec((tq, fp), lambda r: (r, 0)),
                  pl.BlockSpec((fp, hid), const), pl.BlockSpec((1, hid), const),
                  pl.BlockSpec((1, hid), const),
                  pl.BlockSpec((hid, hid), const), pl.BlockSpec((1, hid), const),
                  pl.BlockSpec((1, hid), const),
                  pl.BlockSpec((hid, 2 * c), const),
                  pl.BlockSpec((2, c), const),
                  pl.BlockSpec((1, c), const)],
        out_specs=pl.BlockSpec((2, rows, c), lambda r: (0, r, 0)),
        out_shape=jax.ShapeDtypeStruct((2, n_lat // DFT_N2 * GROUP_PITCH, c), F32),
        compiler_params=_cparams(("parallel",)),
        name="hy_filter",
    )(feats, w1, b1, q1, w2, b2, q2, w3, decay, bias)


def _conv3(cur_ref, prev_ref, next_ref, w_ref, b_ref, i, last, tm):
    row = lax.broadcasted_iota(jnp.int32, (tm, 1), 0)
    cur = cur_ref[0].astype(F32)
    halo = prev_ref.shape[1]
    prev_row = jnp.where(i == 0, 0.0, prev_ref[0, halo - 1:halo, :].astype(F32))
    next_row = jnp.where(i == last, 0.0, next_ref[0, 0:1, :].astype(F32))
    up = jnp.where(row == 0, prev_row, pltpu.roll(cur, 1, 0))
    dn = jnp.where(row == tm - 1, next_row, pltpu.roll(cur, tm - 1, 0))
    w = w_ref[...]
    return up * w[0:1] + cur * w[1:2] + dn * w[2:3] + b_ref[...]


CONV_HALO = 16


def _conv3_specs(tm, c, n, col_block):
    hb = tm // CONV_HALO
    nhb = n // CONV_HALO
    return [pl.BlockSpec((1, tm, c), lambda b, i: (b, i, col_block)),
            pl.BlockSpec((1, CONV_HALO, c),
                         lambda b, i: (b, jnp.maximum(i * hb - 1, 0), col_block)),
            pl.BlockSpec((1, CONV_HALO, c),
                         lambda b, i: (b, jnp.minimum((i + 1) * hb, nhb - 1), col_block))]


def _sconv_kernel(x1_ref, x1p_ref, x1n_ref, v_ref, vp_ref, vn_ref, w1_ref, wv_ref, b1_ref, bv_ref,
                  u_ref, *, tm):
    i = pl.program_id(1)
    last = pl.num_programs(1) - 1
    x1 = _conv3(x1_ref, x1p_ref, x1n_ref, w1_ref, b1_ref, i, last, tm)
    v = _conv3(v_ref, vp_ref, vn_ref, wv_ref, bv_ref, i, last, tm)
    _store_padded(u_ref, 0, x1 * v)


def _short_conv(p_x, conv_w, conv_b, col0, c, tm):
    bsz, n, _ = p_x.shape
    cb0 = col0 // c
    part = lambda p, shape: pl.BlockSpec(shape, lambda b, i: (0, p))
    return pl.pallas_call(
        functools.partial(_sconv_kernel, tm=tm),
        grid=(bsz, n // tm),
        in_specs=(_conv3_specs(tm, c, n, cb0 + 1) + _conv3_specs(tm, c, n, cb0 + 2)
                  + [part(1, (3, c)), part(2, (3, c)), part(1, (1, c)), part(2, (1, c))]),
        out_specs=pl.BlockSpec((1, tm // DFT_N2 * GROUP_PITCH, c), lambda b, i: (b, i, 0)),
        out_shape=jax.ShapeDtypeStruct((bsz, n // DFT_N2 * GROUP_PITCH, c), F32),
        compiler_params=_cparams(("parallel", "parallel")),
        name="short_conv",
    )(p_x, p_x, p_x, p_x, p_x, p_x, conv_w, conv_w, conv_b, conv_b)


def _dft_tables(n1):
    n = n1 * DFT_N2
    k1 = jnp.arange(n1, dtype=jnp.int32)
    n2 = jnp.arange(DFT_N2, dtype=jnp.int32)

    def unit(ph, period):
        ang = (ph % period).astype(F32) * F32(2.0 * math.pi / period)
        return jnp.cos(ang), -jnp.sin(ang)

    ar, ai = unit(k1[:, None] * k1[None, :], n1)
    tr, ti = unit(n2[:, None] * k1[None, :], n)
    gr = ar[None] * tr[:, :, None] - ai[None] * ti[:, :, None]
    gi = ar[None] * ti[:, :, None] + ai[None] * tr[:, :, None]
    h = n1 // 2
    g1c = jnp.concatenate([jnp.concatenate([gr[:, :, :h], -gi[:, :, :h]], axis=2),
                           jnp.concatenate([gi[:, :, :h], gr[:, :, :h]], axis=2)], axis=1)
    m = jnp.arange(h, dtype=jnp.int32)

    def fold(g):
        up = jnp.take(g, n1 - 1 - m, axis=2)
        up0 = jnp.take(g[0], (n1 - m) % n1, axis=1) * (m != 0).astype(F32)
        return jnp.concatenate([g[:, :, :h], up.at[0].set(up0)], axis=2)

    g1f = jnp.concatenate([fold(gr), fold(gi)], axis=1)
    fr, fi = unit(n2[:, None] * n2[None, :], DFT_N2)
    f2 = jnp.concatenate([jnp.concatenate([fr, -fi], axis=1),
                          jnp.concatenate([fi, fr], axis=1)], axis=0)
    f2c = jnp.concatenate([jnp.concatenate([fr, fi], axis=1),
                           jnp.concatenate([-fi, fr], axis=1)], axis=0) / n
    return g1c.astype(BF16), g1f.astype(BF16), f2.astype(BF16), f2c.astype(BF16)


def _stage1(gather, g_ref, a_ref, cn, n1, pitch):
    def trip(chunk, carry):
        for q in range(cn):
            n2 = chunk * cn + q
            a = _dot(g_ref[n2], gather(n2).astype(BF16))
            a_ref[0, pl.ds(n2, n1, stride=pitch), :] = a[:n1]
            a_ref[1, pl.ds(n2, n1, stride=pitch), :] = a[n1:]
        return carry
    lax.fori_loop(0, DFT_N2 // cn, trip, 0)


def _a_rows(chunk, ck, q, pitch):
    return pl.ds(pl.multiple_of(chunk * (ck * pitch), SUBLANES) + q * pitch, DFT_N2)


def _load_a_chunk(a_ref, chunk, ck, pitch):
    tiles = []
    for q in range(ck):
        rows = _a_rows(chunk, ck, q, pitch)
        tiles.append(jnp.concatenate([a_ref[0, rows, :], a_ref[1, rows, :]], axis=0).astype(BF16))
    return jnp.concatenate(tiles, axis=1)


def _spec_kernel(t_ref, g_ref, f2_ref, o_ref, a_ref, *, n1, cn, ck, pitch):
    s = pl.program_id(1)
    h = n1 // 2

    @pl.when(s == 0)
    def _():
        def gather(n2):
            back = (DFT_N2 - n2) % DFT_N2
            return jnp.concatenate([t_ref[0, pl.ds(n2, h, stride=GROUP_PITCH), :],
                                    t_ref[1, pl.ds(back, h, stride=GROUP_PITCH), :]], axis=0)
        _stage1(gather, g_ref, a_ref, cn, n1, pitch)

    @pl.when(s >= 1)
    def _():
        o_ref[0] = _dot(f2_ref[...], _load_a_chunk(a_ref, s - 1, ck, pitch)).astype(o_ref.dtype)


def _conv_kernel(u_ref, g1_ref, kf_ref, f2_ref, f2c_ref, y_ref, a_ref,
                 *, n1, cn, ck, pitch, np2):
    s = pl.program_id(1)
    h = n1 // 2

    @pl.when(s == 0)
    def _():
        y_ref[...] = jnp.zeros_like(y_ref)

        def gather(n2):
            idx = pl.ds(n2, h, stride=GROUP_PITCH)
            return jnp.concatenate([u_ref[0, idx, :], u_ref[1, idx, :]], axis=0)
        _stage1(gather, g1_ref, a_ref, cn, n1, pitch)

    @pl.when(jnp.logical_and(s >= 1, s <= np2))
    def _():
        chunk = s - 1
        x = _dot(f2_ref[...], _load_a_chunk(a_ref, chunk, ck, pitch))
        xr, xi = x[:DFT_N2], x[DFT_N2:]
        kr, ki = kf_ref[0, :DFT_N2, :].astype(F32), kf_ref[0, DFT_N2:, :].astype(F32)
        y = jnp.concatenate([xr * kr - xi * ki, xr * ki + xi * kr], axis=0)
        b = _dot(f2c_ref[...], y.astype(BF16))
        for q in range(ck):
            rows = _a_rows(chunk, ck, q, pitch)
            a_ref[0, rows, :] = b[:DFT_N2, q * LANES:(q + 1) * LANES]
            a_ref[1, rows, :] = b[DFT_N2:, q * LANES:(q + 1) * LANES]

    @pl.when(s > np2)
    def _():
        def trip(chunk, carry):
            for q in range(cn):
                n2 = chunk * cn + q
                idx = pl.ds(n2, n1, stride=pitch)
                b = jnp.concatenate([a_ref[0, idx, :], a_ref[1, idx, :]], axis=0)
                y = lax.dot_general(g1_ref[n2], b.astype(BF16), (((0,), (0,)), ((), ())),
                                    preferred_element_type=F32)
                out = pl.ds(n2, h, stride=GROUP_PITCH)
                y_ref[0, out, :] = y[:h]
                y_ref[1, out, :] = y[h:]
            return carry
        lax.fori_loop(0, DFT_N2 // cn, trip, 0)


DFT_UNROLL = 32
A_PITCH = DFT_N2 + SUBLANES


def _filter_spectrum(taps, g1f, f2):
    _, rows, c = taps.shape
    n1 = 2 * rows // GROUP_PITCH
    ck = min(32, n1)
    np2 = n1 // ck
    pitch = A_PITCH
    return pl.pallas_call(
        functools.partial(_spec_kernel, n1=n1, cn=DFT_UNROLL, ck=ck, pitch=pitch),
        grid=(c // LANES, 1 + np2),
        in_specs=[pl.BlockSpec((2, rows, LANES), lambda j, s: (0, 0, j),
                               pipeline_mode=pl.Buffered(1)),
                  pl.BlockSpec((DFT_N2, 2 * n1, n1), lambda j, s: (0, 0, 0),
                               pipeline_mode=pl.Buffered(1)),
                  pl.BlockSpec((2 * DFT_N2, 2 * DFT_N2), lambda j, s: (0, 0))],
        out_specs=pl.BlockSpec((1, 2 * DFT_N2, ck * LANES),
                               lambda j, s: (j, 0, jnp.maximum(s - 1, 0))),
        out_shape=jax.ShapeDtypeStruct((c // LANES, 2 * DFT_N2, n1 * LANES), BF16),
        scratch_shapes=[pltpu.VMEM((2, n1 * pitch, LANES), F32)],
        compiler_params=_cparams(("parallel", "arbitrary")),
        name="hy_spectrum",
    )(taps, g1f, f2)


def _long_conv(u, kf, g1c, f2, f2c):
    bsz, rows, c = u.shape
    assert bsz == 2, "the two batch elements are packed as one complex signal"
    n1 = 2 * rows // GROUP_PITCH
    ck = min(16, n1)
    np2 = n1 // ck
    pitch = A_PITCH
    once = dict(pipeline_mode=pl.Buffered(1))
    return pl.pallas_call(
        functools.partial(_conv_kernel, n1=n1, cn=DFT_UNROLL, ck=ck, pitch=pitch, np2=np2),
        grid=(c // LANES, np2 + 2),
        in_specs=[
            pl.BlockSpec((2, rows, LANES), lambda j, s: (0, 0, j), **once),
            pl.BlockSpec((DFT_N2, 2 * n1, n1), lambda j, s: (0, 0, 0), **once),
            pl.BlockSpec((1, 2 * DFT_N2, ck * LANES),
                         lambda j, s: (j, 0, jnp.clip(s - 1, 0, np2 - 1))),
            pl.BlockSpec((2 * DFT_N2, 2 * DFT_N2), lambda j, s: (0, 0)),
            pl.BlockSpec((2 * DFT_N2, 2 * DFT_N2), lambda j, s: (0, 0)),
        ],
        out_specs=pl.BlockSpec((2, rows, LANES), lambda j, s: (0, 0, j), **once),
        out_shape=jax.ShapeDtypeStruct((2, rows, c), F32),
        scratch_shapes=[pltpu.VMEM((2, n1 * pitch, LANES), F32)],
        compiler_params=_cparams(("parallel", "arbitrary")),
        name="hy_conv",
    )(u, g1c, kf, f2, f2c)


def _mixer_kernel(ya_ref, yb_ref, u_ref, h0_ref, h0p_ref, h0n_ref, cw_ref, cb_ref, yc_ref, gs_ref,
                  gh_ref, x_ref, rows_ref, cols_ref, d_ref, gate_ref, wglu_ref, why_ref, wout_ref,
                  o_ref, *, tm, cw):
    first = pl.program_id(0) == 0
    y = (jnp.where(first, ya_ref[0], yb_ref[0]).astype(F32)
         + jnp.where(first, ya_ref[1], yb_ref[1]).astype(F32)
         + u_ref[0].astype(F32) * d_ref[...])
    h = jax.nn.gelu(y, approximate=True).astype(BF16)
    y_conv = jnp.concatenate(
        [yc_ref[0, g * GROUP_PITCH:g * GROUP_PITCH + DFT_N2, :] for g in range(tm // DFT_N2)],
        axis=0)
    x0 = _conv3(h0_ref, h0p_ref, h0n_ref, cw_ref, cb_ref, pl.program_id(1),
                pl.num_programs(1) - 1, tm)
    hx = (x0 * y_conv).astype(BF16)
    d = o_ref.shape[-1]
    merged = []
    for c in range(d // cw):
        cols = slice(c * cw, (c + 1) * cw)
        glu_v = _dot(h, wglu_ref[:, cols])
        glu_g = _dot(h, wglu_ref[:, d + c * cw:d + (c + 1) * cw])
        branch_hy = _dot(hx, why_ref[:, cols])
        m = (jax.nn.sigmoid(gs_ref[0, :, cols].astype(F32)) * (glu_v * jax.nn.sigmoid(glu_g))
             + jax.nn.sigmoid(gh_ref[0, :, cols].astype(F32)) * branch_hy)
        merged.append(m.astype(BF16))
    merged = jnp.concatenate(merged, axis=-1)
    dh = d // 2
    rp, cp = _pos_block(rows_ref[0], cols_ref[...], tm)
    for c in range(d // cw):
        cols = slice(c * cw, (c + 1) * cw)
        pos = rp[:, cols] if (c + 1) * cw <= dh else cp[:, c * cw - dh:(c + 1) * cw - dh]
        o_ref[0, :, cols] = (x_ref[0, :, cols] + pos
                             + gate_ref[0, :, cols] * _dot(merged, wout_ref[:, cols]))


def _mixer(y_ssm, p_x, conv_w, conv_b, y_conv, x, pos_tabs, s5_d, gate, w_glu, w_hy_out, w_out,
           s5w, gate_col0, tm):
    bsz, n, d = x.shape
    assert bsz == 2, "the S5 readouts arrive as one array per batch element"
    hyw = w_hy_out.shape[0]
    rows_tab, cols_tab = pos_tabs
    nr = tm // GRID_W
    dh = d // 2
    ni = n // tm
    g0 = gate_col0 // d
    cw = min(512, dh)
    resident = dict(pipeline_mode=pl.Buffered(1))
    tok = lambda width, col: pl.BlockSpec((1, tm, width), lambda b, i: (b, i, col))
    return pl.pallas_call(
        functools.partial(_mixer_kernel, tm=tm, cw=cw),
        grid=(bsz, ni),
        in_specs=[
            pl.BlockSpec((2, tm, s5w), lambda b, i: (0, jnp.where(b == 0, i, ni - 1), 0)),
            pl.BlockSpec((2, tm, s5w), lambda b, i: (0, jnp.where(b == 0, 0, i), 0)),
            tok(s5w, 0),
            *_conv3_specs(tm, hyw, n, s5w // hyw),
            pl.BlockSpec((3, hyw), lambda b, i: (0, 0)),
            pl.BlockSpec((1, hyw), lambda b, i: (0, 0)),
            pl.BlockSpec((1, tm // DFT_N2 * GROUP_PITCH, hyw), lambda b, i: (b, i, 0)),
            tok(d, g0), tok(d, g0 + 1), tok(d, 0),
            pl.BlockSpec((1, nr, dh), lambda b, i: (i, 0, 0)),
            pl.BlockSpec((GRID_W, dh), lambda b, i: (0, 0)),
            pl.BlockSpec((1, s5w), lambda b, i: (0, 0)),
            pl.BlockSpec((1, 1, d), lambda b, i: (b, 0, 0)),
            pl.BlockSpec((s5w, 2 * d), lambda b, i: (0, 0), **resident),
            pl.BlockSpec((hyw, d), lambda b, i: (0, 0), **resident),
            pl.BlockSpec((d, d), lambda b, i: (0, 0), **resident),
        ],
        out_specs=pl.BlockSpec((1, tm, d), lambda b, i: (b, i, 0)),
        out_shape=jax.ShapeDtypeStruct((bsz, n, d), F32),
        compiler_params=_cparams(("arbitrary", "arbitrary")),
        name="mixer",
    )(y_ssm[0], y_ssm[1], p_x, p_x, p_x, p_x, conv_w, conv_b, y_conv, p_x, p_x, x,
      rows_tab.reshape(-1, nr, dh), cols_tab, s5_d, gate, w_glu, w_hy_out, w_out)


def _ffn_kernel(x_ref, g_ref, sh_ref, sc_ref, gate_ref, wa_ref, wb_ref, wo_ref, nf_ref,
                o_ref, h_ref, acc_ref):
    j = pl.program_id(2)

    @pl.when(j == 0)
    def _():
        h_ref[...] = _rms_mod(x_ref[0], g_ref[...], sh_ref[0], sc_ref[0]).astype(BF16)
        acc_ref[...] = jnp.zeros_like(acc_ref)

    h = h_ref[...]
    act = _silu(_dot(h, wa_ref[...])) * _dot(h, wb_ref[...])
    acc_ref[...] += _dot(act.astype(BF16), wo_ref[...])

    @pl.when(j == pl.num_programs(2) - 1)
    def _():
        xo = x_ref[0] + gate_ref[0] * acc_ref[...]
        o_ref[0] = xo * lax.rsqrt(jnp.mean(xo * xo, axis=-1, keepdims=True) + EPS) * nf_ref[...]


def _ffn(x, g, shift, scale, gate, w_in, w_out, norm_f, tm, tf):
    bsz, n, d = x.shape
    dff = w_out.shape[0]
    nj = dff // tf
    vec = pl.BlockSpec((1, 1, d), lambda b, i, j: (b, 0, 0))
    return pl.pallas_call(
        _ffn_kernel,
        grid=(bsz, n // tm, nj),
        in_specs=[pl.BlockSpec((1, tm, d), lambda b, i, j: (b, i, 0)),
                  pl.BlockSpec((1, d), lambda b, i, j: (0, 0)),
                  vec, vec, vec,
                  pl.BlockSpec((d, tf), lambda b, i, j: (0, j)),
                  pl.BlockSpec((d, tf), lambda b, i, j: (0, nj + j)),
                  pl.BlockSpec((tf, d), lambda b, i, j: (j, 0)),
                  pl.BlockSpec((1, d), lambda b, i, j: (0, 0))],
        out_specs=pl.BlockSpec((1, tm, d), lambda b, i, j: (b, i, 0)),
        out_shape=jax.ShapeDtypeStruct((bsz, n, d), F32),
        scratch_shapes=[pltpu.VMEM((tm, d), BF16), pltpu.VMEM((tm, d), F32)],
        compiler_params=_cparams(("parallel", "parallel", "arbitrary")),
        name="ffn",
    )(x, g, shift, scale, gate, w_in, w_in, w_out, norm_f)


def _pos_tables(n_rows, d):
    quarter = d // 4
    omega = 10000.0 ** (-jnp.arange(quarter, dtype=F32) / quarter)
    ar = jnp.arange(n_rows, dtype=F32)[:, None] * omega
    ac = jnp.arange(GRID_W, dtype=F32)[:, None] * omega
    return (jnp.concatenate([jnp.sin(ar), jnp.cos(ar)], axis=-1),
            jnp.concatenate([jnp.sin(ac), jnp.cos(ac)], axis=-1))


def _filter_features(n_lat, width):
    pos = jnp.arange(n_lat, dtype=F32)
    t = pos / float(max(n_lat - 1, 1))
    w = 2.0 * math.pi * pos / n_lat
    bands = jnp.linspace(1e-4, HY_BANDS - 1, HY_BANDS, dtype=F32)
    feats = jnp.concatenate([t[:, None], jnp.cos(w[:, None] * bands), -jnp.sin(w[:, None] * bands)],
                            axis=-1)
    return jnp.pad(feats, ((0, 0), (0, width - feats.shape[1])))


def kernel(x, c, ctx, c_ctx, w_ada, b_ada, norm_mix, w_in, s5_a_re, s5_a_im, s5_log_dt,
           s5_b_re, s5_b_im, s5_c_re, s5_c_im, s5_d, w_glu, hy_conv_w, hy_conv_b,
           hy_f1_w, hy_f1_b, hy_f1_freq, hy_f2_w, hy_f2_b, hy_f2_freq, hy_f3_w, hy_decay,
           hy_bias, w_hy_out, w_out, norm_ffn, w_ffn_in, w_ffn_out, norm_f):
    bsz, n_lat, d = x.shape
    depth = w_ada.shape[0]
    assert depth == 1, "the context stream is only advanced for the single-layer trunk"
    l = 0
    s5w = s5_d.shape[-1]
    hyw = hy_bias.shape[-1]
    n_ctx = ctx.shape[1]
    tm = 512
    tn = min(1024, d)
    tb = 256
    assert n_ctx % tb == 0 and n_lat % tm == 0 and s5w == S5_CHUNKS * LANES

    pos_tabs = _pos_tables(n_lat // GRID_W, d)

    cond_t = jnp.zeros((d, SUBLANES), F32).at[:, :bsz].set(c.T).at[:, bsz].set(c_ctx)
    ada = _ada(cond_t, w_ada[l], b_ada[l][None], bsz + 1, tn=min(1024, d))

    def vec(row0, rows, part):
        v = ada[row0:row0 + rows, part * d:(part + 1) * d]
        return jnp.broadcast_to(v, (bsz, d))[:, None, :]

    shift_mix, scale_mix, gate_mix = vec(0, bsz, 0), vec(0, bsz, 1), vec(0, bsz, 2)
    shift_ffn, scale_ffn, gate_ffn = vec(0, bsz, 3), vec(0, bsz, 4), vec(0, bsz, 5)
    cshift_mix, cscale_mix = vec(bsz, 1, 0), vec(bsz, 1, 1)

    w_in_b = w_in[l].astype(BF16)
    g_mix = norm_mix[l][None]
    u_c = _ctx_proj(ctx, g_mix, cshift_mix, cscale_mix, w_in_b, s5w, tb)
    p_x = _in_proj_latent(x, pos_tabs, g_mix, shift_mix, scale_mix, w_in_b, min(1024, n_lat), tn)

    packed = [_s5_params(s5_a_re[l, k], s5_a_im[l, k], s5_log_dt[l, k], s5_b_re[l, k],
                         s5_b_im[l, k], s5_c_re[l, k], s5_c_im[l, k]) for k in range(2)]
    bd, cd, lam_r, lam_i = (jnp.stack(t) for t in zip(*packed))
    y_ssm = _s5(u_c, p_x, bd, cd, lam_r, lam_i, s5w, tb)

    fp = 64
    feats = _filter_features(n_lat, fp)
    w1 = jnp.pad(hy_f1_w[l], ((0, fp - hy_f1_w.shape[1]), (0, 0)))
    taps = _filter_taps(feats, w1, hy_f1_b[l][None], hy_f1_freq[l][None], hy_f2_w[l],
                        hy_f2_b[l][None], hy_f2_freq[l][None], hy_f3_w[l], hy_decay[l],
                        hy_bias[l][None])
    g1c, g1f, f2, f2c = _dft_tables(2 * n_lat // DFT_N2)
    kf = _filter_spectrum(taps, g1f, f2)
    u_hy = _short_conv(p_x, hy_conv_w[l], hy_conv_b[l][None], s5w, hyw, tm)
    y_conv = _long_conv(u_hy, kf, g1c, f2, f2c)

    x1 = _mixer(y_ssm, p_x, hy_conv_w[l], hy_conv_b[l][None], y_conv, x, pos_tabs, s5_d[l][None],
                gate_mix,
                w_glu[l].astype(BF16), w_hy_out[l].astype(BF16), w_out[l].astype(BF16),
                s5w, s5w + 3 * hyw, 256)

    return _ffn(x1, norm_ffn[l][None], shift_ffn, scale_ffn, gate_ffn,
                w_ffn_in[l].astype(BF16), w_ffn_out[l].astype(BF16), norm_f[None], tm, 512)
```

```python
import functools
import math

import jax
import jax.numpy as jnp
from jax import lax
from jax.experimental import pallas as pl
from jax.experimental.pallas import tpu as pltpu

F32 = jnp.float32
BF16 = jnp.bfloat16
HIGHEST = lax.Precision.HIGHEST

GRID_W = 64
N_ADA = 6
EPS = 1e-6
HY_BANDS = 16
LANES = 128
SUBLANES = 8
DFT_N2 = LANES
GROUP_PITCH = DFT_N2 + SUBLANES // 2
VMEM_LIMIT = 60 * 1024 * 1024


def _cparams(sem):
    return pltpu.CompilerParams(dimension_semantics=sem, vmem_limit_bytes=VMEM_LIMIT)


def _dot(a, b):
    return jnp.dot(a, b, preferred_element_type=F32)


def _dot_hi(a, b):
    return jnp.dot(a, b, preferred_element_type=F32, precision=HIGHEST)


def _dot_3pass(a, b):
    a_hi, b_hi = a.astype(BF16), b.astype(BF16)
    a_lo = (a - a_hi.astype(F32)).astype(BF16)
    b_lo = (b - b_hi.astype(F32)).astype(BF16)
    return _dot(a_hi, b_hi) + (_dot(a_hi, b_lo) + _dot(a_lo, b_hi))


def _silu(x):
    return x * jax.nn.sigmoid(x)


def _rms_mod(x, g, shift, scale):
    y = x * lax.rsqrt(jnp.mean(x * x, axis=-1, keepdims=True) + EPS)
    return (y * g) * (1.0 + scale) + shift


def _pos_block(rows, cols, tm):
    nr = tm // GRID_W
    dh = rows.shape[-1]
    rp = jnp.broadcast_to(rows[:, None, :], (nr, GRID_W, dh)).reshape(tm, dh)
    cp = jnp.broadcast_to(cols[None], (nr, GRID_W, dh)).reshape(tm, dh)
    return rp, cp


def _store_padded(ref, lead, val):
    pad = jnp.zeros((GROUP_PITCH - DFT_N2, val.shape[-1]), val.dtype)
    for g in range(val.shape[0] // DFT_N2):
        ref[lead, g * GROUP_PITCH:g * GROUP_PITCH + DFT_N2, :] = val[g * DFT_N2:(g + 1) * DFT_N2]
        ref[lead, g * GROUP_PITCH + DFT_N2:(g + 1) * GROUP_PITCH, :] = pad


def _ada_kernel(ct_ref, w_ref, b_ref, o_ref, *, n_vec):
    sc = _silu(ct_ref[...])
    w = w_ref[...]
    rows = [jnp.sum(w * sc[:, r:r + 1], axis=0, keepdims=True) for r in range(n_vec)]
    rows += [jnp.zeros_like(rows[0])] * (SUBLANES - n_vec)
    o_ref[...] = jnp.concatenate(rows, axis=0) + b_ref[...]


def _ada(cond_t, w, b, n_vec, tn=1024):
    d, n = w.shape
    return pl.pallas_call(
        functools.partial(_ada_kernel, n_vec=n_vec),
        grid=(n // tn,),
        in_specs=[pl.BlockSpec((d, SUBLANES), lambda j: (0, 0)),
                  pl.BlockSpec((d, tn), lambda j: (0, j)),
                  pl.BlockSpec((1, tn), lambda j: (0, j))],
        out_specs=pl.BlockSpec((SUBLANES, tn), lambda j: (0, j)),
        out_shape=jax.ShapeDtypeStruct((SUBLANES, n), F32),
        compiler_params=_cparams(("arbitrary",)),
        name="ada",
    )(cond_t, w, b)


def _ctx_kernel(x_ref, g_ref, sh_ref, sc_ref, w_ref, o_ref):
    h = _rms_mod(x_ref[0], g_ref[...], sh_ref[0], sc_ref[0]).astype(BF16)
    o_ref[0] = _dot(h, w_ref[...]).astype(o_ref.dtype)


def _ctx_proj(x, g, shift, scale, w, n_cols, tm):
    bsz, n, d = x.shape
    vec = pl.BlockSpec((1, 1, d), lambda b, i: (b, 0, 0))
    return pl.pallas_call(
        _ctx_kernel,
        grid=(bsz, n // tm),
        in_specs=[pl.BlockSpec((1, tm, d), lambda b, i: (b, i, 0)),
                  pl.BlockSpec((1, d), lambda b, i: (0, 0)), vec, vec,
                  pl.BlockSpec((d, n_cols), lambda b, i: (0, 0))],
        out_specs=pl.BlockSpec((1, tm, n_cols), lambda b, i: (b, i, 0)),
        out_shape=jax.ShapeDtypeStruct((bsz, n, n_cols), BF16),
        compiler_params=_cparams(("parallel", "parallel")),
        name="ctx_proj",
    )(x, g, shift, scale, w)


PRO_CHUNKS = 8


def _next_block(b, i, ni, nb):
    t = jnp.minimum(b * ni + i + 1, nb * ni - 1)
    return t // ni, t % ni


def _in_pipe_kernel(x0_ref, rows0_ref, sh0_ref, sc0_ref, xn_ref, rowsn_ref, shn_ref, scn_ref,
                    cols_ref, g_ref, w_ref, o_ref, h_ref, *, tm):
    b, i, j = pl.program_id(0), pl.program_id(1), pl.program_id(2)
    par = (b * pl.num_programs(1) + i) % 2
    cr = tm // PRO_CHUNKS
    dh = cols_ref.shape[-1]

    @pl.when(jnp.logical_and(jnp.logical_and(b == 0, i == 0), j == 0))
    def _():
        x = x0_ref[0] + jnp.concatenate(_pos_block(rows0_ref[...], cols_ref[...], tm), axis=-1)
        h_ref[0] = _rms_mod(x, g_ref[...], sh0_ref[0], sc0_ref[0]).astype(BF16)

    o_ref[0] = _dot(h_ref[par], w_ref[...]).astype(o_ref.dtype)

    c = jnp.minimum(j, PRO_CHUNKS - 1)
    rows = pl.ds(pl.multiple_of(c * cr, cr), cr)
    gpc = cr // GRID_W
    rp = jnp.concatenate([jnp.broadcast_to(rowsn_ref[pl.ds(c * gpc + r, 1), :], (GRID_W, dh))
                          for r in range(gpc)], axis=0)
    cp = jnp.concatenate([cols_ref[...]] * gpc, axis=0)
    xc = xn_ref[0, rows, :] + jnp.concatenate([rp, cp], axis=-1)
    h_ref[1 - par, rows, :] = _rms_mod(xc, g_ref[...], shn_ref[0], scn_ref[0]).astype(BF16)


def _in_proj_latent(x, pos_tabs, g, shift, scale, w, tm, tn):
    bsz, n, d = x.shape
    n_cols = w.shape[1]
    rows_tab, cols_tab = pos_tabs
    dh = d // 2
    ni = n // tm
    nr = tm // GRID_W
    assert (tm // PRO_CHUNKS) % GRID_W == 0 and n_cols // tn >= PRO_CHUNKS
    nxt = lambda b, i: _next_block(b, i, ni, bsz)
    once = dict(pipeline_mode=pl.Buffered(1))
    return pl.pallas_call(
        functools.partial(_in_pipe_kernel, tm=tm),
        grid=(bsz, ni, n_cols // tn),
        in_specs=[
            pl.BlockSpec((1, tm, d), lambda b, i, j: (0, 0, 0), **once),
            pl.BlockSpec((nr, dh), lambda b, i, j: (0, 0)),
            pl.BlockSpec((1, 1, d), lambda b, i, j: (0, 0, 0)),
            pl.BlockSpec((1, 1, d), lambda b, i, j: (0, 0, 0)),
            pl.BlockSpec((1, tm, d), lambda b, i, j: nxt(b, i) + (0,)),
            pl.BlockSpec((nr, dh), lambda b, i, j: (nxt(b, i)[1], 0)),
            pl.BlockSpec((1, 1, d), lambda b, i, j: (nxt(b, i)[0], 0, 0)),
            pl.BlockSpec((1, 1, d), lambda b, i, j: (nxt(b, i)[0], 0, 0)),
            pl.BlockSpec((GRID_W, dh), lambda b, i, j: (0, 0)),
            pl.BlockSpec((1, d), lambda b, i, j: (0, 0)),
            pl.BlockSpec((d, tn), lambda b, i, j: (0, j)),
        ],
        out_specs=pl.BlockSpec((1, tm, tn), lambda b, i, j: (b, i, j)),
        out_shape=jax.ShapeDtypeStruct((bsz, n, n_cols), BF16),
        scratch_shapes=[pltpu.VMEM((2, tm, d), BF16)],
        compiler_params=_cparams(("arbitrary", "arbitrary", "arbitrary")),
        name="in_proj",
    )(x, rows_tab, shift, scale, x, rows_tab, shift, scale, cols_tab, g, w)


S5_CHUNKS = 8


def _s5_rows(k, tb, pitch):
    pad = pitch - tb
    if (k * pitch) % SUBLANES == 0:
        return k * pitch, k * pitch + tb
    return k * pitch - pad, k * pitch + tb + pad


def _s5_fill(buf, lhs_of, bd_ref, k, tb, pitch):
    lhs = lhs_of(k)
    lo, hi = _s5_rows(k, tb, pitch)
    if lo != k * pitch:
        z = jnp.zeros((k * pitch - lo, LANES), F32)
        lhs = jnp.concatenate([z, lhs.astype(F32), z], axis=0).astype(BF16)
    bu = _dot(lhs, bd_ref[0, k])
    for m in range(buf.shape[0]):
        buf[m, lo:hi, :] = bu[:, m * LANES:(m + 1) * LANES]


def _s5_pass(d, bu_scan, st_scan, carry_ref, st_mm, bu_mm, lhs_of, y_ref, bd_ref, cd_ref,
             lam_r, lam_i, tb, pitch):
    nslab = bu_scan.shape[0]
    half = nslab // 2
    per = tb // S5_CHUNKS
    sr = [carry_ref[m] for m in range(half)]
    si = [carry_ref[half + m] for m in range(half)]

    def scan_steps(q0, q1):
        for q in range(q0, q1):
            idx = pl.ds(jnp.where(d == 0, q, tb - 1 - q), SUBLANES, stride=pitch)
            for m in range(half):
                nr = lam_r[m] * sr[m] - lam_i[m] * si[m] + bu_scan[m, idx, :]
                ni = lam_r[m] * si[m] + lam_i[m] * sr[m] + bu_scan[half + m, idx, :]
                st_scan[m, idx, :] = nr
                st_scan[half + m, idx, :] = ni
                sr[m], si[m] = nr, ni

    for k in range(S5_CHUNKS):
        lo, hi = _s5_rows(k, tb, pitch)
        sk = jnp.concatenate([st_mm[m, lo:hi, :] for m in range(nslab)], axis=-1)
        yk = _dot(sk.astype(BF16), cd_ref[0, k])
        y_ref[0, :, k * LANES:(k + 1) * LANES] = yk[k * pitch - lo:k * pitch - lo + tb].astype(
            y_ref.dtype)
        scan_steps(k * per, k * per + per // 2)
        _s5_fill(bu_mm, lhs_of, bd_ref, k, tb, pitch)
        scan_steps(k * per + per // 2, (k + 1) * per)
    for m in range(half):
        carry_ref[m] = sr[m]
        carry_ref[half + m] = si[m]


def _s5_kernel(uc0_ref, uca_ref, uxa_ref, ucb_ref, uxb_ref, bd_ref, cd_ref, lr_ref, li_ref,
               ya_ref, yb_ref, a_bu, a_st, b_bu, b_st, ca_ref, cb_ref, *, tb, pitch, nc, nblk):
    d = pl.program_id(0)
    i = pl.program_id(1)
    half = a_bu.shape[0] // 2
    lam_r = [lr_ref[0, m] for m in range(half)]
    lam_i = [li_ref[0, m] for m in range(half)]

    @pl.when(i == 0)
    def _():
        ca_ref[...] = jnp.zeros_like(ca_ref)
        cb_ref[...] = jnp.zeros_like(cb_ref)
        a_st[...] = jnp.zeros_like(a_st)
        b_st[...] = jnp.zeros_like(b_st)
        for k in range(S5_CHUNKS):
            _s5_fill(a_bu, lambda k: uc0_ref[0, :, k * LANES:(k + 1) * LANES], bd_ref, k, tb, pitch)

    def lhs(uc_ref, ux_ref, pos):
        def of(k):
            cols = slice(k * LANES, (k + 1) * LANES)
            return jnp.where(pos < nc, uc_ref[0, :, cols], ux_ref[0, :, cols])
        return of

    _s5_pass(d, a_bu, a_st, ca_ref, b_st, b_bu, lhs(ucb_ref, uxb_ref, i), yb_ref,
             bd_ref, cd_ref, lam_r, lam_i, tb, pitch)

    @pl.when(i < nblk)
    def _():
        _s5_pass(d, b_bu, b_st, cb_ref, a_st, a_bu, lhs(uca_ref, uxa_ref, i + 1), ya_ref,
                 bd_ref, cd_ref, lam_r, lam_i, tb, pitch)


def _s5(u_c, p_x, bd, cd, lam_r, lam_i, s5w, tb):
    bsz, lc, _ = u_c.shape
    assert bsz == 2, "the two batch elements are the two interleaved sequences"
    n = p_x.shape[1]
    nc, nb = lc // tb, n // tb
    nblk = nc + nb
    nslab = bd.shape[-1] // LANES
    pitch = tb + SUBLANES // 2

    def blk(d, pos, count):
        j = jnp.clip(pos, 0, count - 1)
        return jnp.where(d == 0, j, count - 1 - j)

    def u_spec(b, off, ctx):
        if ctx:
            return pl.BlockSpec((1, tb, s5w), lambda d, i: (b, blk(d, i + off, nc), 0))
        return pl.BlockSpec((1, tb, s5w), lambda d, i: (b, blk(d, i + off - nc, nb), 0))

    def y_spec(off):
        return pl.BlockSpec((1, tb, s5w), lambda d, i: (d, blk(d, i + off - nc, nb), 0))

    par = lambda d, i: (d, 0, 0, 0)
    buf = pltpu.VMEM((nslab, S5_CHUNKS * pitch, LANES), F32)
    state = pltpu.VMEM((nslab, SUBLANES, LANES), F32)
    return pl.pallas_call(
        functools.partial(_s5_kernel, tb=tb, pitch=pitch, nc=nc, nblk=nblk),
        grid=(2, nblk + 1),
        in_specs=[
            pl.BlockSpec((1, tb, s5w), lambda d, i: (0, blk(d, 0, nc), 0)),
            u_spec(0, 1, True), u_spec(0, 1, False),
            u_spec(1, 0, True), u_spec(1, 0, False),
            pl.BlockSpec((1, S5_CHUNKS, LANES, nslab * LANES), par, pipeline_mode=pl.Buffered(1)),
            pl.BlockSpec((1, S5_CHUNKS, nslab * LANES, LANES), par, pipeline_mode=pl.Buffered(1)),
            pl.BlockSpec((1, nslab // 2, SUBLANES, LANES), par),
            pl.BlockSpec((1, nslab // 2, SUBLANES, LANES), par),
        ],
        out_specs=[y_spec(0), y_spec(-1)],
        out_shape=[jax.ShapeDtypeStruct((2, n, s5w), BF16)] * 2,
        scratch_shapes=[buf, buf, buf, buf, state, state],
        compiler_params=_cparams(("arbitrary", "arbitrary")),
        name="s5",
    )(u_c, u_c, p_x, u_c, p_x, bd, cd, lam_r, lam_i)


def _s5_params(a_re, a_im, log_dt, b_re, b_im, c_re, c_im):
    g, p, h = b_re.shape
    gl = g // S5_CHUNKS
    dt = jnp.exp(log_dt.astype(F32))[:, None]
    mag = jnp.exp(a_re.astype(F32) * dt)
    lr, li = mag * jnp.cos(a_im.astype(F32) * dt), mag * jnp.sin(a_im.astype(F32) * dt)
    den = a_re * a_re + a_im * a_im
    qr = ((lr - 1.0) * a_re + li * a_im) / den
    qi = (li * a_re - (lr - 1.0) * a_im) / den
    bbr = qr[:, :, None] * b_re - qi[:, :, None] * b_im
    bbi = qr[:, :, None] * b_im + qi[:, :, None] * b_re
    eye = jnp.eye(gl, dtype=F32)

    def pack_b(m):
        return jnp.einsum("kgph,gq->kghqp", m.reshape(S5_CHUNKS, gl, p, h), eye).reshape(
            S5_CHUNKS, gl * h, gl * p)

    def pack_c(m):
        return jnp.einsum("kghp,gq->kgpqh", m.reshape(S5_CHUNKS, gl, h, p), eye).reshape(
            S5_CHUNKS, gl * p, gl * h)

    bd = jnp.concatenate([pack_b(bbr), pack_b(bbi)], axis=-1).astype(BF16)
    cd = jnp.concatenate([pack_c(c_re.astype(F32)), -pack_c(c_im.astype(F32))], axis=1).astype(BF16)

    def slabs(v):
        return v.reshape(S5_CHUNKS, -1, LANES).transpose(1, 0, 2)

    return bd, cd, slabs(lr), slabs(li)


def _filt_kernel(f_ref, w1_ref, b1_ref, q1_ref, w2_ref, b2_ref, q2_ref, w3_ref, dec_ref,
                 bias_ref, o_ref, *, tq):
    f = f_ref[...]
    h = jnp.sin(q1_ref[...] * (_dot_hi(f, w1_ref[...]) + b1_ref[...]))
    h = jnp.sin(q2_ref[...] * (_dot_hi(h, w2_ref[...]) + b2_ref[...]))
    h = _dot_3pass(h, w3_ref[...])
    c = bias_ref.shape[-1]
    t = f[:, 0:1]
    lag = pl.program_id(0) * tq + lax.broadcasted_iota(jnp.int32, (tq, 1), 0)
    fwd = h[:, :c] * jnp.exp(-t * jnp.abs(dec_ref[0:1, :]))
    bwd = h[:, c:] * jnp.exp(-t * jnp.abs(dec_ref[1:2, :]))
    fwd = jnp.where(lag == 0, fwd + bias_ref[...], fwd)
    bwd = jnp.where(lag == 0, 0.0, bwd)
    _store_padded(o_ref, 0, fwd)
    _store_padded(o_ref, 1, bwd)


def _filter_taps(feats, w1, b1, q1, w2, b2, q2, w3, decay, bias, tq=512):
    n_lat, fp = feats.shape
    hid = w2.shape[0]
    c = bias.shape[-1]
    rows = tq // DFT_N2 * GROUP_PITCH
    const = lambda r: (0, 0)
    return pl.pallas_call(
        functools.partial(_filt_kernel, tq=tq),
        grid=(n_lat // tq,),
        in_specs=[pl.BlockSpec((tq, fp), lambda r: (r, 0)),
                  pl.BlockSpec((fp, hid), const), pl.BlockSpec((1, hid), const),
                  pl.BlockSpec((1, hid), const),
                  pl.BlockSpec((hid, hid), const), pl.BlockSpec((1, hid), const),
                  pl.BlockSpec((1, hid), const),
                  pl.BlockSpec((hid, 2 * c), const),
                  pl.BlockSpec((2, c), const),
                  pl.BlockSpec((1, c), const)],
        out_specs=pl.BlockSpec((2, rows, c), lambda r: (0, r, 0)),
        out_shape=jax.ShapeDtypeStruct((2, n_lat // DFT_N2 * GROUP_PITCH, c), F32),
        compiler_params=_cparams(("parallel",)),
        name="hy_filter",
    )(feats, w1, b1, q1, w2, b2, q2, w3, decay, bias)


def _conv3(cur_ref, prev_ref, next_ref, w_ref, b_ref, i, last, tm):
    row = lax.broadcasted_iota(jnp.int32, (tm, 1), 0)
    cur = cur_ref[0].astype(F32)
    halo = prev_ref.shape[1]
    prev_row = jnp.where(i == 0, 0.0, prev_ref[0, halo - 1:halo, :].astype(F32))
    next_row = jnp.where(i == last, 0.0, next_ref[0, 0:1, :].astype(F32))
    up = jnp.where(row == 0, prev_row, pltpu.roll(cur, 1, 0))
    dn = jnp.where(row == tm - 1, next_row, pltpu.roll(cur, tm - 1, 0))
    w = w_ref[...]
    return up * w[0:1] + cur * w[1:2] + dn * w[2:3] + b_ref[...]


CONV_HALO = 16


def _conv3_specs(tm, c, n, col_block):
    hb = tm // CONV_HALO
    nhb = n // CONV_HALO
    return [pl.BlockSpec((1, tm, c), lambda b, i: (b, i, col_block)),
            pl.BlockSpec((1, CONV_HALO, c),
                         lambda b, i: (b, jnp.maximum(i * hb - 1, 0), col_block)),
            pl.BlockSpec((1, CONV_HALO, c),
                         lambda b, i: (b, jnp.minimum((i + 1) * hb, nhb - 1), col_block))]


def _sconv_kernel(x1_ref, x1p_ref, x1n_ref, v_ref, vp_ref, vn_ref, w1_ref, wv_ref, b1_ref, bv_ref,
                  u_ref, *, tm):
    i = pl.program_id(1)
    last = pl.num_programs(1) - 1
    x1 = _conv3(x1_ref, x1p_ref, x1n_ref, w1_ref, b1_ref, i, last, tm)
    v = _conv3(v_ref, vp_ref, vn_ref, wv_ref, bv_ref, i, last, tm)
    _store_padded(u_ref, 0, x1 * v)


def _short_conv(p_x, conv_w, conv_b, col0, c, tm):
    bsz, n, _ = p_x.shape
    cb0 = col0 // c
    part = lambda p, shape: pl.BlockSpec(shape, lambda b, i: (0, p))
    return pl.pallas_call(
        functools.partial(_sconv_kernel, tm=tm),
        grid=(bsz, n // tm),
        in_specs=(_conv3_specs(tm, c, n, cb0 + 1) + _conv3_specs(tm, c, n, cb0 + 2)
                  + [part(1, (3, c)), part(2, (3, c)), part(1, (1, c)), part(2, (1, c))]),
        out_specs=pl.BlockSpec((1, tm // DFT_N2 * GROUP_PITCH, c), lambda b, i: (b, i, 0)),
        out_shape=jax.ShapeDtypeStruct((bsz, n // DFT_N2 * GROUP_PITCH, c), F32),
        compiler_params=_cparams(("parallel", "parallel")),
        name="short_conv",
    )(p_x, p_x, p_x, p_x, p_x, p_x, conv_w, conv_w, conv_b, conv_b)


def _dft_tables(n1):
    n = n1 * DFT_N2
    k1 = jnp.arange(n1, dtype=jnp.int32)
    n2 = jnp.arange(DFT_N2, dtype=jnp.int32)

    def unit(ph, period):
        ang = (ph % period).astype(F32) * F32(2.0 * math.pi / period)
        return jnp.cos(ang), -jnp.sin(ang)

    ar, ai = unit(k1[:, None] * k1[None, :], n1)
    tr, ti = unit(n2[:, None] * k1[None, :], n)
    gr = ar[None] * tr[:, :, None] - ai[None] * ti[:, :, None]
    gi = ar[None] * ti[:, :, None] + ai[None] * tr[:, :, None]
    h = n1 // 2
    g1c = jnp.concatenate([jnp.concatenate([gr[:, :, :h], -gi[:, :, :h]], axis=2),
                           jnp.concatenate([gi[:, :, :h], gr[:, :, :h]], axis=2)], axis=1)
    m = jnp.arange(h, dtype=jnp.int32)

    def fold(g):
        up = jnp.take(g, n1 - 1 - m, axis=2)
        up0 = jnp.take(g[0], (n1 - m) % n1, axis=1) * (m != 0).astype(F32)
        return jnp.concatenate([g[:, :, :h], up.at[0].set(up0)], axis=2)

    g1f = jnp.concatenate([fold(gr), fold(gi)], axis=1)
    fr, fi = unit(n2[:, None] * n2[None, :], DFT_N2)
    f2 = jnp.concatenate([jnp.concatenate([fr, -fi], axis=1),
                          jnp.concatenate([fi, fr], axis=1)], axis=0)
    f2c = jnp.concatenate([jnp.concatenate([fr, fi], axis=1),
                           jnp.concatenate([-fi, fr], axis=1)], axis=0) / n
    return g1c.astype(BF16), g1f.astype(BF16), f2.astype(BF16), f2c.astype(BF16)


def _stage1(gather, g_ref, a_ref, cn, n1, pitch):
    def trip(chunk, carry):
        for q in range(cn):
            n2 = chunk * cn + q
            a = _dot(g_ref[n2], gather(n2).astype(BF16))
            a_ref[0, pl.ds(n2, n1, stride=pitch), :] = a[:n1]
            a_ref[1, pl.ds(n2, n1, stride=pitch), :] = a[n1:]
        return carry
    lax.fori_loop(0, DFT_N2 // cn, trip, 0)


def _a_rows(chunk, ck, q, pitch):
    return pl.ds(pl.multiple_of(chunk * (ck * pitch), SUBLANES) + q * pitch, DFT_N2)


def _load_a_chunk(a_ref, chunk, ck, pitch):
    tiles = []
    for q in range(ck):
        rows = _a_rows(chunk, ck, q, pitch)
        tiles.append(jnp.concatenate([a_ref[0, rows, :], a_ref[1, rows, :]], axis=0).astype(BF16))
    return jnp.concatenate(tiles, axis=1)


def _spec_kernel(t_ref, g_ref, f2_ref, o_ref, a_ref, *, n1, cn, ck, pitch):
    s = pl.program_id(1)
    h = n1 // 2

    @pl.when(s == 0)
    def _():
        def gather(n2):
            back = (DFT_N2 - n2) % DFT_N2
            return jnp.concatenate([t_ref[0, pl.ds(n2, h, stride=GROUP_PITCH), :],
                                    t_ref[1, pl.ds(back, h, stride=GROUP_PITCH), :]], axis=0)
        _stage1(gather, g_ref, a_ref, cn, n1, pitch)

    @pl.when(s >= 1)
    def _():
        o_ref[0] = _dot(f2_ref[...], _load_a_chunk(a_ref, s - 1, ck, pitch)).astype(o_ref.dtype)


def _conv_kernel(u_ref, g1_ref, kf_ref, f2_ref, f2c_ref, y_ref, a_ref,
                 *, n1, cn, ck, pitch, np2):
    s = pl.program_id(1)
    h = n1 // 2

    @pl.when(s == 0)
    def _():
        y_ref[...] = jnp.zeros_like(y_ref)

        def gather(n2):
            idx = pl.ds(n2, h, stride=GROUP_PITCH)
            return jnp.concatenate([u_ref[0, idx, :], u_ref[1, idx, :]], axis=0)
        _stage1(gather, g1_ref, a_ref, cn, n1, pitch)

    @pl.when(jnp.logical_and(s >= 1, s <= np2))
    def _():
        chunk = s - 1
        x = _dot(f2_ref[...], _load_a_chunk(a_ref, chunk, ck, pitch))
        xr, xi = x[:DFT_N2], x[DFT_N2:]
        kr, ki = kf_ref[0, :DFT_N2, :].astype(F32), kf_ref[0, DFT_N2:, :].astype(F32)
        y = jnp.concatenate([xr * kr - xi * ki, xr * ki + xi * kr], axis=0)
        b = _dot(f2c_ref[...], y.astype(BF16))
        for q in range(ck):
            rows = _a_rows(chunk, ck, q, pitch)
            a_ref[0, rows, :] = b[:DFT_N2, q * LANES:(q + 1) * LANES]
            a_ref[1, rows, :] = b[DFT_N2:, q * LANES:(q + 1) * LANES]

    @pl.when(s > np2)
    def _():
        def trip(chunk, carry):
            for q in range(cn):
                n2 = chunk * cn + q
                idx = pl.ds(n2, n1, stride=pitch)
                b = jnp.concatenate([a_ref[0, idx, :], a_ref[1, idx, :]], axis=0)
                y = lax.dot_general(g1_ref[n2], b.astype(BF16), (((0,), (0,)), ((), ())),
                                    preferred_element_type=F32)
                out = pl.ds(n2, h, stride=GROUP_PITCH)
                y_ref[0, out, :] = y[:h]
                y_ref[1, out, :] = y[h:]
            return carry
        lax.fori_loop(0, DFT_N2 // cn, trip, 0)


DFT_UNROLL = 32
A_PITCH = DFT_N2 + SUBLANES


def _filter_spectrum(taps, g1f, f2):
    _, rows, c = taps.shape
    n1 = 2 * rows // GROUP_PITCH
    ck = min(32, n1)
    np2 = n1 // ck
    pitch = A_PITCH
    return pl.pallas_call(
        functools.partial(_spec_kernel, n1=n1, cn=DFT_UNROLL, ck=ck, pitch=pitch),
        grid=(c // LANES, 1 + np2),
        in_specs=[pl.BlockSpec((2, rows, LANES), lambda j, s: (0, 0, j),
                               pipeline_mode=pl.Buffered(1)),
                  pl.BlockSpec((DFT_N2, 2 * n1, n1), lambda j, s: (0, 0, 0),
                               pipeline_mode=pl.Buffered(1)),
                  pl.BlockSpec((2 * DFT_N2, 2 * DFT_N2), lambda j, s: (0, 0))],
        out_specs=pl.BlockSpec((1, 2 * DFT_N2, ck * LANES),
                               lambda j, s: (j, 0, jnp.maximum(s - 1, 0))),
        out_shape=jax.ShapeDtypeStruct((c // LANES, 2 * DFT_N2, n1 * LANES), BF16),
        scratch_shapes=[pltpu.VMEM((2, n1 * pitch, LANES), F32)],
        compiler_params=_cparams(("parallel", "arbitrary")),
        name="hy_spectrum",
    )(taps, g1f, f2)


def _long_conv(u, kf, g1c, f2, f2c):
    bsz, rows, c = u.shape
    assert bsz == 2, "the two batch elements are packed as one complex signal"
    n1 = 2 * rows // GROUP_PITCH
    ck = min(16, n1)
    np2 = n1 // ck
    pitch = A_PITCH
    once = dict(pipeline_mode=pl.Buffered(1))
    return pl.pallas_call(
        functools.partial(_conv_kernel, n1=n1, cn=DFT_UNROLL, ck=ck, pitch=pitch, np2=np2),
        grid=(c // LANES, np2 + 2),
        in_specs=[
            pl.BlockSpec((2, rows, LANES), lambda j, s: (0, 0, j), **once),
            pl.BlockSpec((DFT_N2, 2 * n1, n1), lambda j, s: (0, 0, 0), **once),
            pl.BlockSpec((1, 2 * DFT_N2, ck * LANES),
                         lambda j, s: (j, 0, jnp.clip(s - 1, 0, np2 - 1))),
            pl.BlockSpec((2 * DFT_N2, 2 * DFT_N2), lambda j, s: (0, 0)),
            pl.BlockSpec((2 * DFT_N2, 2 * DFT_N2), lambda j, s: (0, 0)),
        ],
        out_specs=pl.BlockSpec((2, rows, LANES), lambda j, s: (0, 0, j), **once),
        out_shape=jax.ShapeDtypeStruct((2, rows, c), F32),
        scratch_shapes=[pltpu.VMEM((2, n1 * pitch, LANES), F32)],
        compiler_params=_cparams(("parallel", "arbitrary")),
        name="hy_conv",
    )(u, g1c, kf, f2, f2c)


def _mixer_kernel(ya_ref, yb_ref, u_ref, h0_ref, h0p_ref, h0n_ref, cw_ref, cb_ref, yc_ref, gs_ref,
                  gh_ref, x_ref, rows_ref, cols_ref, d_ref, gate_ref, gf_ref, shf_ref, scf_ref,
                  wglu_ref, why_ref, wout_ref, o_ref, hf_ref, *, tm, cw):
    first = pl.program_id(0) == 0
    y = (jnp.where(first, ya_ref[0], yb_ref[0]).astype(F32)
         + jnp.where(first, ya_ref[1], yb_ref[1]).astype(F32)
         + u_ref[0].astype(F32) * d_ref[...])
    h = jax.nn.gelu(y, approximate=True).astype(BF16)
    y_conv = jnp.concatenate(
        [yc_ref[0, g * GROUP_PITCH:g * GROUP_PITCH + DFT_N2, :] for g in range(tm // DFT_N2)],
        axis=0)
    x0 = _conv3(h0_ref, h0p_ref, h0n_ref, cw_ref, cb_ref, pl.program_id(1),
                pl.num_programs(1) - 1, tm)
    hx = (x0 * y_conv).astype(BF16)
    d = o_ref.shape[-1]
    merged = []
    for c in range(d // cw):
        cols = slice(c * cw, (c + 1) * cw)
        glu_v = _dot(h, wglu_ref[:, cols])
        glu_g = _dot(h, wglu_ref[:, d + c * cw:d + (c + 1) * cw])
        branch_hy = _dot(hx, why_ref[:, cols])
        m = (jax.nn.sigmoid(gs_ref[0, :, cols].astype(F32)) * (glu_v * jax.nn.sigmoid(glu_g))
             + jax.nn.sigmoid(gh_ref[0, :, cols].astype(F32)) * branch_hy)
        merged.append(m.astype(BF16))
    merged = jnp.concatenate(merged, axis=-1)
    dh = d // 2
    rp, cp = _pos_block(rows_ref[0], cols_ref[...], tm)
    for c in range(d // cw):
        cols = slice(c * cw, (c + 1) * cw)
        pos = rp[:, cols] if (c + 1) * cw <= dh else cp[:, c * cw - dh:(c + 1) * cw - dh]
        o_ref[0, :, cols] = (x_ref[0, :, cols] + pos
                             + gate_ref[0, :, cols] * _dot(merged, wout_ref[:, cols]))
    hf_ref[0] = _rms_mod(o_ref[0], gf_ref[...], shf_ref[0], scf_ref[0]).astype(BF16)


def _mixer(y_ssm, p_x, conv_w, conv_b, y_conv, x, pos_tabs, s5_d, gate, g_ffn, shift_ffn, scale_ffn,
           w_glu, w_hy_out, w_out, s5w, gate_col0, tm):
    bsz, n, d = x.shape
    assert bsz == 2, "the S5 readouts arrive as one array per batch element"
    hyw = w_hy_out.shape[0]
    rows_tab, cols_tab = pos_tabs
    nr = tm // GRID_W
    dh = d // 2
    ni = n // tm
    g0 = gate_col0 // d
    cw = min(512, dh)
    resident = dict(pipeline_mode=pl.Buffered(1))
    tok = lambda width, col: pl.BlockSpec((1, tm, width), lambda b, i: (b, i, col))
    return pl.pallas_call(
        functools.partial(_mixer_kernel, tm=tm, cw=cw),
        grid=(bsz, ni),
        in_specs=[
            pl.BlockSpec((2, tm, s5w), lambda b, i: (0, jnp.where(b == 0, i, ni - 1), 0)),
            pl.BlockSpec((2, tm, s5w), lambda b, i: (0, jnp.where(b == 0, 0, i), 0)),
            tok(s5w, 0),
            *_conv3_specs(tm, hyw, n, s5w // hyw),
            pl.BlockSpec((3, hyw), lambda b, i: (0, 0)),
            pl.BlockSpec((1, hyw), lambda b, i: (0, 0)),
            pl.BlockSpec((1, tm // DFT_N2 * GROUP_PITCH, hyw), lambda b, i: (b, i, 0)),
            tok(d, g0), tok(d, g0 + 1), tok(d, 0),
            pl.BlockSpec((1, nr, dh), lambda b, i: (i, 0, 0)),
            pl.BlockSpec((GRID_W, dh), lambda b, i: (0, 0)),
            pl.BlockSpec((1, s5w), lambda b, i: (0, 0)),
            pl.BlockSpec((1, 1, d), lambda b, i: (b, 0, 0)),
            pl.BlockSpec((1, d), lambda b, i: (0, 0)),
            pl.BlockSpec((1, 1, d), lambda b, i: (b, 0, 0)),
            pl.BlockSpec((1, 1, d), lambda b, i: (b, 0, 0)),
            pl.BlockSpec((s5w, 2 * d), lambda b, i: (0, 0), **resident),
            pl.BlockSpec((hyw, d), lambda b, i: (0, 0), **resident),
            pl.BlockSpec((d, d), lambda b, i: (0, 0), **resident),
        ],
        out_specs=[pl.BlockSpec((1, tm, d), lambda b, i: (b, i, 0)),
                   pl.BlockSpec((1, tm, d), lambda b, i: (b, i, 0))],
        out_shape=[jax.ShapeDtypeStruct((bsz, n, d), F32),
                   jax.ShapeDtypeStruct((bsz, n, d), BF16)],
        compiler_params=_cparams(("arbitrary", "arbitrary")),
        name="mixer",
    )(y_ssm[0], y_ssm[1], p_x, p_x, p_x, p_x, conv_w, conv_b, y_conv, p_x, p_x, x,
      rows_tab.reshape(-1, nr, dh), cols_tab, s5_d, gate, g_ffn, shift_ffn, scale_ffn,
      w_glu, w_hy_out, w_out)


def _ffn_kernel(x_ref, h_ref, gate_ref, wa_ref, wb_ref, wo_ref, nf_ref, o_ref, acc_ref):
    j = pl.program_id(2)

    @pl.when(j == 0)
    def _():
        acc_ref[...] = jnp.zeros_like(acc_ref)

    h = h_ref[0]
    act = _silu(_dot(h, wa_ref[...])) * _dot(h, wb_ref[...])
    acc_ref[...] += _dot(act.astype(BF16), wo_ref[...])

    @pl.when(j == pl.num_programs(2) - 1)
    def _():
        xo = x_ref[0] + gate_ref[0] * acc_ref[...]
        o_ref[0] = xo * lax.rsqrt(jnp.mean(xo * xo, axis=-1, keepdims=True) + EPS) * nf_ref[...]


def _ffn(x, h, gate, w_in, w_out, norm_f, tm, tf):
    bsz, n, d = x.shape
    dff = w_out.shape[0]
    nj = dff // tf
    tok = pl.BlockSpec((1, tm, d), lambda b, i, j: (b, i, 0))
    return pl.pallas_call(
        _ffn_kernel,
        grid=(bsz, n // tm, nj),
        in_specs=[tok, tok,
                  pl.BlockSpec((1, 1, d), lambda b, i, j: (b, 0, 0)),
                  pl.BlockSpec((d, tf), lambda b, i, j: (0, j)),
                  pl.BlockSpec((d, tf), lambda b, i, j: (0, nj + j)),
                  pl.BlockSpec((tf, d), lambda b, i, j: (j, 0)),
                  pl.BlockSpec((1, d), lambda b, i, j: (0, 0))],
        out_specs=pl.BlockSpec((1, tm, d), lambda b, i, j: (b, i, 0)),
        out_shape=jax.ShapeDtypeStruct((bsz, n, d), F32),
        scratch_shapes=[pltpu.VMEM((tm, d), F32)],
        compiler_params=_cparams(("parallel", "parallel", "arbitrary")),
        name="ffn",
    )(x, h, gate, w_in, w_in, w_out, norm_f)


def _pos_tables(n_rows, d):
    quarter = d // 4
    omega = 10000.0 ** (-jnp.arange(quarter, dtype=F32) / quarter)
    ar = jnp.arange(n_rows, dtype=F32)[:, None] * omega
    ac = jnp.arange(GRID_W, dtype=F32)[:, None] * omega
    return (jnp.concatenate([jnp.sin(ar), jnp.cos(ar)], axis=-1),
            jnp.concatenate([jnp.sin(ac), jnp.cos(ac)], axis=-1))


def _filter_features(n_lat, width):
    pos = jnp.arange(n_lat, dtype=F32)
    t = pos / float(max(n_lat - 1, 1))
    w = 2.0 * math.pi * pos / n_lat
    bands = jnp.linspace(1e-4, HY_BANDS - 1, HY_BANDS, dtype=F32)
    feats = jnp.concatenate([t[:, None], jnp.cos(w[:, None] * bands), -jnp.sin(w[:, None] * bands)],
                            axis=-1)
    return jnp.pad(feats, ((0, 0), (0, width - feats.shape[1])))


def kernel(x, c, ctx, c_ctx, w_ada, b_ada, norm_mix, w_in, s5_a_re, s5_a_im, s5_log_dt,
           s5_b_re, s5_b_im, s5_c_re, s5_c_im, s5_d, w_glu, hy_conv_w, hy_conv_b,
           hy_f1_w, hy_f1_b, hy_f1_freq, hy_f2_w, hy_f2_b, hy_f2_freq, hy_f3_w, hy_decay,
           hy_bias, w_hy_out, w_out, norm_ffn, w_ffn_in, w_ffn_out, norm_f):
    bsz, n_lat, d = x.shape
    depth = w_ada.shape[0]
    assert depth == 1, "the context stream is only advanced for the single-layer trunk"
    l = 0
    s5w = s5_d.shape[-1]
    hyw = hy_bias.shape[-1]
    n_ctx = ctx.shape[1]
    tm = 512
    tn = min(1024, d)
    tb = 256
    assert n_ctx % tb == 0 and n_lat % tm == 0 and s5w == S5_CHUNKS * LANES

    pos_tabs = _pos_tables(n_lat // GRID_W, d)

    cond_t = jnp.zeros((d, SUBLANES), F32).at[:, :bsz].set(c.T).at[:, bsz].set(c_ctx)
    ada = _ada(cond_t, w_ada[l], b_ada[l][None], bsz + 1, tn=min(1024, d))

    def vec(row0, rows, part):
        v = ada[row0:row0 + rows, part * d:(part + 1) * d]
        return jnp.broadcast_to(v, (bsz, d))[:, None, :]

    shift_mix, scale_mix, gate_mix = vec(0, bsz, 0), vec(0, bsz, 1), vec(0, bsz, 2)
    shift_ffn, scale_ffn, gate_ffn = vec(0, bsz, 3), vec(0, bsz, 4), vec(0, bsz, 5)
    cshift_mix, cscale_mix = vec(bsz, 1, 0), vec(bsz, 1, 1)

    w_in_b = w_in[l].astype(BF16)
    g_mix = norm_mix[l][None]
    u_c = _ctx_proj(ctx, g_mix, cshift_mix, cscale_mix, w_in_b, s5w, tb)
    p_x = _in_proj_latent(x, pos_tabs, g_mix, shift_mix, scale_mix, w_in_b, min(1024, n_lat), tn)

    packed = [_s5_params(s5_a_re[l, k], s5_a_im[l, k], s5_log_dt[l, k], s5_b_re[l, k],
                         s5_b_im[l, k], s5_c_re[l, k], s5_c_im[l, k]) for k in range(2)]
    bd, cd, lam_r, lam_i = (jnp.stack(t) for t in zip(*packed))
    y_ssm = _s5(u_c, p_x, bd, cd, lam_r, lam_i, s5w, tb)

    fp = 64
    feats = _filter_features(n_lat, fp)
    w1 = jnp.pad(hy_f1_w[l], ((0, fp - hy_f1_w.shape[1]), (0, 0)))
    taps = _filter_taps(feats, w1, hy_f1_b[l][None], hy_f1_freq[l][None], hy_f2_w[l],
                        hy_f2_b[l][None], hy_f2_freq[l][None], hy_f3_w[l], hy_decay[l],
                        hy_bias[l][None])
    g1c, g1f, f2, f2c = _dft_tables(2 * n_lat // DFT_N2)
    kf = _filter_spectrum(taps, g1f, f2)
    u_hy = _short_conv(p_x, hy_conv_w[l], hy_conv_b[l][None], s5w, hyw, tm)
    y_conv = _long_conv(u_hy, kf, g1c, f2, f2c)

    x1, h_ffn = _mixer(y_ssm, p_x, hy_conv_w[l], hy_conv_b[l][None], y_conv, x, pos_tabs,
                       s5_d[l][None], gate_mix, norm_ffn[l][None], shift_ffn, scale_ffn,
                       w_glu[l].astype(BF16), w_hy_out[l].astype(BF16), w_out[l].astype(BF16),
                       s5w, s5w + 3 * hyw, 256)

    return _ffn(x1, h_ffn, gate_ffn, w_ffn_in[l].astype(BF16), w_ffn_out[l].astype(BF16),
                norm_f[None], tm, 512)
```

```python
import functools
import math

import jax
import jax.numpy as jnp
from jax import lax
from jax.experimental import pallas as pl
from jax.experimental.pallas import tpu as pltpu

F32 = jnp.float32
BF16 = jnp.bfloat16
HIGHEST = lax.Precision.HIGHEST

GRID_W = 64
N_ADA = 6
EPS = 1e-6
HY_BANDS = 16
LANES = 128
SUBLANES = 8
DFT_N2 = LANES
GROUP_PITCH = DFT_N2 + SUBLANES // 2
VMEM_LIMIT = 60 * 1024 * 1024


def _cparams(sem):
    return pltpu.CompilerParams(dimension_semantics=sem, vmem_limit_bytes=VMEM_LIMIT)


def _dot(a, b):
    return jnp.dot(a, b, preferred_element_type=F32)


def _dot_hi(a, b):
    return jnp.dot(a, b, preferred_element_type=F32, precision=HIGHEST)


def _dot_3pass(a, b):
    a_hi, b_hi = a.astype(BF16), b.astype(BF16)
    a_lo = (a - a_hi.astype(F32)).astype(BF16)
    b_lo = (b - b_hi.astype(F32)).astype(BF16)
    return _dot(a_hi, b_hi) + (_dot(a_hi, b_lo) + _dot(a_lo, b_hi))


def _silu(x):
    return x * jax.nn.sigmoid(x)


def _rms_mod(x, g, shift, scale):
    y = x * lax.rsqrt(jnp.mean(x * x, axis=-1, keepdims=True) + EPS)
    return (y * g) * (1.0 + scale) + shift


def _pos_block(rows, cols, tm):
    nr = tm // GRID_W
    dh = rows.shape[-1]
    rp = jnp.broadcast_to(rows[:, None, :], (nr, GRID_W, dh)).reshape(tm, dh)
    cp = jnp.broadcast_to(cols[None], (nr, GRID_W, dh)).reshape(tm, dh)
    return rp, cp


def _store_padded(ref, lead, val):
    pad = jnp.zeros((GROUP_PITCH - DFT_N2, val.shape[-1]), val.dtype)
    for g in range(val.shape[0] // DFT_N2):
        ref[lead, g * GROUP_PITCH:g * GROUP_PITCH + DFT_N2, :] = val[g * DFT_N2:(g + 1) * DFT_N2]
        ref[lead, g * GROUP_PITCH + DFT_N2:(g + 1) * GROUP_PITCH, :] = pad


def _ada_kernel(ct_ref, w_ref, b_ref, o_ref, *, n_vec):
    sc = _silu(ct_ref[...])
    w = w_ref[...]
    rows = [jnp.sum(w * sc[:, r:r + 1], axis=0, keepdims=True) for r in range(n_vec)]
    rows += [jnp.zeros_like(rows[0])] * (SUBLANES - n_vec)
    o_ref[...] = jnp.concatenate(rows, axis=0) + b_ref[...]


def _ada(cond_t, w, b, n_vec, tn=1024):
    d, n = w.shape
    return pl.pallas_call(
        functools.partial(_ada_kernel, n_vec=n_vec),
        grid=(n // tn,),
        in_specs=[pl.BlockSpec((d, SUBLANES), lambda j: (0, 0)),
                  pl.BlockSpec((d, tn), lambda j: (0, j)),
                  pl.BlockSpec((1, tn), lambda j: (0, j))],
        out_specs=pl.BlockSpec((SUBLANES, tn), lambda j: (0, j)),
        out_shape=jax.ShapeDtypeStruct((SUBLANES, n), F32),
        compiler_params=_cparams(("arbitrary",)),
        name="ada",
    )(cond_t, w, b)


def _ctx_kernel(x_ref, g_ref, sh_ref, sc_ref, w_ref, o_ref):
    h = _rms_mod(x_ref[0], g_ref[...], sh_ref[0], sc_ref[0]).astype(BF16)
    o_ref[0] = _dot(h, w_ref[...]).astype(o_ref.dtype)


def _ctx_proj(x, g, shift, scale, w, n_cols, tm):
    bsz, n, d = x.shape
    vec = pl.BlockSpec((1, 1, d), lambda b, i: (b, 0, 0))
    return pl.pallas_call(
        _ctx_kernel,
        grid=(bsz, n // tm),
        in_specs=[pl.BlockSpec((1, tm, d), lambda b, i: (b, i, 0)),
                  pl.BlockSpec((1, d), lambda b, i: (0, 0)), vec, vec,
                  pl.BlockSpec((d, n_cols), lambda b, i: (0, 0))],
        out_specs=pl.BlockSpec((1, tm, n_cols), lambda b, i: (b, i, 0)),
        out_shape=jax.ShapeDtypeStruct((bsz, n, n_cols), BF16),
        compiler_params=_cparams(("parallel", "parallel")),
        name="ctx_proj",
    )(x, g, shift, scale, w)


PRO_CHUNKS = 8


def _next_block(b, i, ni, nb):
    t = jnp.minimum(b * ni + i + 1, nb * ni - 1)
    return t // ni, t % ni


def _in_pipe_kernel(x0_ref, rows0_ref, sh0_ref, sc0_ref, xn_ref, rowsn_ref, shn_ref, scn_ref,
                    cols_ref, g_ref, w_ref, o_ref, h_ref, *, tm):
    b, i, j = pl.program_id(0), pl.program_id(1), pl.program_id(2)
    par = (b * pl.num_programs(1) + i) % 2
    cr = tm // PRO_CHUNKS
    dh = cols_ref.shape[-1]

    @pl.when(jnp.logical_and(jnp.logical_and(b == 0, i == 0), j == 0))
    def _():
        x = x0_ref[0] + jnp.concatenate(_pos_block(rows0_ref[...], cols_ref[...], tm), axis=-1)
        h_ref[0] = _rms_mod(x, g_ref[...], sh0_ref[0], sc0_ref[0]).astype(BF16)

    o_ref[0] = _dot(h_ref[par], w_ref[...]).astype(o_ref.dtype)

    c = jnp.minimum(j, PRO_CHUNKS - 1)
    rows = pl.ds(pl.multiple_of(c * cr, cr), cr)
    gpc = cr // GRID_W
    rp = jnp.concatenate([jnp.broadcast_to(rowsn_ref[pl.ds(c * gpc + r, 1), :], (GRID_W, dh))
                          for r in range(gpc)], axis=0)
    cp = jnp.concatenate([cols_ref[...]] * gpc, axis=0)
    xc = xn_ref[0, rows, :] + jnp.concatenate([rp, cp], axis=-1)
    h_ref[1 - par, rows, :] = _rms_mod(xc, g_ref[...], shn_ref[0], scn_ref[0]).astype(BF16)


def _in_proj_latent(x, pos_tabs, g, shift, scale, w, tm, tn):
    bsz, n, d = x.shape
    n_cols = w.shape[1]
    rows_tab, cols_tab = pos_tabs
    dh = d // 2
    ni = n // tm
    nr = tm // GRID_W
    assert (tm // PRO_CHUNKS) % GRID_W == 0 and n_cols // tn >= PRO_CHUNKS
    nxt = lambda b, i: _next_block(b, i, ni, bsz)
    once = dict(pipeline_mode=pl.Buffered(1))
    return pl.pallas_call(
        functools.partial(_in_pipe_kernel, tm=tm),
        grid=(bsz, ni, n_cols // tn),
        in_specs=[
            pl.BlockSpec((1, tm, d), lambda b, i, j: (0, 0, 0), **once),
            pl.BlockSpec((nr, dh), lambda b, i, j: (0, 0)),
            pl.BlockSpec((1, 1, d), lambda b, i, j: (0, 0, 0)),
            pl.BlockSpec((1, 1, d), lambda b, i, j: (0, 0, 0)),
            pl.BlockSpec((1, tm, d), lambda b, i, j: nxt(b, i) + (0,)),
            pl.BlockSpec((nr, dh), lambda b, i, j: (nxt(b, i)[1], 0)),
            pl.BlockSpec((1, 1, d), lambda b, i, j: (nxt(b, i)[0], 0, 0)),
            pl.BlockSpec((1, 1, d), lambda b, i, j: (nxt(b, i)[0], 0, 0)),
            pl.BlockSpec((GRID_W, dh), lambda b, i, j: (0, 0)),
            pl.BlockSpec((1, d), lambda b, i, j: (0, 0)),
            pl.BlockSpec((d, tn), lambda b, i, j: (0, j)),
        ],
        out_specs=pl.BlockSpec((1, tm, tn), lambda b, i, j: (b, i, j)),
        out_shape=jax.ShapeDtypeStruct((bsz, n, n_cols), BF16),
        scratch_shapes=[pltpu.VMEM((2, tm, d), BF16)],
        compiler_params=_cparams(("arbitrary", "arbitrary", "arbitrary")),
        name="in_proj",
    )(x, rows_tab, shift, scale, x, rows_tab, shift, scale, cols_tab, g, w)


S5_CHUNKS = 8


def _s5_rows(k, tb, pitch):
    pad = pitch - tb
    if (k * pitch) % SUBLANES == 0:
        return k * pitch, k * pitch + tb
    return k * pitch - pad, k * pitch + tb + pad


def _s5_fill(buf, lhs_of, bd_ref, k, tb, pitch):
    lhs = lhs_of(k)
    lo, hi = _s5_rows(k, tb, pitch)
    if lo != k * pitch:
        z = jnp.zeros((k * pitch - lo, LANES), F32)
        lhs = jnp.concatenate([z, lhs.astype(F32), z], axis=0).astype(BF16)
    bu = _dot(lhs, bd_ref[0, k])
    for m in range(buf.shape[0]):
        buf[m, lo:hi, :] = bu[:, m * LANES:(m + 1) * LANES]


def _s5_pass(d, bu_scan, st_scan, carry_ref, st_mm, bu_mm, lhs_of, y_ref, bd_ref, cd_ref,
             lam_r, lam_i, tb, pitch):
    nslab = bu_scan.shape[0]
    half = nslab // 2
    per = tb // S5_CHUNKS
    sr = [carry_ref[m] for m in range(half)]
    si = [carry_ref[half + m] for m in range(half)]

    def scan_steps(q0, q1):
        for q in range(q0, q1):
            idx = pl.ds(jnp.where(d == 0, q, tb - 1 - q), SUBLANES, stride=pitch)
            for m in range(half):
                nr = lam_r[m] * sr[m] - lam_i[m] * si[m] + bu_scan[m, idx, :]
                ni = lam_r[m] * si[m] + lam_i[m] * sr[m] + bu_scan[half + m, idx, :]
                st_scan[m, idx, :] = nr
                st_scan[half + m, idx, :] = ni
                sr[m], si[m] = nr, ni

    for k in range(S5_CHUNKS):
        lo, hi = _s5_rows(k, tb, pitch)
        sk = jnp.concatenate([st_mm[m, lo:hi, :] for m in range(nslab)], axis=-1)
        yk = _dot(sk.astype(BF16), cd_ref[0, k])
        y_ref[0, :, k * LANES:(k + 1) * LANES] = yk[k * pitch - lo:k * pitch - lo + tb].astype(
            y_ref.dtype)
        scan_steps(k * per, k * per + per // 2)
        _s5_fill(bu_mm, lhs_of, bd_ref, k, tb, pitch)
        scan_steps(k * per + per // 2, (k + 1) * per)
    for m in range(half):
        carry_ref[m] = sr[m]
        carry_ref[half + m] = si[m]


def _s5_kernel(uc0_ref, uca_ref, uxa_ref, ucb_ref, uxb_ref, bd_ref, cd_ref, lr_ref, li_ref,
               ya_ref, yb_ref, a_bu, a_st, b_bu, b_st, ca_ref, cb_ref, *, tb, pitch, nc, nblk):
    d = pl.program_id(0)
    i = pl.program_id(1)
    half = a_bu.shape[0] // 2
    lam_r = [lr_ref[0, m] for m in range(half)]
    lam_i = [li_ref[0, m] for m in range(half)]

    @pl.when(i == 0)
    def _():
        ca_ref[...] = jnp.zeros_like(ca_ref)
        cb_ref[...] = jnp.zeros_like(cb_ref)
        a_st[...] = jnp.zeros_like(a_st)
        b_st[...] = jnp.zeros_like(b_st)
        for k in range(S5_CHUNKS):
            _s5_fill(a_bu, lambda k: uc0_ref[0, :, k * LANES:(k + 1) * LANES], bd_ref, k, tb, pitch)

    def lhs(uc_ref, ux_ref, pos):
        def of(k):
            cols = slice(k * LANES, (k + 1) * LANES)
            return jnp.where(pos < nc, uc_ref[0, :, cols], ux_ref[0, :, cols])
        return of

    _s5_pass(d, a_bu, a_st, ca_ref, b_st, b_bu, lhs(ucb_ref, uxb_ref, i), yb_ref,
             bd_ref, cd_ref, lam_r, lam_i, tb, pitch)

    @pl.when(i < nblk)
    def _():
        _s5_pass(d, b_bu, b_st, cb_ref, a_st, a_bu, lhs(uca_ref, uxa_ref, i + 1), ya_ref,
                 bd_ref, cd_ref, lam_r, lam_i, tb, pitch)


def _s5(u_c, p_x, bd, cd, lam_r, lam_i, s5w, tb):
    bsz, lc, _ = u_c.shape
    assert bsz == 2, "the two batch elements are the two interleaved sequences"
    n = p_x.shape[1]
    nc, nb = lc // tb, n // tb
    nblk = nc + nb
    nslab = bd.shape[-1] // LANES
    pitch = tb + SUBLANES // 2

    def blk(d, pos, count):
        j = jnp.clip(pos, 0, count - 1)
        return jnp.where(d == 0, j, count - 1 - j)

    def u_spec(b, off, ctx):
        if ctx:
            return pl.BlockSpec((1, tb, s5w), lambda d, i: (b, blk(d, i + off, nc), 0))
        return pl.BlockSpec((1, tb, s5w), lambda d, i: (b, blk(d, i + off - nc, nb), 0))

    def y_spec(off):
        return pl.BlockSpec((1, tb, s5w), lambda d, i: (d, blk(d, i + off - nc, nb), 0))

    par = lambda d, i: (d, 0, 0, 0)
    buf = pltpu.VMEM((nslab, S5_CHUNKS * pitch, LANES), F32)
    state = pltpu.VMEM((nslab, SUBLANES, LANES), F32)
    return pl.pallas_call(
        functools.partial(_s5_kernel, tb=tb, pitch=pitch, nc=nc, nblk=nblk),
        grid=(2, nblk + 1),
        in_specs=[
            pl.BlockSpec((1, tb, s5w), lambda d, i: (0, blk(d, 0, nc), 0)),
            u_spec(0, 1, True), u_spec(0, 1, False),
            u_spec(1, 0, True), u_spec(1, 0, False),
            pl.BlockSpec((1, S5_CHUNKS, LANES, nslab * LANES), par, pipeline_mode=pl.Buffered(1)),
            pl.BlockSpec((1, S5_CHUNKS, nslab * LANES, LANES), par, pipeline_mode=pl.Buffered(1)),
            pl.BlockSpec((1, nslab // 2, SUBLANES, LANES), par),
            pl.BlockSpec((1, nslab // 2, SUBLANES, LANES), par),
        ],
        out_specs=[y_spec(0), y_spec(-1)],
        out_shape=[jax.ShapeDtypeStruct((2, n, s5w), BF16)] * 2,
        scratch_shapes=[buf, buf, buf, buf, state, state],
        compiler_params=_cparams(("arbitrary", "arbitrary")),
        name="s5",
    )(u_c, u_c, p_x, u_c, p_x, bd, cd, lam_r, lam_i)


def _s5_params(a_re, a_im, log_dt, b_re, b_im, c_re, c_im):
    g, p, h = b_re.shape
    gl = g // S5_CHUNKS
    dt = jnp.exp(log_dt.astype(F32))[:, None]
    mag = jnp.exp(a_re.astype(F32) * dt)
    lr, li = mag * jnp.cos(a_im.astype(F32) * dt), mag * jnp.sin(a_im.astype(F32) * dt)
    den = a_re * a_re + a_im * a_im
    qr = ((lr - 1.0) * a_re + li * a_im) / den
    qi = (li * a_re - (lr - 1.0) * a_im) / den
    bbr = qr[:, :, None] * b_re - qi[:, :, None] * b_im
    bbi = qr[:, :, None] * b_im + qi[:, :, None] * b_re
    eye = jnp.eye(gl, dtype=F32)

    def pack_b(m):
        return jnp.einsum("kgph,gq->kghqp", m.reshape(S5_CHUNKS, gl, p, h), eye).reshape(
            S5_CHUNKS, gl * h, gl * p)

    def pack_c(m):
        return jnp.einsum("kghp,gq->kgpqh", m.reshape(S5_CHUNKS, gl, h, p), eye).reshape(
            S5_CHUNKS, gl * p, gl * h)

    bd = jnp.concatenate([pack_b(bbr), pack_b(bbi)], axis=-1).astype(BF16)
    cd = jnp.concatenate([pack_c(c_re.astype(F32)), -pack_c(c_im.astype(F32))], axis=1).astype(BF16)

    def slabs(v):
        return v.reshape(S5_CHUNKS, -1, LANES).transpose(1, 0, 2)

    return bd, cd, slabs(lr), slabs(li)


def _filt_kernel(f_ref, w1_ref, b1_ref, q1_ref, w2_ref, b2_ref, q2_ref, w3_ref, dec_ref,
                 bias_ref, o_ref, *, tq):
    f = f_ref[...]
    h = jnp.sin(q1_ref[...] * (_dot_hi(f, w1_ref[...]) + b1_ref[...]))
    h = jnp.sin(q2_ref[...] * (_dot_hi(h, w2_ref[...]) + b2_ref[...]))
    h = jnp.concatenate([_dot_3pass(h, w3_ref[0]), _dot_3pass(h, w3_ref[1])], axis=0)
    c = bias_ref.shape[-1]
    hid = f.shape[-1] // 2
    t = jnp.concatenate([f[:, 0:1], f[:, hid:hid + 1]], axis=0)
    lag = pl.program_id(0) * tq + lax.broadcasted_iota(jnp.int32, (tq, 1), 0)
    fwd = h[:, :c] * jnp.exp(-t * jnp.abs(dec_ref[0:1, :]))
    bwd = h[:, c:] * jnp.exp(-t * jnp.abs(dec_ref[1:2, :]))
    fwd = jnp.where(lag == 0, fwd + bias_ref[...], fwd)
    bwd = jnp.where(lag == 0, 0.0, bwd)
    _store_padded(o_ref, 0, fwd)
    _store_padded(o_ref, 1, bwd)


def _filter_taps(feats, w1, b1, q1, w2, b2, q2, w3, decay, bias, tq=512):
    n_lat, fp = feats.shape
    hid = w2.shape[0]
    assert fp == hid and 2 * hid == LANES
    c = bias.shape[-1]
    rows = tq // DFT_N2 * GROUP_PITCH
    feats2 = feats.reshape(n_lat // tq, 2, tq // 2, fp).transpose(0, 2, 1, 3).reshape(
        n_lat // 2, 2 * fp)
    pair = lambda m: jnp.kron(jnp.eye(2, dtype=m.dtype), m)
    twice = lambda v: jnp.concatenate([v, v], axis=-1)
    zero = jnp.zeros_like(w3)
    w3_halves = jnp.stack([jnp.concatenate([w3, zero], axis=0),
                           jnp.concatenate([zero, w3], axis=0)])
    const = lambda r: (0, 0)
    return pl.pallas_call(
        functools.partial(_filt_kernel, tq=tq),
        grid=(n_lat // tq,),
        in_specs=[pl.BlockSpec((tq // 2, LANES), lambda r: (r, 0)),
                  pl.BlockSpec((LANES, LANES), const), pl.BlockSpec((1, LANES), const),
                  pl.BlockSpec((1, LANES), const),
                  pl.BlockSpec((LANES, LANES), const), pl.BlockSpec((1, LANES), const),
                  pl.BlockSpec((1, LANES), const),
                  pl.BlockSpec((2, LANES, 2 * c), lambda r: (0, 0, 0)),
                  pl.BlockSpec((2, c), const),
                  pl.BlockSpec((1, c), const)],
        out_specs=pl.BlockSpec((2, rows, c), lambda r: (0, r, 0)),
        out_shape=jax.ShapeDtypeStruct((2, n_lat // DFT_N2 * GROUP_PITCH, c), F32),
        compiler_params=_cparams(("parallel",)),
        name="hy_filter",
    )(feats2, pair(w1), twice(b1), twice(q1), pair(w2), twice(b2), twice(q2), w3_halves, decay,
      bias)


def _conv3(cur_ref, prev_ref, next_ref, w_ref, b_ref, i, last, tm):
    row = lax.broadcasted_iota(jnp.int32, (tm, 1), 0)
    cur = cur_ref[0].astype(F32)
    halo = prev_ref.shape[1]
    prev_row = jnp.where(i == 0, 0.0, prev_ref[0, halo - 1:halo, :].astype(F32))
    next_row = jnp.where(i == last, 0.0, next_ref[0, 0:1, :].astype(F32))
    up = jnp.where(row == 0, prev_row, pltpu.roll(cur, 1, 0))
    dn = jnp.where(row == tm - 1, next_row, pltpu.roll(cur, tm - 1, 0))
    w = w_ref[...]
    return up * w[0:1] + cur * w[1:2] + dn * w[2:3] + b_ref[...]


CONV_HALO = 16


def _conv3_specs(tm, c, n, col_block):
    hb = tm // CONV_HALO
    nhb = n // CONV_HALO
    return [pl.BlockSpec((1, tm, c), lambda b, i: (b, i, col_block)),
            pl.BlockSpec((1, CONV_HALO, c),
                         lambda b, i: (b, jnp.maximum(i * hb - 1, 0), col_block)),
            pl.BlockSpec((1, CONV_HALO, c),
                         lambda b, i: (b, jnp.minimum((i + 1) * hb, nhb - 1), col_block))]


def _sconv_kernel(x1_ref, x1p_ref, x1n_ref, v_ref, vp_ref, vn_ref, w1_ref, wv_ref, b1_ref, bv_ref,
                  u_ref, *, tm):
    i = pl.program_id(1)
    last = pl.num_programs(1) - 1
    x1 = _conv3(x1_ref, x1p_ref, x1n_ref, w1_ref, b1_ref, i, last, tm)
    v = _conv3(v_ref, vp_ref, vn_ref, wv_ref, bv_ref, i, last, tm)
    _store_padded(u_ref, 0, x1 * v)


def _short_conv(p_x, conv_w, conv_b, col0, c, tm):
    bsz, n, _ = p_x.shape
    cb0 = col0 // c
    part = lambda p, shape: pl.BlockSpec(shape, lambda b, i: (0, p))
    return pl.pallas_call(
        functools.partial(_sconv_kernel, tm=tm),
        grid=(bsz, n // tm),
        in_specs=(_conv3_specs(tm, c, n, cb0 + 1) + _conv3_specs(tm, c, n, cb0 + 2)
                  + [part(1, (3, c)), part(2, (3, c)), part(1, (1, c)), part(2, (1, c))]),
        out_specs=pl.BlockSpec((1, tm // DFT_N2 * GROUP_PITCH, c), lambda b, i: (b, i, 0)),
        out_shape=jax.ShapeDtypeStruct((bsz, n // DFT_N2 * GROUP_PITCH, c), F32),
        compiler_params=_cparams(("parallel", "parallel")),
        name="short_conv",
    )(p_x, p_x, p_x, p_x, p_x, p_x, conv_w, conv_w, conv_b, conv_b)


def _dft_tables(n1):
    n = n1 * DFT_N2
    k1 = jnp.arange(n1, dtype=jnp.int32)
    n2 = jnp.arange(DFT_N2, dtype=jnp.int32)

    def unit(ph, period):
        ang = (ph % period).astype(F32) * F32(2.0 * math.pi / period)
        return jnp.cos(ang), -jnp.sin(ang)

    ar, ai = unit(k1[:, None] * k1[None, :], n1)
    tr, ti = unit(n2[:, None] * k1[None, :], n)
    gr = ar[None] * tr[:, :, None] - ai[None] * ti[:, :, None]
    gi = ar[None] * ti[:, :, None] + ai[None] * tr[:, :, None]
    h = n1 // 2
    g1c = jnp.concatenate([jnp.concatenate([gr[:, :, :h], -gi[:, :, :h]], axis=2),
                           jnp.concatenate([gi[:, :, :h], gr[:, :, :h]], axis=2)], axis=1)
    m = jnp.arange(h, dtype=jnp.int32)

    def fold(g):
        up = jnp.take(g, n1 - 1 - m, axis=2)
        up0 = jnp.take(g[0], (n1 - m) % n1, axis=1) * (m != 0).astype(F32)
        return jnp.concatenate([g[:, :, :h], up.at[0].set(up0)], axis=2)

    g1f = jnp.concatenate([fold(gr), fold(gi)], axis=1)
    fr, fi = unit(n2[:, None] * n2[None, :], DFT_N2)
    f2 = jnp.concatenate([jnp.concatenate([fr, -fi], axis=1),
                          jnp.concatenate([fi, fr], axis=1)], axis=0)
    f2c = jnp.concatenate([jnp.concatenate([fr, fi], axis=1),
                           jnp.concatenate([-fi, fr], axis=1)], axis=0) / n
    return g1c.astype(BF16), g1f.astype(BF16), f2.astype(BF16), f2c.astype(BF16)


def _stage1(gather, g_ref, a_ref, cn, n1, pitch):
    def trip(chunk, carry):
        for q in range(cn):
            n2 = chunk * cn + q
            a = _dot(g_ref[n2], gather(n2).astype(BF16))
            a_ref[0, pl.ds(n2, n1, stride=pitch), :] = a[:n1]
            a_ref[1, pl.ds(n2, n1, stride=pitch), :] = a[n1:]
        return carry
    lax.fori_loop(0, DFT_N2 // cn, trip, 0)


def _a_rows(chunk, ck, q, pitch):
    return pl.ds(pl.multiple_of(chunk * (ck * pitch), SUBLANES) + q * pitch, DFT_N2)


def _load_a_chunk(a_ref, chunk, ck, pitch):
    tiles = []
    for q in range(ck):
        rows = _a_rows(chunk, ck, q, pitch)
        tiles.append(jnp.concatenate([a_ref[0, rows, :], a_ref[1, rows, :]], axis=0).astype(BF16))
    return jnp.concatenate(tiles, axis=1)


def _spec_kernel(t_ref, g_ref, f2_ref, o_ref, a_ref, *, n1, cn, ck, pitch):
    s = pl.program_id(1)
    h = n1 // 2

    @pl.when(s == 0)
    def _():
        def gather(n2):
            back = (DFT_N2 - n2) % DFT_N2
            return jnp.concatenate([t_ref[0, pl.ds(n2, h, stride=GROUP_PITCH), :],
                                    t_ref[1, pl.ds(back, h, stride=GROUP_PITCH), :]], axis=0)
        _stage1(gather, g_ref, a_ref, cn, n1, pitch)

    @pl.when(s >= 1)
    def _():
        o_ref[0] = _dot(f2_ref[...], _load_a_chunk(a_ref, s - 1, ck, pitch)).astype(o_ref.dtype)


def _conv_kernel(u_ref, g1_ref, kf_ref, f2_ref, f2c_ref, y_ref, a_ref,
                 *, n1, cn, ck, pitch, np2):
    s = pl.program_id(1)
    h = n1 // 2

    @pl.when(s == 0)
    def _():
        y_ref[...] = jnp.zeros_like(y_ref)

        def gather(n2):
            idx = pl.ds(n2, h, stride=GROUP_PITCH)
            return jnp.concatenate([u_ref[0, idx, :], u_ref[1, idx, :]], axis=0)
        _stage1(gather, g1_ref, a_ref, cn, n1, pitch)

    @pl.when(jnp.logical_and(s >= 1, s <= np2))
    def _():
        chunk = s - 1
        x = _dot(f2_ref[...], _load_a_chunk(a_ref, chunk, ck, pitch))
        xr, xi = x[:DFT_N2], x[DFT_N2:]
        kr, ki = kf_ref[0, :DFT_N2, :].astype(F32), kf_ref[0, DFT_N2:, :].astype(F32)
        y = jnp.concatenate([xr * kr - xi * ki, xr * ki + xi * kr], axis=0)
        b = _dot(f2c_ref[...], y.astype(BF16))
        for q in range(ck):
            rows = _a_rows(chunk, ck, q, pitch)
            a_ref[0, rows, :] = b[:DFT_N2, q * LANES:(q + 1) * LANES]
            a_ref[1, rows, :] = b[DFT_N2:, q * LANES:(q + 1) * LANES]

    @pl.when(s > np2)
    def _():
        def trip(chunk, carry):
            for q in range(cn):
                n2 = chunk * cn + q
                idx = pl.ds(n2, n1, stride=pitch)
                b = jnp.concatenate([a_ref[0, idx, :], a_ref[1, idx, :]], axis=0)
                y = lax.dot_general(g1_ref[n2], b.astype(BF16), (((0,), (0,)), ((), ())),
                                    preferred_element_type=F32)
                out = pl.ds(n2, h, stride=GROUP_PITCH)
                y_ref[0, out, :] = y[:h]
                y_ref[1, out, :] = y[h:]
            return carry
        lax.fori_loop(0, DFT_N2 // cn, trip, 0)


DFT_UNROLL = 32
A_PITCH = DFT_N2 + SUBLANES


def _filter_spectrum(taps, g1f, f2):
    _, rows, c = taps.shape
    n1 = 2 * rows // GROUP_PITCH
    ck = min(32, n1)
    np2 = n1 // ck
    pitch = A_PITCH
    return pl.pallas_call(
        functools.partial(_spec_kernel, n1=n1, cn=DFT_UNROLL, ck=ck, pitch=pitch),
        grid=(c // LANES, 1 + np2),
        in_specs=[pl.BlockSpec((2, rows, LANES), lambda j, s: (0, 0, j),
                               pipeline_mode=pl.Buffered(1)),
                  pl.BlockSpec((DFT_N2, 2 * n1, n1), lambda j, s: (0, 0, 0),
                               pipeline_mode=pl.Buffered(1)),
                  pl.BlockSpec((2 * DFT_N2, 2 * DFT_N2), lambda j, s: (0, 0))],
        out_specs=pl.BlockSpec((1, 2 * DFT_N2, ck * LANES),
                               lambda j, s: (j, 0, jnp.maximum(s - 1, 0))),
        out_shape=jax.ShapeDtypeStruct((c // LANES, 2 * DFT_N2, n1 * LANES), BF16),
        scratch_shapes=[pltpu.VMEM((2, n1 * pitch, LANES), F32)],
        compiler_params=_cparams(("parallel", "arbitrary")),
        name="hy_spectrum",
    )(taps, g1f, f2)


def _long_conv(u, kf, g1c, f2, f2c):
    bsz, rows, c = u.shape
    assert bsz == 2, "the two batch elements are packed as one complex signal"
    n1 = 2 * rows // GROUP_PITCH
    ck = min(16, n1)
    np2 = n1 // ck
    pitch = A_PITCH
    once = dict(pipeline_mode=pl.Buffered(1))
    return pl.pallas_call(
        functools.partial(_conv_kernel, n1=n1, cn=DFT_UNROLL, ck=ck, pitch=pitch, np2=np2),
        grid=(c // LANES, np2 + 2),
        in_specs=[
            pl.BlockSpec((2, rows, LANES), lambda j, s: (0, 0, j), **once),
            pl.BlockSpec((DFT_N2, 2 * n1, n1), lambda j, s: (0, 0, 0), **once),
            pl.BlockSpec((1, 2 * DFT_N2, ck * LANES),
                         lambda j, s: (j, 0, jnp.clip(s - 1, 0, np2 - 1))),
            pl.BlockSpec((2 * DFT_N2, 2 * DFT_N2), lambda j, s: (0, 0)),
            pl.BlockSpec((2 * DFT_N2, 2 * DFT_N2), lambda j, s: (0, 0)),
        ],
        out_specs=pl.BlockSpec((2, rows, LANES), lambda j, s: (0, 0, j), **once),
        out_shape=jax.ShapeDtypeStruct((2, rows, c), F32),
        scratch_shapes=[pltpu.VMEM((2, n1 * pitch, LANES), F32)],
        compiler_params=_cparams(("parallel", "arbitrary")),
        name="hy_conv",
    )(u, g1c, kf, f2, f2c)


def _mixer_kernel(ya_ref, yb_ref, u_ref, h0_ref, h0p_ref, h0n_ref, cw_ref, cb_ref, yc_ref, gs_ref,
                  gh_ref, x_ref, rows_ref, cols_ref, d_ref, gate_ref, gf_ref, shf_ref, scf_ref,
                  wglu_ref, why_ref, wout_ref, o_ref, hf_ref, *, tm, cw):
    first = pl.program_id(0) == 0
    y = (jnp.where(first, ya_ref[0], yb_ref[0]).astype(F32)
         + jnp.where(first, ya_ref[1], yb_ref[1]).astype(F32)
         + u_ref[0].astype(F32) * d_ref[...])
    h = jax.nn.gelu(y, approximate=True).astype(BF16)
    y_conv = jnp.concatenate(
        [yc_ref[0, g * GROUP_PITCH:g * GROUP_PITCH + DFT_N2, :] for g in range(tm // DFT_N2)],
        axis=0)
    x0 = _conv3(h0_ref, h0p_ref, h0n_ref, cw_ref, cb_ref, pl.program_id(1),
                pl.num_programs(1) - 1, tm)
    hx = (x0 * y_conv).astype(BF16)
    d = o_ref.shape[-1]
    merged = []
    for c in range(d // cw):
        cols = slice(c * cw, (c + 1) * cw)
        glu_v = _dot(h, wglu_ref[:, cols])
        glu_g = _dot(h, wglu_ref[:, d + c * cw:d + (c + 1) * cw])
        branch_hy = _dot(hx, why_ref[:, cols])
        m = (jax.nn.sigmoid(gs_ref[0, :, cols].astype(F32)) * (glu_v * jax.nn.sigmoid(glu_g))
             + jax.nn.sigmoid(gh_ref[0, :, cols].astype(F32)) * branch_hy)
        merged.append(m.astype(BF16))
    merged = jnp.concatenate(merged, axis=-1)
    dh = d // 2
    rp, cp = _pos_block(rows_ref[0], cols_ref[...], tm)
    for c in range(d // cw):
        cols = slice(c * cw, (c + 1) * cw)
        pos = rp[:, cols] if (c + 1) * cw <= dh else cp[:, c * cw - dh:(c + 1) * cw - dh]
        o_ref[0, :, cols] = (x_ref[0, :, cols] + pos
                             + gate_ref[0, :, cols] * _dot(merged, wout_ref[:, cols]))
    hf_ref[0] = _rms_mod(o_ref[0], gf_ref[...], shf_ref[0], scf_ref[0]).astype(BF16)


def _mixer(y_ssm, p_x, conv_w, conv_b, y_conv, x, pos_tabs, s5_d, gate, g_ffn, shift_ffn, scale_ffn,
           w_glu, w_hy_out, w_out, s5w, gate_col0, tm):
    bsz, n, d = x.shape
    assert bsz == 2, "the S5 readouts arrive as one array per batch element"
    hyw = w_hy_out.shape[0]
    rows_tab, cols_tab = pos_tabs
    nr = tm // GRID_W
    dh = d // 2
    ni = n // tm
    g0 = gate_col0 // d
    cw = min(512, dh)
    resident = dict(pipeline_mode=pl.Buffered(1))
    tok = lambda width, col: pl.BlockSpec((1, tm, width), lambda b, i: (b, i, col))
    return pl.pallas_call(
        functools.partial(_mixer_kernel, tm=tm, cw=cw),
        grid=(bsz, ni),
        in_specs=[
            pl.BlockSpec((2, tm, s5w), lambda b, i: (0, jnp.where(b == 0, i, ni - 1), 0)),
            pl.BlockSpec((2, tm, s5w), lambda b, i: (0, jnp.where(b == 0, 0, i), 0)),
            tok(s5w, 0),
            *_conv3_specs(tm, hyw, n, s5w // hyw),
            pl.BlockSpec((3, hyw), lambda b, i: (0, 0)),
            pl.BlockSpec((1, hyw), lambda b, i: (0, 0)),
            pl.BlockSpec((1, tm // DFT_N2 * GROUP_PITCH, hyw), lambda b, i: (b, i, 0)),
            tok(d, g0), tok(d, g0 + 1), tok(d, 0),
            pl.BlockSpec((1, nr, dh), lambda b, i: (i, 0, 0)),
            pl.BlockSpec((GRID_W, dh), lambda b, i: (0, 0)),
            pl.BlockSpec((1, s5w), lambda b, i: (0, 0)),
            pl.BlockSpec((1, 1, d), lambda b, i: (b, 0, 0)),
            pl.BlockSpec((1, d), lambda b, i: (0, 0)),
            pl.BlockSpec((1, 1, d), lambda b, i: (b, 0, 0)),
            pl.BlockSpec((1, 1, d), lambda b, i: (b, 0, 0)),
            pl.BlockSpec((s5w, 2 * d), lambda b, i: (0, 0), **resident),
            pl.BlockSpec((hyw, d), lambda b, i: (0, 0), **resident),
            pl.BlockSpec((d, d), lambda b, i: (0, 0), **resident),
        ],
        out_specs=[pl.BlockSpec((1, tm, d), lambda b, i: (b, i, 0)),
                   pl.BlockSpec((1, tm, d), lambda b, i: (b, i, 0))],
        out_shape=[jax.ShapeDtypeStruct((bsz, n, d), F32),
                   jax.ShapeDtypeStruct((bsz, n, d), BF16)],
        compiler_params=_cparams(("arbitrary", "arbitrary")),
        name="mixer",
    )(y_ssm[0], y_ssm[1], p_x, p_x, p_x, p_x, conv_w, conv_b, y_conv, p_x, p_x, x,
      rows_tab.reshape(-1, nr, dh), cols_tab, s5_d, gate, g_ffn, shift_ffn, scale_ffn,
      w_glu, w_hy_out, w_out)


def _ffn_kernel(x_ref, h_ref, gate_ref, wa_ref, wb_ref, wo_ref, nf_ref, o_ref, acc_ref):
    j = pl.program_id(2)

    @pl.when(j == 0)
    def _():
        acc_ref[...] = jnp.zeros_like(acc_ref)

    h = h_ref[0]
    act = _silu(_dot(h, wa_ref[...])) * _dot(h, wb_ref[...])
    acc_ref[...] += _dot(act.astype(BF16), wo_ref[...])

    @pl.when(j == pl.num_programs(2) - 1)
    def _():
        xo = x_ref[0] + gate_ref[0] * acc_ref[...]
        o_ref[0] = xo * lax.rsqrt(jnp.mean(xo * xo, axis=-1, keepdims=True) + EPS) * nf_ref[...]


def _ffn(x, h, gate, w_in, w_out, norm_f, tm, tf):
    bsz, n, d = x.shape
    dff = w_out.shape[0]
    nj = dff // tf
    tok = pl.BlockSpec((1, tm, d), lambda b, i, j: (b, i, 0))
    return pl.pallas_call(
        _ffn_kernel,
        grid=(bsz, n // tm, nj),
        in_specs=[tok, tok,
                  pl.BlockSpec((1, 1, d), lambda b, i, j: (b, 0, 0)),
                  pl.BlockSpec((d, tf), lambda b, i, j: (0, j)),
                  pl.BlockSpec((d, tf), lambda b, i, j: (0, nj + j)),
                  pl.BlockSpec((tf, d), lambda b, i, j: (j, 0)),
                  pl.BlockSpec((1, d), lambda b, i, j: (0, 0))],
        out_specs=pl.BlockSpec((1, tm, d), lambda b, i, j: (b, i, 0)),
        out_shape=jax.ShapeDtypeStruct((bsz, n, d), F32),
        scratch_shapes=[pltpu.VMEM((tm, d), F32)],
        compiler_params=_cparams(("parallel", "parallel", "arbitrary")),
        name="ffn",
    )(x, h, gate, w_in, w_in, w_out, norm_f)


def _pos_tables(n_rows, d):
    quarter = d // 4
    omega = 10000.0 ** (-jnp.arange(quarter, dtype=F32) / quarter)
    ar = jnp.arange(n_rows, dtype=F32)[:, None] * omega
    ac = jnp.arange(GRID_W, dtype=F32)[:, None] * omega
    return (jnp.concatenate([jnp.sin(ar), jnp.cos(ar)], axis=-1),
            jnp.concatenate([jnp.sin(ac), jnp.cos(ac)], axis=-1))


def _filter_features(n_lat, width):
    pos = jnp.arange(n_lat, dtype=F32)
    t = pos / float(max(n_lat - 1, 1))
    w = 2.0 * math.pi * pos / n_lat
    bands = jnp.linspace(1e-4, HY_BANDS - 1, HY_BANDS, dtype=F32)
    feats = jnp.concatenate([t[:, None], jnp.cos(w[:, None] * bands), -jnp.sin(w[:, None] * bands)],
                            axis=-1)
    return jnp.pad(feats, ((0, 0), (0, width - feats.shape[1])))


def kernel(x, c, ctx, c_ctx, w_ada, b_ada, norm_mix, w_in, s5_a_re, s5_a_im, s5_log_dt,
           s5_b_re, s5_b_im, s5_c_re, s5_c_im, s5_d, w_glu, hy_conv_w, hy_conv_b,
           hy_f1_w, hy_f1_b, hy_f1_freq, hy_f2_w, hy_f2_b, hy_f2_freq, hy_f3_w, hy_decay,
           hy_bias, w_hy_out, w_out, norm_ffn, w_ffn_in, w_ffn_out, norm_f):
    bsz, n_lat, d = x.shape
    depth = w_ada.shape[0]
    assert depth == 1, "the context stream is only advanced for the single-layer trunk"
    l = 0
    s5w = s5_d.shape[-1]
    hyw = hy_bias.shape[-1]
    n_ctx = ctx.shape[1]
    tm = 512
    tn = min(1024, d)
    tb = 256
    assert n_ctx % tb == 0 and n_lat % tm == 0 and s5w == S5_CHUNKS * LANES

    pos_tabs = _pos_tables(n_lat // GRID_W, d)

    cond_t = jnp.zeros((d, SUBLANES), F32).at[:, :bsz].set(c.T).at[:, bsz].set(c_ctx)
    ada = _ada(cond_t, w_ada[l], b_ada[l][None], bsz + 1, tn=min(1024, d))

    def vec(row0, rows, part):
        v = ada[row0:row0 + rows, part * d:(part + 1) * d]
        return jnp.broadcast_to(v, (bsz, d))[:, None, :]

    shift_mix, scale_mix, gate_mix = vec(0, bsz, 0), vec(0, bsz, 1), vec(0, bsz, 2)
    shift_ffn, scale_ffn, gate_ffn = vec(0, bsz, 3), vec(0, bsz, 4), vec(0, bsz, 5)
    cshift_mix, cscale_mix = vec(bsz, 1, 0), vec(bsz, 1, 1)

    w_in_b = w_in[l].astype(BF16)
    g_mix = norm_mix[l][None]
    u_c = _ctx_proj(ctx, g_mix, cshift_mix, cscale_mix, w_in_b, s5w, tb)
    p_x = _in_proj_latent(x, pos_tabs, g_mix, shift_mix, scale_mix, w_in_b, min(1024, n_lat), tn)

    packed = [_s5_params(s5_a_re[l, k], s5_a_im[l, k], s5_log_dt[l, k], s5_b_re[l, k],
                         s5_b_im[l, k], s5_c_re[l, k], s5_c_im[l, k]) for k in range(2)]
    bd, cd, lam_r, lam_i = (jnp.stack(t) for t in zip(*packed))
    y_ssm = _s5(u_c, p_x, bd, cd, lam_r, lam_i, s5w, tb)

    fp = 64
    feats = _filter_features(n_lat, fp)
    w1 = jnp.pad(hy_f1_w[l], ((0, fp - hy_f1_w.shape[1]), (0, 0)))
    taps = _filter_taps(feats, w1, hy_f1_b[l][None], hy_f1_freq[l][None], hy_f2_w[l],
                        hy_f2_b[l][None], hy_f2_freq[l][None], hy_f3_w[l], hy_decay[l],
                        hy_bias[l][None])
    g1c, g1f, f2, f2c = _dft_tables(2 * n_lat // DFT_N2)
    kf = _filter_spectrum(taps, g1f, f2)
    u_hy = _short_conv(p_x, hy_conv_w[l], hy_conv_b[l][None], s5w, hyw, tm)
    y_conv = _long_conv(u_hy, kf, g1c, f2, f2c)

    x1, h_ffn = _mixer(y_ssm, p_x, hy_conv_w[l], hy_conv_b[l][None], y_conv, x, pos_tabs,
                       s5_d[l][None], gate_mix, norm_ffn[l][None], shift_ffn, scale_ffn,
                       w_glu[l].astype(BF16), w_hy_out[l].astype(BF16), w_out[l].astype(BF16),
                       s5w, s5w + 3 * hyw, 256)

    return _ffn(x1, h_ffn, gate_ffn, w_ffn_in[l].astype(BF16), w_ffn_out[l].astype(BF16),
                norm_f[None], tm, 512)
```

```python
import functools
import math

import jax
import jax.numpy as jnp
from jax import lax
from jax.experimental import pallas as pl
from jax.experimental.pallas import tpu as pltpu

F32 = jnp.float32
BF16 = jnp.bfloat16
HIGHEST = lax.Precision.HIGHEST

GRID_W = 64
N_ADA = 6
EPS = 1e-6
HY_BANDS = 16
LANES = 128
SUBLANES = 8
DFT_N2 = LANES
GROUP_PITCH = DFT_N2 + SUBLANES // 2
VMEM_LIMIT = 60 * 1024 * 1024


def _cparams(sem):
    return pltpu.CompilerParams(dimension_semantics=sem, vmem_limit_bytes=VMEM_LIMIT)


def _dot(a, b):
    return jnp.dot(a, b, preferred_element_type=F32)


def _dot_hi(a, b):
    return jnp.dot(a, b, preferred_element_type=F32, precision=HIGHEST)


def _dot_3pass(a, b):
    a_hi, b_hi = a.astype(BF16), b.astype(BF16)
    a_lo = (a - a_hi.astype(F32)).astype(BF16)
    b_lo = (b - b_hi.astype(F32)).astype(BF16)
    return _dot(a_hi, b_hi) + (_dot(a_hi, b_lo) + _dot(a_lo, b_hi))


def _silu(x):
    return x * jax.nn.sigmoid(x)


def _rms_mod(x, g, shift, scale):
    y = x * lax.rsqrt(jnp.mean(x * x, axis=-1, keepdims=True) + EPS)
    return (y * g) * (1.0 + scale) + shift


def _pos_block(rows, cols, tm):
    nr = tm // GRID_W
    dh = rows.shape[-1]
    rp = jnp.broadcast_to(rows[:, None, :], (nr, GRID_W, dh)).reshape(tm, dh)
    cp = jnp.broadcast_to(cols[None], (nr, GRID_W, dh)).reshape(tm, dh)
    return rp, cp


def _store_padded(ref, lead, val):
    pad = jnp.zeros((GROUP_PITCH - DFT_N2, val.shape[-1]), val.dtype)
    for g in range(val.shape[0] // DFT_N2):
        ref[lead, g * GROUP_PITCH:g * GROUP_PITCH + DFT_N2, :] = val[g * DFT_N2:(g + 1) * DFT_N2]
        ref[lead, g * GROUP_PITCH + DFT_N2:(g + 1) * GROUP_PITCH, :] = pad


def _ada_kernel(ct_ref, w_ref, b_ref, o_ref, *, n_vec):
    sc = _silu(ct_ref[...])
    w = w_ref[...]
    rows = [jnp.sum(w * sc[:, r:r + 1], axis=0, keepdims=True) for r in range(n_vec)]
    rows += [jnp.zeros_like(rows[0])] * (SUBLANES - n_vec)
    o_ref[...] = jnp.concatenate(rows, axis=0) + b_ref[...]


def _ada(cond_t, w, b, n_vec, tn=1024):
    d, n = w.shape
    return pl.pallas_call(
        functools.partial(_ada_kernel, n_vec=n_vec),
        grid=(n // tn,),
        in_specs=[pl.BlockSpec((d, SUBLANES), lambda j: (0, 0)),
                  pl.BlockSpec((d, tn), lambda j: (0, j)),
                  pl.BlockSpec((1, tn), lambda j: (0, j))],
        out_specs=pl.BlockSpec((SUBLANES, tn), lambda j: (0, j)),
        out_shape=jax.ShapeDtypeStruct((SUBLANES, n), F32),
        compiler_params=_cparams(("arbitrary",)),
        name="ada",
    )(cond_t, w, b)


def _ctx_kernel(x_ref, g_ref, sh_ref, sc_ref, w_ref, o_ref):
    h = _rms_mod(x_ref[0], g_ref[...], sh_ref[0], sc_ref[0]).astype(BF16)
    o_ref[0] = _dot(h, w_ref[...]).astype(o_ref.dtype)


def _ctx_proj(x, g, shift, scale, w, n_cols, tm):
    bsz, n, d = x.shape
    vec = pl.BlockSpec((1, 1, d), lambda b, i: (b, 0, 0))
    return pl.pallas_call(
        _ctx_kernel,
        grid=(bsz, n // tm),
        in_specs=[pl.BlockSpec((1, tm, d), lambda b, i: (b, i, 0)),
                  pl.BlockSpec((1, d), lambda b, i: (0, 0)), vec, vec,
                  pl.BlockSpec((d, n_cols), lambda b, i: (0, 0))],
        out_specs=pl.BlockSpec((1, tm, n_cols), lambda b, i: (b, i, 0)),
        out_shape=jax.ShapeDtypeStruct((bsz, n, n_cols), BF16),
        compiler_params=_cparams(("parallel", "parallel")),
        name="ctx_proj",
    )(x, g, shift, scale, w)


PRO_CHUNKS = 8


def _next_block(b, i, ni, nb):
    t = jnp.minimum(b * ni + i + 1, nb * ni - 1)
    return t // ni, t % ni


def _in_pipe_kernel(x0_ref, rows0_ref, sh0_ref, sc0_ref, xn_ref, rowsn_ref, shn_ref, scn_ref,
                    cols_ref, g_ref, w_ref, o_ref, h_ref, *, tm):
    b, i, j = pl.program_id(0), pl.program_id(1), pl.program_id(2)
    par = (b * pl.num_programs(1) + i) % 2
    cr = tm // PRO_CHUNKS
    dh = cols_ref.shape[-1]

    @pl.when(jnp.logical_and(jnp.logical_and(b == 0, i == 0), j == 0))
    def _():
        x = x0_ref[0] + jnp.concatenate(_pos_block(rows0_ref[...], cols_ref[...], tm), axis=-1)
        h_ref[0] = _rms_mod(x, g_ref[...], sh0_ref[0], sc0_ref[0]).astype(BF16)

    o_ref[0] = _dot(h_ref[par], w_ref[...]).astype(o_ref.dtype)

    c = jnp.minimum(j, PRO_CHUNKS - 1)
    rows = pl.ds(pl.multiple_of(c * cr, cr), cr)
    gpc = cr // GRID_W
    rp = jnp.concatenate([jnp.broadcast_to(rowsn_ref[pl.ds(c * gpc + r, 1), :], (GRID_W, dh))
                          for r in range(gpc)], axis=0)
    cp = jnp.concatenate([cols_ref[...]] * gpc, axis=0)
    xc = xn_ref[0, rows, :] + jnp.concatenate([rp, cp], axis=-1)
    h_ref[1 - par, rows, :] = _rms_mod(xc, g_ref[...], shn_ref[0], scn_ref[0]).astype(BF16)


def _in_proj_latent(x, pos_tabs, g, shift, scale, w, tm, tn):
    bsz, n, d = x.shape
    n_cols = w.shape[1]
    rows_tab, cols_tab = pos_tabs
    dh = d // 2
    ni = n // tm
    nr = tm // GRID_W
    assert (tm // PRO_CHUNKS) % GRID_W == 0 and n_cols // tn >= PRO_CHUNKS
    nxt = lambda b, i: _next_block(b, i, ni, bsz)
    once = dict(pipeline_mode=pl.Buffered(1))
    return pl.pallas_call(
        functools.partial(_in_pipe_kernel, tm=tm),
        grid=(bsz, ni, n_cols // tn),
        in_specs=[
            pl.BlockSpec((1, tm, d), lambda b, i, j: (0, 0, 0), **once),
            pl.BlockSpec((nr, dh), lambda b, i, j: (0, 0)),
            pl.BlockSpec((1, 1, d), lambda b, i, j: (0, 0, 0)),
            pl.BlockSpec((1, 1, d), lambda b, i, j: (0, 0, 0)),
            pl.BlockSpec((1, tm, d), lambda b, i, j: nxt(b, i) + (0,)),
            pl.BlockSpec((nr, dh), lambda b, i, j: (nxt(b, i)[1], 0)),
            pl.BlockSpec((1, 1, d), lambda b, i, j: (nxt(b, i)[0], 0, 0)),
            pl.BlockSpec((1, 1, d), lambda b, i, j: (nxt(b, i)[0], 0, 0)),
            pl.BlockSpec((GRID_W, dh), lambda b, i, j: (0, 0)),
            pl.BlockSpec((1, d), lambda b, i, j: (0, 0)),
            pl.BlockSpec((d, tn), lambda b, i, j: (0, j)),
        ],
        out_specs=pl.BlockSpec((1, tm, tn), lambda b, i, j: (b, i, j)),
        out_shape=jax.ShapeDtypeStruct((bsz, n, n_cols), BF16),
        scratch_shapes=[pltpu.VMEM((2, tm, d), BF16)],
        compiler_params=_cparams(("arbitrary", "arbitrary", "arbitrary")),
        name="in_proj",
    )(x, rows_tab, shift, scale, x, rows_tab, shift, scale, cols_tab, g, w)


S5_CHUNKS = 8


def _s5_rows(k, tb, pitch):
    pad = pitch - tb
    if (k * pitch) % SUBLANES == 0:
        return k * pitch, k * pitch + tb
    return k * pitch - pad, k * pitch + tb + pad


def _s5_fill(buf, lhs_of, bd_ref, k, tb, pitch):
    lhs = lhs_of(k)
    lo, hi = _s5_rows(k, tb, pitch)
    if lo != k * pitch:
        z = jnp.zeros((k * pitch - lo, LANES), F32)
        lhs = jnp.concatenate([z, lhs.astype(F32), z], axis=0).astype(BF16)
    bu = _dot(lhs, bd_ref[0, k])
    for m in range(buf.shape[0]):
        buf[m, lo:hi, :] = bu[:, m * LANES:(m + 1) * LANES]


def _s5_pass(d, bu_scan, st_scan, carry_ref, st_mm, bu_mm, lhs_of, y_ref, bd_ref, cd_ref,
             lam_r, lam_i, tb, pitch):
    nslab = bu_scan.shape[0]
    half = nslab // 2
    per = tb // S5_CHUNKS
    sr = [carry_ref[m] for m in range(half)]
    si = [carry_ref[half + m] for m in range(half)]

    def scan_steps(q0, q1):
        for q in range(q0, q1):
            idx = pl.ds(jnp.where(d == 0, q, tb - 1 - q), SUBLANES, stride=pitch)
            for m in range(half):
                nr = lam_r[m] * sr[m] - lam_i[m] * si[m] + bu_scan[m, idx, :]
                ni = lam_r[m] * si[m] + lam_i[m] * sr[m] + bu_scan[half + m, idx, :]
                st_scan[m, idx, :] = nr
                st_scan[half + m, idx, :] = ni
                sr[m], si[m] = nr, ni

    for k in range(S5_CHUNKS):
        lo, hi = _s5_rows(k, tb, pitch)
        sk = jnp.concatenate([st_mm[m, lo:hi, :] for m in range(nslab)], axis=-1)
        yk = _dot(sk.astype(BF16), cd_ref[0, k])
        y_ref[0, :, k * LANES:(k + 1) * LANES] = yk[k * pitch - lo:k * pitch - lo + tb].astype(
            y_ref.dtype)
        scan_steps(k * per, k * per + per // 2)
        _s5_fill(bu_mm, lhs_of, bd_ref, k, tb, pitch)
        scan_steps(k * per + per // 2, (k + 1) * per)
    for m in range(half):
        carry_ref[m] = sr[m]
        carry_ref[half + m] = si[m]


def _s5_kernel(uc0_ref, uca_ref, uxa_ref, ucb_ref, uxb_ref, bd_ref, cd_ref, lr_ref, li_ref,
               ya_ref, yb_ref, a_bu, a_st, b_bu, b_st, ca_ref, cb_ref, *, tb, pitch, nc, nblk):
    d = pl.program_id(0)
    i = pl.program_id(1)
    half = a_bu.shape[0] // 2
    lam_r = [lr_ref[0, m] for m in range(half)]
    lam_i = [li_ref[0, m] for m in range(half)]

    @pl.when(i == 0)
    def _():
        ca_ref[...] = jnp.zeros_like(ca_ref)
        cb_ref[...] = jnp.zeros_like(cb_ref)
        a_st[...] = jnp.zeros_like(a_st)
        b_st[...] = jnp.zeros_like(b_st)
        for k in range(S5_CHUNKS):
            _s5_fill(a_bu, lambda k: uc0_ref[0, :, k * LANES:(k + 1) * LANES], bd_ref, k, tb, pitch)

    def lhs(uc_ref, ux_ref, pos):
        def of(k):
            cols = slice(k * LANES, (k + 1) * LANES)
            return jnp.where(pos < nc, uc_ref[0, :, cols], ux_ref[0, :, cols])
        return of

    _s5_pass(d, a_bu, a_st, ca_ref, b_st, b_bu, lhs(ucb_ref, uxb_ref, i), yb_ref,
             bd_ref, cd_ref, lam_r, lam_i, tb, pitch)

    @pl.when(i < nblk)
    def _():
        _s5_pass(d, b_bu, b_st, cb_ref, a_st, a_bu, lhs(uca_ref, uxa_ref, i + 1), ya_ref,
                 bd_ref, cd_ref, lam_r, lam_i, tb, pitch)


def _s5(u_c, p_x, bd, cd, lam_r, lam_i, s5w, tb):
    bsz, lc, _ = u_c.shape
    assert bsz == 2, "the two batch elements are the two interleaved sequences"
    n = p_x.shape[1]
    nc, nb = lc // tb, n // tb
    nblk = nc + nb
    nslab = bd.shape[-1] // LANES
    pitch = tb + SUBLANES // 2

    def blk(d, pos, count):
        j = jnp.clip(pos, 0, count - 1)
        return jnp.where(d == 0, j, count - 1 - j)

    def u_spec(b, off, ctx):
        if ctx:
            return pl.BlockSpec((1, tb, s5w), lambda d, i: (b, blk(d, i + off, nc), 0))
        return pl.BlockSpec((1, tb, s5w), lambda d, i: (b, blk(d, i + off - nc, nb), 0))

    def y_spec(off):
        return pl.BlockSpec((1, tb, s5w), lambda d, i: (d, blk(d, i + off - nc, nb), 0))

    par = lambda d, i: (d, 0, 0, 0)
    buf = pltpu.VMEM((nslab, S5_CHUNKS * pitch, LANES), F32)
    state = pltpu.VMEM((nslab, SUBLANES, LANES), F32)
    return pl.pallas_call(
        functools.partial(_s5_kernel, tb=tb, pitch=pitch, nc=nc, nblk=nblk),
        grid=(2, nblk + 1),
        in_specs=[
            pl.BlockSpec((1, tb, s5w), lambda d, i: (0, blk(d, 0, nc), 0)),
            u_spec(0, 1, True), u_spec(0, 1, False),
            u_spec(1, 0, True), u_spec(1, 0, False),
            pl.BlockSpec((1, S5_CHUNKS, LANES, nslab * LANES), par, pipeline_mode=pl.Buffered(1)),
            pl.BlockSpec((1, S5_CHUNKS, nslab * LANES, LANES), par, pipeline_mode=pl.Buffered(1)),
            pl.BlockSpec((1, nslab // 2, SUBLANES, LANES), par),
            pl.BlockSpec((1, nslab // 2, SUBLANES, LANES), par),
        ],
        out_specs=[y_spec(0), y_spec(-1)],
        out_shape=[jax.ShapeDtypeStruct((2, n, s5w), BF16)] * 2,
        scratch_shapes=[buf, buf, buf, buf, state, state],
        compiler_params=_cparams(("arbitrary", "arbitrary")),
        name="s5",
    )(u_c, u_c, p_x, u_c, p_x, bd, cd, lam_r, lam_i)


def _s5_params(a_re, a_im, log_dt, b_re, b_im, c_re, c_im):
    g, p, h = b_re.shape
    gl = g // S5_CHUNKS
    dt = jnp.exp(log_dt.astype(F32))[:, None]
    mag = jnp.exp(a_re.astype(F32) * dt)
    lr, li = mag * jnp.cos(a_im.astype(F32) * dt), mag * jnp.sin(a_im.astype(F32) * dt)
    den = a_re * a_re + a_im * a_im
    qr = ((lr - 1.0) * a_re + li * a_im) / den
    qi = (li * a_re - (lr - 1.0) * a_im) / den
    bbr = qr[:, :, None] * b_re - qi[:, :, None] * b_im
    bbi = qr[:, :, None] * b_im + qi[:, :, None] * b_re
    eye = jnp.eye(gl, dtype=F32)

    def pack_b(m):
        return jnp.einsum("kgph,gq->kghqp", m.reshape(S5_CHUNKS, gl, p, h), eye).reshape(
            S5_CHUNKS, gl * h, gl * p)

    def pack_c(m):
        return jnp.einsum("kghp,gq->kgpqh", m.reshape(S5_CHUNKS, gl, h, p), eye).reshape(
            S5_CHUNKS, gl * p, gl * h)

    bd = jnp.concatenate([pack_b(bbr), pack_b(bbi)], axis=-1).astype(BF16)
    cd = jnp.concatenate([pack_c(c_re.astype(F32)), -pack_c(c_im.astype(F32))], axis=1).astype(BF16)

    def slabs(v):
        return v.reshape(S5_CHUNKS, -1, LANES).transpose(1, 0, 2)

    return bd, cd, slabs(lr), slabs(li)


def _filt_kernel(f_ref, w1_ref, b1_ref, q1_ref, w2_ref, b2_ref, q2_ref, w3_ref, dec_ref,
                 bias_ref, o_ref, *, tq):
    f = f_ref[...]
    h = jnp.sin(q1_ref[...] * (_dot_hi(f, w1_ref[...]) + b1_ref[...]))
    h = jnp.sin(q2_ref[...] * (_dot_hi(h, w2_ref[...]) + b2_ref[...]))
    h = jnp.concatenate([_dot_3pass(h, w3_ref[0]), _dot_3pass(h, w3_ref[1])], axis=0)
    c = bias_ref.shape[-1]
    hid = f.shape[-1] // 2
    t = jnp.concatenate([f[:, 0:1], f[:, hid:hid + 1]], axis=0)
    lag = pl.program_id(0) * tq + lax.broadcasted_iota(jnp.int32, (tq, 1), 0)
    fwd = h[:, :c] * jnp.exp(-t * jnp.abs(dec_ref[0:1, :]))
    bwd = h[:, c:] * jnp.exp(-t * jnp.abs(dec_ref[1:2, :]))
    fwd = jnp.where(lag == 0, fwd + bias_ref[...], fwd)
    bwd = jnp.where(lag == 0, 0.0, bwd)
    _store_padded(o_ref, 0, fwd)
    _store_padded(o_ref, 1, bwd)


def _filter_taps(feats, w1, b1, q1, w2, b2, q2, w3, decay, bias, tq=512):
    n_lat, fp = feats.shape
    hid = w2.shape[0]
    assert fp == hid and 2 * hid == LANES
    c = bias.shape[-1]
    rows = tq // DFT_N2 * GROUP_PITCH
    feats2 = feats.reshape(n_lat // tq, 2, tq // 2, fp).transpose(0, 2, 1, 3).reshape(
        n_lat // 2, 2 * fp)
    pair = lambda m: jnp.kron(jnp.eye(2, dtype=m.dtype), m)
    twice = lambda v: jnp.concatenate([v, v], axis=-1)
    zero = jnp.zeros_like(w3)
    w3_halves = jnp.stack([jnp.concatenate([w3, zero], axis=0),
                           jnp.concatenate([zero, w3], axis=0)])
    const = lambda r: (0, 0)
    return pl.pallas_call(
        functools.partial(_filt_kernel, tq=tq),
        grid=(n_lat // tq,),
        in_specs=[pl.BlockSpec((tq // 2, LANES), lambda r: (r, 0)),
                  pl.BlockSpec((LANES, LANES), const), pl.BlockSpec((1, LANES), const),
                  pl.BlockSpec((1, LANES), const),
                  pl.BlockSpec((LANES, LANES), const), pl.BlockSpec((1, LANES), const),
                  pl.BlockSpec((1, LANES), const),
                  pl.BlockSpec((2, LANES, 2 * c), lambda r: (0, 0, 0)),
                  pl.BlockSpec((2, c), const),
                  pl.BlockSpec((1, c), const)],
        out_specs=pl.BlockSpec((2, rows, c), lambda r: (0, r, 0)),
        out_shape=jax.ShapeDtypeStruct((2, n_lat // DFT_N2 * GROUP_PITCH, c), F32),
        compiler_params=_cparams(("parallel",)),
        name="hy_filter",
    )(feats2, pair(w1), twice(b1), twice(q1), pair(w2), twice(b2), twice(q2), w3_halves, decay,
      bias)


def _conv3(cur_ref, prev_ref, next_ref, w_ref, b_ref, i, last, tm):
    row = lax.broadcasted_iota(jnp.int32, (tm, 1), 0)
    cur = cur_ref[0].astype(F32)
    halo = prev_ref.shape[1]
    prev_row = jnp.where(i == 0, 0.0, prev_ref[0, halo - 1:halo, :].astype(F32))
    next_row = jnp.where(i == last, 0.0, next_ref[0, 0:1, :].astype(F32))
    up = jnp.where(row == 0, prev_row, pltpu.roll(cur, 1, 0))
    dn = jnp.where(row == tm - 1, next_row, pltpu.roll(cur, tm - 1, 0))
    w = w_ref[...]
    return up * w[0:1] + cur * w[1:2] + dn * w[2:3] + b_ref[...]


CONV_HALO = 16


def _conv3_specs(tm, c, n, col_block):
    hb = tm // CONV_HALO
    nhb = n // CONV_HALO
    return [pl.BlockSpec((1, tm, c), lambda b, i: (b, i, col_block)),
            pl.BlockSpec((1, CONV_HALO, c),
                         lambda b, i: (b, jnp.maximum(i * hb - 1, 0), col_block)),
            pl.BlockSpec((1, CONV_HALO, c),
                         lambda b, i: (b, jnp.minimum((i + 1) * hb, nhb - 1), col_block))]


def _sconv_kernel(x1_ref, x1p_ref, x1n_ref, v_ref, vp_ref, vn_ref, w1_ref, wv_ref, b1_ref, bv_ref,
                  u_ref, *, tm):
    i = pl.program_id(1)
    last = pl.num_programs(1) - 1
    x1 = _conv3(x1_ref, x1p_ref, x1n_ref, w1_ref, b1_ref, i, last, tm)
    v = _conv3(v_ref, vp_ref, vn_ref, wv_ref, bv_ref, i, last, tm)
    _store_padded(u_ref, 0, x1 * v)


def _short_conv(p_x, conv_w, conv_b, col0, c, tm):
    bsz, n, _ = p_x.shape
    cb0 = col0 // c
    part = lambda p, shape: pl.BlockSpec(shape, lambda b, i: (0, p))
    return pl.pallas_call(
        functools.partial(_sconv_kernel, tm=tm),
        grid=(bsz, n // tm),
        in_specs=(_conv3_specs(tm, c, n, cb0 + 1) + _conv3_specs(tm, c, n, cb0 + 2)
                  + [part(1, (3, c)), part(2, (3, c)), part(1, (1, c)), part(2, (1, c))]),
        out_specs=pl.BlockSpec((1, tm // DFT_N2 * GROUP_PITCH, c), lambda b, i: (b, i, 0)),
        out_shape=jax.ShapeDtypeStruct((bsz, n // DFT_N2 * GROUP_PITCH, c), F32),
        compiler_params=_cparams(("parallel", "parallel")),
        name="short_conv",
    )(p_x, p_x, p_x, p_x, p_x, p_x, conv_w, conv_w, conv_b, conv_b)


def _dft_tables(n1):
    n = n1 * DFT_N2
    k1 = jnp.arange(n1, dtype=jnp.int32)
    n2 = jnp.arange(DFT_N2, dtype=jnp.int32)

    def unit(ph, period):
        ang = (ph % period).astype(F32) * F32(2.0 * math.pi / period)
        return jnp.cos(ang), -jnp.sin(ang)

    ar, ai = unit(k1[:, None] * k1[None, :], n1)
    tr, ti = unit(n2[:, None] * k1[None, :], n)
    gr = ar[None] * tr[:, :, None] - ai[None] * ti[:, :, None]
    gi = ar[None] * ti[:, :, None] + ai[None] * tr[:, :, None]
    h = n1 // 2
    g1c = jnp.concatenate([jnp.concatenate([gr[:, :, :h], -gi[:, :, :h]], axis=2),
                           jnp.concatenate([gi[:, :, :h], gr[:, :, :h]], axis=2)], axis=1)
    m = jnp.arange(h, dtype=jnp.int32)

    def fold(g):
        up = jnp.take(g, n1 - 1 - m, axis=2)
        up0 = jnp.take(g[0], (n1 - m) % n1, axis=1) * (m != 0).astype(F32)
        return jnp.concatenate([g[:, :, :h], up.at[0].set(up0)], axis=2)

    g1f = jnp.concatenate([fold(gr), fold(gi)], axis=1)
    fr, fi = unit(n2[:, None] * n2[None, :], DFT_N2)
    f2 = jnp.concatenate([jnp.concatenate([fr, -fi], axis=1),
                          jnp.concatenate([fi, fr], axis=1)], axis=0)
    f2c = jnp.concatenate([jnp.concatenate([fr, fi], axis=1),
                           jnp.concatenate([-fi, fr], axis=1)], axis=0) / n
    return g1c.astype(BF16), g1f.astype(BF16), f2.astype(BF16), f2c.astype(BF16)


def _stage1(gather, g_ref, a_ref, cn, n1, pitch):
    def trip(chunk, carry):
        for q in range(cn):
            n2 = chunk * cn + q
            a = _dot(g_ref[n2], gather(n2).astype(BF16))
            a_ref[0, pl.ds(n2, n1, stride=pitch), :] = a[:n1]
            a_ref[1, pl.ds(n2, n1, stride=pitch), :] = a[n1:]
        return carry
    lax.fori_loop(0, DFT_N2 // cn, trip, 0)


def _a_rows(chunk, ck, q, pitch):
    return pl.ds(pl.multiple_of(chunk * (ck * pitch), SUBLANES) + q * pitch, DFT_N2)


def _load_a_chunk(a_ref, chunk, ck, pitch):
    tiles = []
    for q in range(ck):
        rows = _a_rows(chunk, ck, q, pitch)
        tiles.append(jnp.concatenate([a_ref[0, rows, :], a_ref[1, rows, :]], axis=0).astype(BF16))
    return jnp.concatenate(tiles, axis=1)


def _spec_kernel(t_ref, g_ref, f2_ref, o_ref, a_ref, *, n1, cn, ck, pitch):
    s = pl.program_id(1)
    h = n1 // 2

    @pl.when(s == 0)
    def _():
        def gather(n2):
            back = (DFT_N2 - n2) % DFT_N2
            return jnp.concatenate([t_ref[0, pl.ds(n2, h, stride=GROUP_PITCH), :],
                                    t_ref[1, pl.ds(back, h, stride=GROUP_PITCH), :]], axis=0)
        _stage1(gather, g_ref, a_ref, cn, n1, pitch)

    @pl.when(s >= 1)
    def _():
        o_ref[0] = _dot(f2_ref[...], _load_a_chunk(a_ref, s - 1, ck, pitch)).astype(o_ref.dtype)


def _conv_kernel(u_ref, g1_ref, kf_ref, f2_ref, f2c_ref, y_ref, a_ref,
                 *, n1, cn, ck, pitch, np2):
    s = pl.program_id(1)
    h = n1 // 2

    @pl.when(s == 0)
    def _():
        y_ref[...] = jnp.zeros_like(y_ref)

        def gather(n2):
            idx = pl.ds(n2, h, stride=GROUP_PITCH)
            return jnp.concatenate([u_ref[0, idx, :], u_ref[1, idx, :]], axis=0)
        _stage1(gather, g1_ref, a_ref, cn, n1, pitch)

    @pl.when(jnp.logical_and(s >= 1, s <= np2))
    def _():
        chunk = s - 1
        x = _dot(f2_ref[...], _load_a_chunk(a_ref, chunk, ck, pitch))
        xr, xi = x[:DFT_N2], x[DFT_N2:]
        kr, ki = kf_ref[0, :DFT_N2, :].astype(F32), kf_ref[0, DFT_N2:, :].astype(F32)
        y = jnp.concatenate([xr * kr - xi * ki, xr * ki + xi * kr], axis=0)
        b = _dot(f2c_ref[...], y.astype(BF16))
        for q in range(ck):
            rows = _a_rows(chunk, ck, q, pitch)
            a_ref[0, rows, :] = b[:DFT_N2, q * LANES:(q + 1) * LANES]
            a_ref[1, rows, :] = b[DFT_N2:, q * LANES:(q + 1) * LANES]

    @pl.when(s > np2)
    def _():
        def trip(chunk, carry):
            for q in range(cn):
                n2 = chunk * cn + q
                idx = pl.ds(n2, n1, stride=pitch)
                b = jnp.concatenate([a_ref[0, idx, :], a_ref[1, idx, :]], axis=0)
                y = lax.dot_general(g1_ref[n2], b.astype(BF16), (((0,), (0,)), ((), ())),
                                    preferred_element_type=F32)
                out = pl.ds(n2, h, stride=GROUP_PITCH)
                y_ref[0, out, :] = y[:h]
                y_ref[1, out, :] = y[h:]
            return carry
        lax.fori_loop(0, DFT_N2 // cn, trip, 0)


DFT_UNROLL = 32
A_PITCH = DFT_N2 + SUBLANES


def _filter_spectrum(taps, g1f, f2):
    _, rows, c = taps.shape
    n1 = 2 * rows // GROUP_PITCH
    ck = min(32, n1)
    np2 = n1 // ck
    pitch = A_PITCH
    return pl.pallas_call(
        functools.partial(_spec_kernel, n1=n1, cn=DFT_UNROLL, ck=ck, pitch=pitch),
        grid=(c // LANES, 1 + np2),
        in_specs=[pl.BlockSpec((2, rows, LANES), lambda j, s: (0, 0, j),
                               pipeline_mode=pl.Buffered(1)),
                  pl.BlockSpec((DFT_N2, 2 * n1, n1), lambda j, s: (0, 0, 0),
                               pipeline_mode=pl.Buffered(1)),
                  pl.BlockSpec((2 * DFT_N2, 2 * DFT_N2), lambda j, s: (0, 0))],
        out_specs=pl.BlockSpec((1, 2 * DFT_N2, ck * LANES),
                               lambda j, s: (j, 0, jnp.maximum(s - 1, 0))),
        out_shape=jax.ShapeDtypeStruct((c // LANES, 2 * DFT_N2, n1 * LANES), BF16),
        scratch_shapes=[pltpu.VMEM((2, n1 * pitch, LANES), F32)],
        compiler_params=_cparams(("parallel", "arbitrary")),
        name="hy_spectrum",
    )(taps, g1f, f2)


def _long_conv(u, kf, g1c, f2, f2c):
    bsz, rows, c = u.shape
    assert bsz == 2, "the two batch elements are packed as one complex signal"
    n1 = 2 * rows // GROUP_PITCH
    ck = min(16, n1)
    np2 = n1 // ck
    pitch = A_PITCH
    once = dict(pipeline_mode=pl.Buffered(1))
    return pl.pallas_call(
        functools.partial(_conv_kernel, n1=n1, cn=DFT_UNROLL, ck=ck, pitch=pitch, np2=np2),
        grid=(c // LANES, np2 + 2),
        in_specs=[
            pl.BlockSpec((2, rows, LANES), lambda j, s: (0, 0, j), **once),
            pl.BlockSpec((DFT_N2, 2 * n1, n1), lambda j, s: (0, 0, 0), **once),
            pl.BlockSpec((1, 2 * DFT_N2, ck * LANES),
                         lambda j, s: (j, 0, jnp.clip(s - 1, 0, np2 - 1))),
            pl.BlockSpec((2 * DFT_N2, 2 * DFT_N2), lambda j, s: (0, 0)),
            pl.BlockSpec((2 * DFT_N2, 2 * DFT_N2), lambda j, s: (0, 0)),
        ],
        out_specs=pl.BlockSpec((2, rows, LANES), lambda j, s: (0, 0, j), **once),
        out_shape=jax.ShapeDtypeStruct((2, rows, c), F32),
        scratch_shapes=[pltpu.VMEM((2, n1 * pitch, LANES), F32)],
        compiler_params=_cparams(("parallel", "arbitrary")),
        name="hy_conv",
    )(u, g1c, kf, f2, f2c)


def _mixer_kernel(ya_ref, yb_ref, u_ref, h0_ref, h0p_ref, h0n_ref, cw_ref, cb_ref, yc_ref, gs_ref,
                  gh_ref, x_ref, rows_ref, cols_ref, d_ref, gate_ref, gf_ref, shf_ref, scf_ref,
                  wglu_ref, why_ref, wout_ref, o_ref, hf_ref, *, tm, cw):
    first = pl.program_id(0) == 0
    y = (jnp.where(first, ya_ref[0], yb_ref[0]).astype(F32)
         + jnp.where(first, ya_ref[1], yb_ref[1]).astype(F32)
         + u_ref[0].astype(F32) * d_ref[...])
    h = jax.nn.gelu(y, approximate=True).astype(BF16)
    y_conv = jnp.concatenate(
        [yc_ref[0, g * GROUP_PITCH:g * GROUP_PITCH + DFT_N2, :] for g in range(tm // DFT_N2)],
        axis=0)
    x0 = _conv3(h0_ref, h0p_ref, h0n_ref, cw_ref, cb_ref, pl.program_id(1),
                pl.num_programs(1) - 1, tm)
    hx = (x0 * y_conv).astype(BF16)
    d = o_ref.shape[-1]
    merged = []
    for c in range(d // cw):
        cols = slice(c * cw, (c + 1) * cw)
        glu_v = _dot(h, wglu_ref[:, cols])
        glu_g = _dot(h, wglu_ref[:, d + c * cw:d + (c + 1) * cw])
        branch_hy = _dot(hx, why_ref[:, cols])
        m = (jax.nn.sigmoid(gs_ref[0, :, cols].astype(F32)) * (glu_v * jax.nn.sigmoid(glu_g))
             + jax.nn.sigmoid(gh_ref[0, :, cols].astype(F32)) * branch_hy)
        merged.append(m.astype(BF16))
    merged = jnp.concatenate(merged, axis=-1)
    dh = d // 2
    rp, cp = _pos_block(rows_ref[0], cols_ref[...], tm)
    for c in range(d // cw):
        cols = slice(c * cw, (c + 1) * cw)
        pos = rp[:, cols] if (c + 1) * cw <= dh else cp[:, c * cw - dh:(c + 1) * cw - dh]
        o_ref[0, :, cols] = (x_ref[0, :, cols] + pos
                             + gate_ref[0, :, cols] * _dot(merged, wout_ref[:, cols]))
    hf_ref[0] = _rms_mod(o_ref[0], gf_ref[...], shf_ref[0], scf_ref[0]).astype(BF16)


def _mixer(y_ssm, p_x, conv_w, conv_b, y_conv, x, pos_tabs, s5_d, gate, g_ffn, shift_ffn, scale_ffn,
           w_glu, w_hy_out, w_out, s5w, gate_col0, tm):
    bsz, n, d = x.shape
    assert bsz == 2, "the S5 readouts arrive as one array per batch element"
    hyw = w_hy_out.shape[0]
    rows_tab, cols_tab = pos_tabs
    nr = tm // GRID_W
    dh = d // 2
    ni = n // tm
    g0 = gate_col0 // d
    cw = min(256, dh)
    resident = dict(pipeline_mode=pl.Buffered(1))
    tok = lambda width, col: pl.BlockSpec((1, tm, width), lambda b, i: (b, i, col))
    return pl.pallas_call(
        functools.partial(_mixer_kernel, tm=tm, cw=cw),
        grid=(bsz, ni),
        in_specs=[
            pl.BlockSpec((2, tm, s5w), lambda b, i: (0, jnp.where(b == 0, i, ni - 1), 0)),
            pl.BlockSpec((2, tm, s5w), lambda b, i: (0, jnp.where(b == 0, 0, i), 0)),
            tok(s5w, 0),
            *_conv3_specs(tm, hyw, n, s5w // hyw),
            pl.BlockSpec((3, hyw), lambda b, i: (0, 0)),
            pl.BlockSpec((1, hyw), lambda b, i: (0, 0)),
            pl.BlockSpec((1, tm // DFT_N2 * GROUP_PITCH, hyw), lambda b, i: (b, i, 0)),
            tok(d, g0), tok(d, g0 + 1), tok(d, 0),
            pl.BlockSpec((1, nr, dh), lambda b, i: (i, 0, 0)),
            pl.BlockSpec((GRID_W, dh), lambda b, i: (0, 0)),
            pl.BlockSpec((1, s5w), lambda b, i: (0, 0)),
            pl.BlockSpec((1, 1, d), lambda b, i: (b, 0, 0)),
            pl.BlockSpec((1, d), lambda b, i: (0, 0)),
            pl.BlockSpec((1, 1, d), lambda b, i: (b, 0, 0)),
            pl.BlockSpec((1, 1, d), lambda b, i: (b, 0, 0)),
            pl.BlockSpec((s5w, 2 * d), lambda b, i: (0, 0), **resident),
            pl.BlockSpec((hyw, d), lambda b, i: (0, 0), **resident),
            pl.BlockSpec((d, d), lambda b, i: (0, 0), **resident),
        ],
        out_specs=[pl.BlockSpec((1, tm, d), lambda b, i: (b, i, 0)),
                   pl.BlockSpec((1, tm, d), lambda b, i: (b, i, 0))],
        out_shape=[jax.ShapeDtypeStruct((bsz, n, d), F32),
                   jax.ShapeDtypeStruct((bsz, n, d), BF16)],
        compiler_params=_cparams(("arbitrary", "arbitrary")),
        name="mixer",
    )(y_ssm[0], y_ssm[1], p_x, p_x, p_x, p_x, conv_w, conv_b, y_conv, p_x, p_x, x,
      rows_tab.reshape(-1, nr, dh), cols_tab, s5_d, gate, g_ffn, shift_ffn, scale_ffn,
      w_glu, w_hy_out, w_out)


def _ffn_kernel(x_ref, h_ref, gate_ref, wa_ref, wb_ref, wo_ref, nf_ref, o_ref, acc_ref):
    j = pl.program_id(2)

    @pl.when(j == 0)
    def _():
        acc_ref[...] = jnp.zeros_like(acc_ref)

    h = h_ref[0]
    act = _silu(_dot(h, wa_ref[...])) * _dot(h, wb_ref[...])
    acc_ref[...] += _dot(act.astype(BF16), wo_ref[...])

    @pl.when(j == pl.num_programs(2) - 1)
    def _():
        xo = x_ref[0] + gate_ref[0] * acc_ref[...]
        o_ref[0] = xo * lax.rsqrt(jnp.mean(xo * xo, axis=-1, keepdims=True) + EPS) * nf_ref[...]


def _ffn(x, h, gate, w_in, w_out, norm_f, tm, tf):
    bsz, n, d = x.shape
    dff = w_out.shape[0]
    nj = dff // tf
    tok = pl.BlockSpec((1, tm, d), lambda b, i, j: (b, i, 0))
    return pl.pallas_call(
        _ffn_kernel,
        grid=(bsz, n // tm, nj),
        in_specs=[tok, tok,
                  pl.BlockSpec((1, 1, d), lambda b, i, j: (b, 0, 0)),
                  pl.BlockSpec((d, tf), lambda b, i, j: (0, j)),
                  pl.BlockSpec((d, tf), lambda b, i, j: (0, nj + j)),
                  pl.BlockSpec((tf, d), lambda b, i, j: (j, 0)),
                  pl.BlockSpec((1, d), lambda b, i, j: (0, 0))],
        out_specs=pl.BlockSpec((1, tm, d), lambda b, i, j: (b, i, 0)),
        out_shape=jax.ShapeDtypeStruct((bsz, n, d), F32),
        scratch_shapes=[pltpu.VMEM((tm, d), F32)],
        compiler_params=_cparams(("parallel", "parallel", "arbitrary")),
        name="ffn",
    )(x, h, gate, w_in, w_in, w_out, norm_f)


def _pos_tables(n_rows, d):
    quarter = d // 4
    omega = 10000.0 ** (-jnp.arange(quarter, dtype=F32) / quarter)
    ar = jnp.arange(n_rows, dtype=F32)[:, None] * omega
    ac = jnp.arange(GRID_W, dtype=F32)[:, None] * omega
    return (jnp.concatenate([jnp.sin(ar), jnp.cos(ar)], axis=-1),
            jnp.concatenate([jnp.sin(ac), jnp.cos(ac)], axis=-1))


def _filter_features(n_lat, width):
    pos = jnp.arange(n_lat, dtype=F32)
    t = pos / float(max(n_lat - 1, 1))
    w = 2.0 * math.pi * pos / n_lat
    bands = jnp.linspace(1e-4, HY_BANDS - 1, HY_BANDS, dtype=F32)
    feats = jnp.concatenate([t[:, None], jnp.cos(w[:, None] * bands), -jnp.sin(w[:, None] * bands)],
                            axis=-1)
    return jnp.pad(feats, ((0, 0), (0, width - feats.shape[1])))


def kernel(x, c, ctx, c_ctx, w_ada, b_ada, norm_mix, w_in, s5_a_re, s5_a_im, s5_log_dt,
           s5_b_re, s5_b_im, s5_c_re, s5_c_im, s5_d, w_glu, hy_conv_w, hy_conv_b,
           hy_f1_w, hy_f1_b, hy_f1_freq, hy_f2_w, hy_f2_b, hy_f2_freq, hy_f3_w, hy_decay,
           hy_bias, w_hy_out, w_out, norm_ffn, w_ffn_in, w_ffn_out, norm_f):
    bsz, n_lat, d = x.shape
    depth = w_ada.shape[0]
    assert depth == 1, "the context stream is only advanced for the single-layer trunk"
    l = 0
    s5w = s5_d.shape[-1]
    hyw = hy_bias.shape[-1]
    n_ctx = ctx.shape[1]
    tm = 512
    tn = min(1024, d)
    tb = 256
    assert n_ctx % tb == 0 and n_lat % tm == 0 and s5w == S5_CHUNKS * LANES

    pos_tabs = _pos_tables(n_lat // GRID_W, d)

    cond_t = jnp.zeros((d, SUBLANES), F32).at[:, :bsz].set(c.T).at[:, bsz].set(c_ctx)
    ada = _ada(cond_t, w_ada[l], b_ada[l][None], bsz + 1, tn=min(1024, d))

    def vec(row0, rows, part):
        v = ada[row0:row0 + rows, part * d:(part + 1) * d]
        return jnp.broadcast_to(v, (bsz, d))[:, None, :]

    shift_mix, scale_mix, gate_mix = vec(0, bsz, 0), vec(0, bsz, 1), vec(0, bsz, 2)
    shift_ffn, scale_ffn, gate_ffn = vec(0, bsz, 3), vec(0, bsz, 4), vec(0, bsz, 5)
    cshift_mix, cscale_mix = vec(bsz, 1, 0), vec(bsz, 1, 1)

    w_in_b = w_in[l].astype(BF16)
    g_mix = norm_mix[l][None]
    u_c = _ctx_proj(ctx, g_mix, cshift_mix, cscale_mix, w_in_b, s5w, tb)
    p_x = _in_proj_latent(x, pos_tabs, g_mix, shift_mix, scale_mix, w_in_b, min(1024, n_lat), tn)

    packed = [_s5_params(s5_a_re[l, k], s5_a_im[l, k], s5_log_dt[l, k], s5_b_re[l, k],
                         s5_b_im[l, k], s5_c_re[l, k], s5_c_im[l, k]) for k in range(2)]
    bd, cd, lam_r, lam_i = (jnp.stack(t) for t in zip(*packed))
    y_ssm = _s5(u_c, p_x, bd, cd, lam_r, lam_i, s5w, tb)

    fp = 64
    feats = _filter_features(n_lat, fp)
    w1 = jnp.pad(hy_f1_w[l], ((0, fp - hy_f1_w.shape[1]), (0, 0)))
    taps = _filter_taps(feats, w1, hy_f1_b[l][None], hy_f1_freq[l][None], hy_f2_w[l],
                        hy_f2_b[l][None], hy_f2_freq[l][None], hy_f3_w[l], hy_decay[l],
                        hy_bias[l][None])
    g1c, g1f, f2, f2c = _dft_tables(2 * n_lat // DFT_N2)
    kf = _filter_spectrum(taps, g1f, f2)
    u_hy = _short_conv(p_x, hy_conv_w[l], hy_conv_b[l][None], s5w, hyw, tm)
    y_conv = _long_conv(u_hy, kf, g1c, f2, f2c)

    x1, h_ffn = _mixer(y_ssm, p_x, hy_conv_w[l], hy_conv_b[l][None], y_conv, x, pos_tabs,
                       s5_d[l][None], gate_mix, norm_ffn[l][None], shift_ffn, scale_ffn,
                       w_glu[l].astype(BF16), w_hy_out[l].astype(BF16), w_out[l].astype(BF16),
                       s5w, s5w + 3 * hyw, 256)

    return _ffn(x1, h_ffn, gate_ffn, w_ffn_in[l].astype(BF16), w_ffn_out[l].astype(BF16),
                norm_f[None], tm, 512)
```
